```python
import jax, jax.numpy as jnp
from jax import lax
import numpy as np

D_MODEL = 1024
BATCH = 16
SEQ = 256
DEPTH = 4
DEC_BATCH = 2
DEC_SEQ = 1024
PAST_LEN = 512

GRID_W = 64
N_DIR = 2
N_BRANCH = 3
N_ADA = 9
D_FF = 2816
CONV_K = 3
EPS = 1e-6

S5_WIDTH = 512
S5_GROUP = 16
S5_GROUPS = S5_WIDTH // S5_GROUP
S5_STATE = 64

SSD_WIDTH = 512
SSD_HEADDIM = 64
SSD_HEADS = SSD_WIDTH // SSD_HEADDIM
SSD_GROUPS = 2
SSD_STATE = 64
SSD_CHUNK = 128
SSD_CONV_DIM = SSD_WIDTH + 2 * SSD_GROUPS * SSD_STATE

DN_HEADS = 4
DN_DK = 128
DN_DV = 128
DN_QK = DN_HEADS * DN_DK
DN_V = DN_HEADS * DN_DV
DN_CHUNK = 64
DN_CONV_DIM = 2 * DN_QK + DN_V

IN_SEGMENTS = (S5_WIDTH, SSD_WIDTH, SSD_CONV_DIM, N_DIR * SSD_HEADS, DN_CONV_DIM, N_DIR * DN_HEADS, N_DIR * DN_HEADS, DN_V, N_BRANCH * D_MODEL)
IN_WIDTH = sum(IN_SEGMENTS)
IN_SPLITS = tuple(int(s) for s in np.cumsum(IN_SEGMENTS)[:-1])

kernel_name = 'hybrid_s5_ssd_deltanet_diffusion_step'


def _rmsnorm(x, g):
    xf = x.astype(jnp.float32)
    y = xf * lax.rsqrt(jnp.mean(xf * xf, axis=-1, keepdims=True) + EPS)
    return (y * g.astype(jnp.float32)).astype(x.dtype)


def _l2norm(x):
    return x * lax.rsqrt(jnp.sum(x * x, axis=-1, keepdims=True) + EPS)


def _rev(t):
    return jnp.flip(t, axis=1)


def _short_conv(x, w, grid):
    bsz, seq, ch = x.shape
    width = GRID_W if grid else seq
    rows = seq // width
    pad = CONV_K // 2
    xp = jnp.pad(x.reshape(bsz, rows, width, ch), ((0, 0), (0, 0), (pad, pad), (0, 0)))
    y = xp[:, :, 0:width] * w[0]
    for k in range(1, CONV_K):
        y = y + xp[:, :, k:k + width] * w[k]
    return y.reshape(bsz, seq, ch)


def _complex_affine_combine(e1, e2):
    a1r, a1i, b1r, b1i = e1
    a2r, a2i, b2r, b2i = e2
    return (a2r * a1r - a2i * a1i,
            a2r * a1i + a2i * a1r,
            a2r * b1r - a2i * b1i + b2r,
            a2r * b1i + a2i * b1r + b2i)


def _s5_scan(u, lam_re, lam_im, log_dt, b_re, b_im, c_re, c_im, h0_re, h0_im):
    dt = jnp.exp(log_dt)[:, None]
    mag = jnp.exp(lam_re * dt)
    lb_re = mag * jnp.cos(lam_im * dt)
    lb_im = mag * jnp.sin(lam_im * dt)
    den = lam_re * lam_re + lam_im * lam_im
    cr = ((lb_re - 1.0) * lam_re + lb_im * lam_im) / den
    ci = (lb_im * lam_re - (lb_re - 1.0) * lam_im) / den
    bb_re = cr[..., None] * b_re - ci[..., None] * b_im
    bb_im = cr[..., None] * b_im + ci[..., None] * b_re
    bu_re = jnp.einsum('blgi,gpi->blgp', u, bb_re)
    bu_im = jnp.einsum('blgi,gpi->blgp', u, bb_im)
    bu_re = bu_re.at[:, 0].add(lb_re * h0_re - lb_im * h0_im)
    bu_im = bu_im.at[:, 0].add(lb_re * h0_im + lb_im * h0_re)
    a_re = jnp.broadcast_to(lb_re, bu_re.shape)
    a_im = jnp.broadcast_to(lb_im, bu_im.shape)
    _, _, h_re, h_im = lax.associative_scan(_complex_affine_combine, (a_re, a_im, bu_re, bu_im), axis=1)
    y = jnp.einsum('blgp,gip->blgi', h_re, c_re) - jnp.einsum('blgp,gip->blgi', h_im, c_im)
    return y, h_re[:, -1], h_im[:, -1]


def _ssd_scan(x, dt, a, bm, cm, h0):
    bsz, seq, nh, hp = x.shape
    ns = bm.shape[-1]
    nc = seq // SSD_CHUNK
    xc = (x * dt[..., None]).reshape(bsz, nc, SSD_CHUNK, nh, hp)
    bc = bm.reshape(bsz, nc, SSD_CHUNK, nh, ns)
    cc = cm.reshape(bsz, nc, SSD_CHUNK, nh, ns)
    acum = jnp.cumsum((dt * a).reshape(bsz, nc, SSD_CHUNK, nh), axis=2)
    idx = jnp.arange(SSD_CHUNK)
    lower = (idx[:, None] >= idx[None, :])[None, None, :, :, None]
    seg = jnp.exp(jnp.where(lower, acum[:, :, :, None, :] - acum[:, :, None, :, :], -jnp.inf))
    scores = jnp.einsum('bcihn,bcjhn->bcijh', cc, bc) * seg
    y_diag = jnp.einsum('bcijh,bcjhp->bcihp', scores, xc)
    to_end = jnp.exp(acum[:, :, -1:, :] - acum)
    chunk_states = jnp.einsum('bcjhn,bcjh,bcjhp->bchpn', bc, to_end, xc)
    chunk_decay = jnp.exp(acum[:, :, -1, :])

    def step(h, inp):
        s, d = inp
        return h * d[:, :, None, None] + s, h

    h_final, h_enter = lax.scan(step, h0, (jnp.moveaxis(chunk_states, 1, 0), jnp.moveaxis(chunk_decay, 1, 0)))
    h_enter = jnp.moveaxis(h_enter, 0, 1)
    y_off = jnp.einsum('bcihn,bchpn,bcih->bcihp', cc, h_enter, jnp.exp(acum))
    return (y_diag + y_off).reshape(bsz, seq, nh, hp), h_final


def _gated_delta(q, k, v, beta, g, h0):
    bsz, seq, nh, dk = q.shape
    dv = v.shape[-1]
    nc = seq // DN_CHUNK

    def chunks(t):
        t = jnp.moveaxis(t, 2, 1)
        return t.reshape(t.shape[0], t.shape[1], nc, DN_CHUNK, *t.shape[3:])

    q, k, v, beta, g = chunks(q), chunks(k), chunks(v), chunks(beta), chunks(g)
    gc = jnp.cumsum(g, axis=-1)
    idx = jnp.arange(DN_CHUNK)
    incl = idx[:, None] >= idx[None, :]
    strict = idx[:, None] > idx[None, :]
    decay = jnp.exp(jnp.where(incl, gc[..., :, None] - gc[..., None, :], -jnp.inf))
    kb = k * beta[..., None]
    m = jnp.where(strict, jnp.einsum('bhnid,bhnjd->bhnij', kb, k) * decay, 0.0)
    a_mat = jnp.eye(DN_CHUNK, dtype=m.dtype) + m
    u = lax.linalg.triangular_solve(a_mat, v * beta[..., None], left_side=True, lower=True)
    w = lax.linalg.triangular_solve(a_mat, kb * jnp.exp(gc)[..., None], left_side=True, lower=True)
    attn = jnp.einsum('bhnid,bhnjd->bhnij', q, k) * decay

    def step(s, inp):
        q_c, k_c, u_c, w_c, attn_c, gc_c = inp
        v_new = u_c - jnp.einsum('bhcd,bhde->bhce', w_c, s)
        o = (jnp.einsum('bhcd,bhde->bhce', q_c * jnp.exp(gc_c)[..., None], s)
             + jnp.einsum('bhij,bhje->bhie', attn_c, v_new))
        g_last = gc_c[..., -1]
        s = (s * jnp.exp(g_last)[..., None, None]
             + jnp.einsum('bhcd,bhce->bhde', k_c * jnp.exp(g_last[..., None] - gc_c)[..., None], v_new))
        return s, o

    xs = tuple(jnp.moveaxis(t, 2, 0) for t in (q, k, u, w, attn, gc))
    s_final, o = lax.scan(step, h0, xs)
    o = jnp.moveaxis(o, 0, 2).reshape(bsz, nh, seq, dv)
    return jnp.moveaxis(o, 1, 2), s_final


def _mixer_block(h, p, init, grid):
    f32 = jnp.float32
    bsz, seq, _ = h.shape
    proj = (h @ p['w_in']).astype(f32)
    u_s5, z, xbc, dt_raw, qkv, beta_raw, a_raw, dn_gate, gate_raw = jnp.split(proj, IN_SPLITS, axis=-1)

    u = u_s5.reshape(bsz, seq, S5_GROUPS, S5_GROUP)
    s5p = [p[n].astype(f32) for n in ('s5_lam_re', 's5_lam_im', 's5_log_dt', 's5_b_re', 's5_b_im', 's5_c_re', 's5_c_im')]
    h0r = init['s5_re'].astype(f32)
    h0i = init['s5_im'].astype(f32)
    y_f, sfr, sfi = _s5_scan(u, *[t[0] for t in s5p], h0r[:, 0], h0i[:, 0])
    y_b, sbr, sbi = _s5_scan(_rev(u), *[t[1] for t in s5p], h0r[:, 1], h0i[:, 1])
    y_s5 = (y_f + _rev(y_b) + p['s5_d'].astype(f32).reshape(S5_GROUPS, S5_GROUP) * u).reshape(bsz, seq, S5_WIDTH)
    g_s5 = jax.nn.gelu(y_s5)
    br_a = (g_s5 @ p['s5_glu'][0]) * jax.nn.sigmoid(g_s5 @ p['s5_glu'][1])

    xbc = jax.nn.silu(_short_conv(xbc, p['ssd_conv_w'].astype(f32), grid) + p['ssd_conv_b'].astype(f32))
    xs, bm, cm = jnp.split(xbc, (SSD_WIDTH, SSD_WIDTH + SSD_GROUPS * SSD_STATE), axis=-1)
    xs = xs.reshape(bsz, seq, SSD_HEADS, SSD_HEADDIM)
    rep = SSD_HEADS // SSD_GROUPS
    bm = jnp.repeat(bm.reshape(bsz, seq, SSD_GROUPS, SSD_STATE), rep, axis=2)
    cm = jnp.repeat(cm.reshape(bsz, seq, SSD_GROUPS, SSD_STATE), rep, axis=2)
    dt = jax.nn.softplus(dt_raw.reshape(bsz, seq, N_DIR, SSD_HEADS) + p['ssd_dt_bias'].astype(f32))
    a_ssd = -jnp.exp(p['ssd_a_log'].astype(f32))
    h0s = init['ssd'].astype(f32)
    ys_f, hs_f = _ssd_scan(xs, dt[:, :, 0], a_ssd[0], bm, cm, h0s[:, 0])
    ys_b, hs_b = _ssd_scan(_rev(xs), _rev(dt[:, :, 1]), a_ssd[1], _rev(bm), _rev(cm), h0s[:, 1])
    y_ssd = ys_f + _rev(ys_b) + p['ssd_d'].astype(f32)[:, None] * xs
    y_ssd = y_ssd.reshape(bsz, seq, SSD_WIDTH) * jax.nn.silu(z)
    br_b = _rmsnorm(y_ssd, p['ssd_norm_g']) @ p['ssd_w_out']

    qkv = jax.nn.silu(_short_conv(qkv, p['dn_conv_w'].astype(f32), grid))
    q, k, v = jnp.split(qkv, (DN_QK, 2 * DN_QK), axis=-1)
    q = _l2norm(q.reshape(bsz, seq, DN_HEADS, DN_DK)) * (DN_DK ** -0.5)
    k = _l2norm(k.reshape(bsz, seq, DN_HEADS, DN_DK))
    v = v.reshape(bsz, seq, DN_HEADS, DN_DV)
    beta = jax.nn.sigmoid(beta_raw.reshape(bsz, seq, N_DIR, DN_HEADS))
    g_dn = -jnp.exp(p['dn_a_log'].astype(f32)) * jax.nn.softplus(a_raw.reshape(bsz, seq, N_DIR, DN_HEADS) + p['dn_dt_bias'].astype(f32))
    h0d = init['dn'].astype(f32)
    o_f, sd_f = _gated_delta(q, k, v, beta[:, :, 0], g_dn[:, :, 0], h0d[:, 0])
    o_b, sd_b = _gated_delta(_rev(q), _rev(k), _rev(v), _rev(beta[:, :, 1]), _rev(g_dn[:, :, 1]), h0d[:, 1])
    o = _rmsnorm(o_f + _rev(o_b), p['dn_norm_g']) * jax.nn.silu(dn_gate.reshape(bsz, seq, DN_HEADS, DN_DV))
    br_c = o.reshape(bsz, seq, DN_V) @ p['dn_w_out']

    gates = jax.nn.sigmoid(gate_raw.reshape(bsz, seq, N_BRANCH, D_MODEL))
    merged = gates[:, :, 0] * br_a + gates[:, :, 1] * br_b + gates[:, :, 2] * br_c
    out = (merged @ p['w_out']).astype(h.dtype)
    states = {'s5_re': jnp.stack([sfr, sbr], axis=1), 's5_im': jnp.stack([sfi, sbi], axis=1),
              'ssd': jnp.stack([hs_f, hs_b], axis=1), 'dn': jnp.stack([sd_f, sd_b], axis=1)}
    return out, states


def _swiglu(h, wi, wo):
    a, b = jnp.split(h @ wi, 2, axis=-1)
    return (jax.nn.silu(a) * b) @ wo


def _layer(x, cond, p, init, grid):
    mod = (jax.nn.silu(cond) @ p['ada_w'] + p['ada_b']).reshape(cond.shape[0], 1, N_ADA, D_MODEL)
    sh1, sc1, g1, sh2, sc2, g2, sh3, sc3, g3 = [mod[:, :, i] for i in range(N_ADA)]
    h = _rmsnorm(x, p['norm_g'][0]) * (1.0 + sc1) + sh1
    x = x + 0.5 * g1 * _swiglu(h, p['ffn_wi'][0], p['ffn_wo'][0])
    h = _rmsnorm(x, p['norm_g'][1]) * (1.0 + sc2) + sh2
    mix, states = _mixer_block(h, p, init, grid)
    x = x + g2 * mix
    h = _rmsnorm(x, p['norm_g'][2]) * (1.0 + sc3) + sh3
    x = x + 0.5 * g3 * _swiglu(h, p['ffn_wi'][1], p['ffn_wo'][1])
    return x, states


def setup_inputs(seed: int = 0) -> dict:
    key = jax.random.key(seed)
    ks = iter(jax.random.split(key, 48))

    def nrm(shape, scale):
        return jax.random.normal(next(ks), shape, jnp.float32) * scale

    def unif(shape, lo, hi):
        return jax.random.uniform(next(ks), shape, jnp.float32, lo, hi)

    def dt_bias(shape):
        dt = jnp.exp(unif(shape, float(np.log(1e-3)), float(np.log(1e-1))))
        return dt + jnp.log(-jnp.expm1(-dt))

    n_idx = jnp.arange(S5_STATE, dtype=jnp.float32)
    s5_shape = (DEPTH, N_DIR, S5_GROUPS, S5_STATE)
    return {
        'x_prompt': nrm((BATCH, SEQ, D_MODEL), 1.0),
        'x_sample': nrm((DEC_BATCH, DEC_SEQ, D_MODEL), 1.0),
        'state_s5_re': nrm((DEC_BATCH, DEPTH, N_DIR, S5_GROUPS, S5_STATE), 0.5),
        'state_s5_im': nrm((DEC_BATCH, DEPTH, N_DIR, S5_GROUPS, S5_STATE), 0.5),
        'state_ssd': nrm((DEC_BATCH, DEPTH, N_DIR, SSD_HEADS, SSD_HEADDIM, SSD_STATE), 0.1),
        'state_dn': nrm((DEC_BATCH, DEPTH, N_DIR, DN_HEADS, DN_DK, DN_DV), 0.1),
        'c': nrm((DEC_BATCH, D_MODEL), 1.0),
        'c_ctx': nrm((D_MODEL,), 1.0),
        'ada_w': nrm((DEPTH, D_MODEL, N_ADA * D_MODEL), 0.5 * D_MODEL ** -0.5),
        'ada_b': nrm((DEPTH, N_ADA * D_MODEL), 0.02),
        'norm_g': 1.0 + nrm((DEPTH, 3, D_MODEL), 0.05),
        'ffn_wi': nrm((DEPTH, 2, D_MODEL, 2 * D_FF), D_MODEL ** -0.5),
        'ffn_wo': nrm((DEPTH, 2, D_FF, D_MODEL), D_FF ** -0.5),
        'w_in': nrm((DEPTH, D_MODEL, IN_WIDTH), D_MODEL ** -0.5),
        's5_lam_re': -0.5 + nrm(s5_shape, 0.01),
        's5_lam_im': jnp.pi * n_idx + nrm(s5_shape, 0.01),
        's5_log_dt': unif((DEPTH, N_DIR, S5_GROUPS), float(np.log(1e-3)), float(np.log(1e-1))),
        's5_b_re': nrm((DEPTH, N_DIR, S5_GROUPS, S5_STATE, S5_GROUP), (2 * S5_GROUP) ** -0.5),
        's5_b_im': nrm((DEPTH, N_DIR, S5_GROUPS, S5_STATE, S5_GROUP), (2 * S5_GROUP) ** -0.5),
        's5_c_re': nrm((DEPTH, N_DIR, S5_GROUPS, S5_GROUP, S5_STATE), (2 * S5_STATE) ** -0.5),
        's5_c_im': nrm((DEPTH, N_DIR, S5_GROUPS, S5_GROUP, S5_STATE), (2 * S5_STATE) ** -0.5),
        's5_d': nrm((DEPTH, S5_WIDTH), 1.0),
        's5_glu': nrm((DEPTH, 2, S5_WIDTH, D_MODEL), S5_WIDTH ** -0.5),
        'ssd_conv_w': nrm((DEPTH, CONV_K, SSD_CONV_DIM), CONV_K ** -0.5),
        'ssd_conv_b': nrm((DEPTH, SSD_CONV_DIM), 0.02),
        'ssd_dt_bias': dt_bias((DEPTH, N_DIR, SSD_HEADS)),
        'ssd_a_log': jnp.log(unif((DEPTH, N_DIR, SSD_HEADS), 1.0, 16.0)),
        'ssd_d': 1.0 + nrm((DEPTH, SSD_HEADS), 0.05),
        'ssd_norm_g': 1.0 + nrm((DEPTH, SSD_WIDTH), 0.05),
        'ssd_w_out': nrm((DEPTH, SSD_WIDTH, D_MODEL), SSD_WIDTH ** -0.5),
        'dn_conv_w': nrm((DEPTH, CONV_K, DN_CONV_DIM), CONV_K ** -0.5),
        'dn_dt_bias': dt_bias((DEPTH, N_DIR, DN_HEADS)),
        'dn_a_log': jnp.log(unif((DEPTH, N_DIR, DN_HEADS), 1.0, 16.0)),
        'dn_norm_g': 1.0 + nrm((DEPTH, DN_DV), 0.05),
        'dn_w_out': nrm((DEPTH, DN_V, D_MODEL), DN_V ** -0.5),
        'w_out': nrm((DEPTH, D_MODEL, D_MODEL), D_MODEL ** -0.5),
        'final_norm_g': 1.0 + nrm((D_MODEL,), 0.05),
    }


def reference(x_prompt, x_sample, state_s5_re, state_s5_im, state_ssd, state_dn, c, c_ctx,
              ada_w, ada_b, norm_g, ffn_wi, ffn_wo, w_in,
              s5_lam_re, s5_lam_im, s5_log_dt, s5_b_re, s5_b_im, s5_c_re, s5_c_im, s5_d, s5_glu,
              ssd_conv_w, ssd_conv_b, ssd_dt_bias, ssd_a_log, ssd_d, ssd_norm_g, ssd_w_out,
              dn_conv_w, dn_dt_bias, dn_a_log, dn_norm_g, dn_w_out, w_out, final_norm_g):
    bsz = x_prompt.shape[0]
    f32 = jnp.float32
    zero_init = {'s5_re': jnp.zeros((bsz, N_DIR, S5_GROUPS, S5_STATE), f32),
                 's5_im': jnp.zeros((bsz, N_DIR, S5_GROUPS, S5_STATE), f32),
                 'ssd': jnp.zeros((bsz, N_DIR, SSD_HEADS, SSD_HEADDIM, SSD_STATE), f32),
                 'dn': jnp.zeros((bsz, N_DIR, DN_HEADS, DN_DK, DN_DV), f32)}
    h_ctx = x_prompt
    h_lat = x_sample
    ctx_states = []
    for l in range(DEPTH):
        p = {'ada_w': ada_w[l], 'ada_b': ada_b[l], 'norm_g': norm_g[l], 'ffn_wi': ffn_wi[l], 'ffn_wo': ffn_wo[l],
             'w_in': w_in[l], 's5_lam_re': s5_lam_re[l], 's5_lam_im': s5_lam_im[l], 's5_log_dt': s5_log_dt[l],
             's5_b_re': s5_b_re[l], 's5_b_im': s5_b_im[l], 's5_c_re': s5_c_re[l], 's5_c_im': s5_c_im[l],
             's5_d': s5_d[l], 's5_glu': s5_glu[l], 'ssd_conv_w': ssd_conv_w[l], 'ssd_conv_b': ssd_conv_b[l],
             'ssd_dt_bias': ssd_dt_bias[l], 'ssd_a_log': ssd_a_log[l], 'ssd_d': ssd_d[l],
             'ssd_norm_g': ssd_norm_g[l], 'ssd_w_out': ssd_w_out[l], 'dn_conv_w': dn_conv_w[l],
             'dn_dt_bias': dn_dt_bias[l], 'dn_a_log': dn_a_log[l], 'dn_norm_g': dn_norm_g[l],
             'dn_w_out': dn_w_out[l], 'w_out': w_out[l]}
        h_ctx, st = _layer(h_ctx, c_ctx[None, :], p, zero_init, grid=False)
        ctx_states.append(st)
        lat_init = {'s5_re': state_s5_re[:, l], 's5_im': state_s5_im[:, l],
                    'ssd': state_ssd[:, l], 'dn': state_dn[:, l]}
        h_lat, _ = _layer(h_lat, c, p, lat_init, grid=True)
    y_prompt = _rmsnorm(h_ctx, final_norm_g)
    y_sample = _rmsnorm(h_lat, final_norm_g)
    new_state_s5_re = jnp.stack([s['s5_re'] for s in ctx_states], axis=1)
    new_state_s5_im = jnp.stack([s['s5_im'] for s in ctx_states], axis=1)
    new_state_ssd = jnp.stack([s['ssd'] for s in ctx_states], axis=1)
    new_state_dn = jnp.stack([s['dn'] for s in ctx_states], axis=1)
    return (y_prompt, y_sample, new_state_s5_re, new_state_s5_im, new_state_ssd, new_state_dn)
```

```python
import functools

import jax
import jax.numpy as jnp
import numpy as np
from jax import lax
from jax.experimental import pallas as pl
from jax.experimental.pallas import tpu as pltpu

F32 = jnp.float32
BF16 = jnp.bfloat16

D_MODEL = 1024
BATCH = 16
SEQ = 256
DEPTH = 4
DEC_BATCH = 2
DEC_SEQ = 1024
GRID_W = 64
N_DIR = 2
N_ADA = 9
D_FF = 2816
EPS = 1e-6

S5_WIDTH = 512
S5_GROUP = 16
S5_GROUPS = 32
S5_STATE = 64
S5_CHUNK = 16

SSD_WIDTH = 512
SSD_HEADDIM = 64
SSD_HEADS = 8
SSD_GROUPS = 2
SSD_STATE = 64
SSD_CHUNK = 128
SSD_CONV_DIM = 768

DN_HEADS = 4
DN_DK = 128
DN_DV = 128
DN_QK = 512
DN_V = 512
DN_CHUNK = 64
DN_CONV_DIM = 1536

IN_SEGMENTS = (512, 512, 768, 16, 1536, 8, 8, 512, 3072)
IN_SPLITS = tuple(int(s) for s in np.cumsum(IN_SEGMENTS)[:-1])

N_CTX = BATCH * SEQ
N_LAT = DEC_BATCH * DEC_SEQ
N_ROWS = N_CTX + N_LAT

COL_GATES = 0
COL_QKV = 3072
COL_U = 4608
COL_Z = 5120
COL_DNG = 5632
COL_XBC = 6144
COL_SMALL = 6912
PROJ_W = 7040
SMALL_W = 128

VMEM_LIMIT = 56 * 1024 * 1024


def _cparams(*sem):
    return pltpu.CompilerParams(dimension_semantics=sem, vmem_limit_bytes=VMEM_LIMIT)


def _sigmoid(x):
    return jax.nn.sigmoid(x)


def _silu(x):
    return x * jax.nn.sigmoid(x)


def _softplus(x):
    return jnp.maximum(x, 0.0) + jnp.log(1.0 + jnp.exp(-jnp.abs(x)))


def _bdot(a, b):
    return jnp.dot(a.astype(BF16), b.astype(BF16), preferred_element_type=F32)


def _bdot_nt(a, b):
    return lax.dot_general(a.astype(BF16), b.astype(BF16), (((1,), (1,)), ((), ())),
                           preferred_element_type=F32)


def _bdot_tn(a, b):
    return lax.dot_general(a.astype(BF16), b.astype(BF16), (((0,), (0,)), ((), ())),
                           preferred_element_type=F32)


def _split3(a):
    hi = a.astype(BF16)
    r = a - hi.astype(F32)
    mid = r.astype(BF16)
    lo = (r - mid.astype(F32)).astype(BF16)
    return hi, mid, lo


def _dot3(a, b):
    ah = a.astype(BF16)
    al = (a - ah.astype(F32)).astype(BF16)
    bh = b.astype(BF16)
    bl = (b - bh.astype(F32)).astype(BF16)
    out = jnp.dot(ah, bh, preferred_element_type=F32)
    out = out + jnp.dot(ah, bl, preferred_element_type=F32)
    out = out + jnp.dot(al, bh, preferred_element_type=F32)
    return out


def _dot_exact_lhs(t_bf16, x):
    hi, mid, lo = _split3(x)
    out = jnp.dot(t_bf16, hi, preferred_element_type=F32)
    out = out + jnp.dot(t_bf16, mid, preferred_element_type=F32)
    out = out + jnp.dot(t_bf16, lo, preferred_element_type=F32)
    return out


def _dot_exact_rhs(x, t_bf16):
    hi, mid, lo = _split3(x)
    out = jnp.dot(hi, t_bf16, preferred_element_type=F32)
    out = out + jnp.dot(mid, t_bf16, preferred_element_type=F32)
    out = out + jnp.dot(lo, t_bf16, preferred_element_type=F32)
    return out


def _norm_mod(x, g, sc, sh):
    ms = jnp.mean(x * x, axis=-1, keepdims=True)
    y = x * lax.rsqrt(ms + EPS) * g
    return y * (1.0 + sc) + sh


def _row_group(i, tm):
    nctx = N_CTX // tm
    per = DEC_SEQ // tm
    return jnp.where(i < nctx, 0, 1 + jnp.maximum(i - nctx, 0) // per)


ADA_TN = 1152


def _ada_kernel(c_ref, w_ref, b_ref, o_ref):
    c = c_ref[...]
    o_ref[...] = _bdot(_silu(c), w_ref[...]) + b_ref[...]


def _ada_mods(cond8, ada_w, ada_b):
    nj = (N_ADA * D_MODEL) // ADA_TN
    return pl.pallas_call(
        _ada_kernel,
        grid=(DEPTH, nj),
        in_specs=[
            pl.BlockSpec((8, D_MODEL), lambda l, j: (0, 0)),
            pl.BlockSpec((None, D_MODEL, ADA_TN), lambda l, j: (l, 0, j)),
            pl.BlockSpec((None, 1, ADA_TN), lambda l, j: (l, 0, j)),
        ],
        out_specs=pl.BlockSpec((None, 8, ADA_TN), lambda l, j: (l, 0, j)),
        out_shape=jax.ShapeDtypeStruct((DEPTH, 8, N_ADA * D_MODEL), F32),
        compiler_params=_cparams("parallel", "parallel"),
        name="ada_mods",
    )(cond8, ada_w, ada_b.reshape(DEPTH, 1, N_ADA * D_MODEL))


FFN_TM = 1024
FFN_TF = 256


def _ffn_kernel(x_ref, sh_ref, sc_ref, gt_ref, g_ref, wa_ref, wb_ref, wo_ref, o_ref, h_scr, acc_scr):
    i = pl.program_id(0)
    j = pl.program_id(1)
    grp = _row_group(i, FFN_TM)

    @pl.when(j == 0)
    def _():
        h = _norm_mod(x_ref[...], g_ref[...], sc_ref[pl.ds(grp, 1), :], sh_ref[pl.ds(grp, 1), :])
        h_scr[...] = h.astype(BF16)
        acc_scr[...] = jnp.zeros_like(acc_scr)

    h = h_scr[...]
    a = jnp.dot(h, wa_ref[...].astype(BF16), preferred_element_type=F32)
    b = jnp.dot(h, wb_ref[...].astype(BF16), preferred_element_type=F32)
    u = (_silu(a) * b).astype(BF16)
    acc_scr[...] += jnp.dot(u, wo_ref[...].astype(BF16), preferred_element_type=F32)

    @pl.when(j == pl.num_programs(1) - 1)
    def _():
        o_ref[...] = x_ref[...] + (0.5 * gt_ref[pl.ds(grp, 1), :]) * acc_scr[...]


def _ffn(x, mod, norm_g_row, ffn_wi, ffn_wo, layer, which):
    nf = D_FF // FFN_TF
    base = 0 if which == 0 else 6
    return pl.pallas_call(
        _ffn_kernel,
        grid=(N_ROWS // FFN_TM, nf),
        in_specs=[
            pl.BlockSpec((FFN_TM, D_MODEL), lambda i, j: (i, 0)),
            pl.BlockSpec((8, D_MODEL), lambda i, j: (0, base)),
            pl.BlockSpec((8, D_MODEL), lambda i, j: (0, base + 1)),
            pl.BlockSpec((8, D_MODEL), lambda i, j: (0, base + 2)),
            pl.BlockSpec((1, D_MODEL), lambda i, j: (0, 0)),
            pl.BlockSpec((None, None, D_MODEL, FFN_TF), lambda i, j: (layer, which, 0, j)),
            pl.BlockSpec((None, None, D_MODEL, FFN_TF), lambda i, j: (layer, which, 0, j + nf)),
            pl.BlockSpec((None, None, FFN_TF, D_MODEL), lambda i, j: (layer, which, j, 0)),
        ],
        out_specs=pl.BlockSpec((FFN_TM, D_MODEL), lambda i, j: (i, 0)),
        out_shape=jax.ShapeDtypeStruct((N_ROWS, D_MODEL), F32),
        scratch_shapes=[pltpu.VMEM((FFN_TM, D_MODEL), BF16), pltpu.VMEM((FFN_TM, D_MODEL), F32)],
        compiler_params=_cparams("parallel", "arbitrary"),
        name="ffn",
    )(x, mod, mod, mod, norm_g_row, ffn_wi, ffn_wi, ffn_wo)


INP_TM = 1024
INP_TN = 640


def _inproj_kernel(x_ref, sh_ref, sc_ref, g_ref, w_ref, o_ref, h_scr):
    i = pl.program_id(0)
    j = pl.program_id(1)
    grp = _row_group(i, INP_TM)

    @pl.when(j == 0)
    def _():
        h = _norm_mod(x_ref[...], g_ref[...], sc_ref[pl.ds(grp, 1), :], sh_ref[pl.ds(grp, 1), :])
        h_scr[...] = h.astype(BF16)

    o_ref[...] = jnp.dot(h_scr[...], w_ref[...], preferred_element_type=F32)


def _repack_w_in(w):
    u, z, xbc, dt, qkv, beta, a, dng, gates = jnp.split(w, IN_SPLITS, axis=-1)
    pad = jnp.zeros((w.shape[0], SMALL_W - 32), w.dtype)
    return jnp.concatenate([gates, qkv, u, z, dng, xbc, dt, beta, a, pad], axis=-1).astype(BF16)


def _inproj(x, mod, norm_g_row, w_packed):
    return pl.pallas_call(
        _inproj_kernel,
        grid=(N_ROWS // INP_TM, PROJ_W // INP_TN),
        in_specs=[
            pl.BlockSpec((INP_TM, D_MODEL), lambda i, j: (i, 0)),
            pl.BlockSpec((8, D_MODEL), lambda i, j: (0, 3)),
            pl.BlockSpec((8, D_MODEL), lambda i, j: (0, 4)),
            pl.BlockSpec((1, D_MODEL), lambda i, j: (0, 0)),
            pl.BlockSpec((D_MODEL, INP_TN), lambda i, j: (0, j)),
        ],
        out_specs=pl.BlockSpec((INP_TM, INP_TN), lambda i, j: (i, j)),
        out_shape=jax.ShapeDtypeStruct((N_ROWS, PROJ_W), F32),
        scratch_shapes=[pltpu.VMEM((INP_TM, D_MODEL), BF16)],
        compiler_params=_cparams("parallel", "arbitrary"),
        name="inproj",
    )(x, mod, mod, norm_g_row, w_packed)


S5_ROW = S5_CHUNK * S5_GROUP


def _s5_prep_kernel(lam_re_ref, lam_im_ref, ldt_ref, btr_ref, bti_ref, cr_ref, ci_ref, ctr_ref, cti_ref,
                    m_ref, p_ref, q_ref, a_ref):
    tau = lax.broadcasted_iota(jnp.int32, (S5_CHUNK, 1), 0).astype(F32)
    lane_t = lax.broadcasted_iota(jnp.int32, (1, S5_ROW), 1) // S5_GROUP
    mmat = jnp.zeros((S5_ROW, S5_ROW), F32)

    for d in range(N_DIR):
        lr = lam_re_ref[d:d + 1, :]
        li = lam_im_ref[d:d + 1, :]
        dt = jnp.exp(ldt_ref[d:d + 1, :])
        mag = jnp.exp(lr * dt)
        lb_re = mag * jnp.cos(li * dt)
        lb_im = mag * jnp.sin(li * dt)
        den = lr * lr + li * li
        cr = ((lb_re - 1.0) * lr + lb_im * li) / den
        ci = (lb_im * lr - (lb_re - 1.0) * li) / den
        bt_r = btr_ref[d]
        bt_i = bti_ref[d]
        bbt_r = cr * bt_r - ci * bt_i
        bbt_i = cr * bt_i + ci * bt_r
        c_r = cr_ref[d]
        c_i = ci_ref[d]

        def powtab(t):
            m = jnp.exp(t * (lr * dt))
            ang = t * (li * dt)
            return m * jnp.cos(ang), m * jnp.sin(ang)

        def outer(ar, ai, xr, xi):
            rr = ar[:, None, :] * xr[None, :, :] - ai[:, None, :] * xi[None, :, :]
            ii = ar[:, None, :] * xi[None, :, :] + ai[:, None, :] * xr[None, :, :]
            return rr.reshape(S5_ROW, S5_STATE), ii.reshape(S5_ROW, S5_STATE)

        t_in = (S5_CHUNK - 1) - tau if d == 0 else tau
        ar, ai = powtab(t_in)
        ba_r, ba_i = outer(ar, ai, bbt_r, bbt_i)
        p_ref[2 * d] = ba_r
        p_ref[2 * d + 1] = ba_i
        kt = _dot3(ba_r, ctr_ref[d]) - _dot3(ba_i, cti_ref[d])
        for s in range(S5_CHUNK):
            rows = s * S5_GROUP
            if d == 0:
                sh = kt if s == 0 else jnp.concatenate(
                    [kt[rows:, :], jnp.zeros((rows, S5_ROW), F32)], axis=0)
                mmat = mmat + jnp.where(lane_t == (S5_CHUNK - 1) - s, sh, 0.0)
            else:
                sh = kt if s == 0 else jnp.concatenate(
                    [jnp.zeros((rows, S5_ROW), F32), kt[:S5_ROW - rows, :]], axis=0)
                mmat = mmat + jnp.where(lane_t == s, sh, 0.0)
        t_out = tau + 1.0 if d == 0 else S5_CHUNK - tau
        ar, ai = powtab(t_out)
        qr, qi = outer(ar, ai, c_r, c_i)
        q_ref[2 * d] = qr
        q_ref[2 * d + 1] = -qi
        a16r, a16i = powtab(jnp.full((1, 1), float(S5_CHUNK), F32))
        a_ref[2 * d:2 * d + 1, :] = a16r
        a_ref[2 * d + 1:2 * d + 2, :] = a16i

    m_ref[...] = mmat


def _s5_prep(lam_re, lam_im, log_dt, b_re, b_im, c_re, c_im):
    tg = lambda t: jnp.swapaxes(t, 1, 2)
    lam_re_g = tg(lam_re)
    lam_im_g = tg(lam_im)
    ldt_g = tg(log_dt)[..., None]
    bt_r = jnp.swapaxes(tg(b_re), -1, -2)
    bt_i = jnp.swapaxes(tg(b_im), -1, -2)
    c_r = tg(c_re)
    c_i = tg(c_im)
    ct_r = jnp.tile(jnp.swapaxes(c_r, -1, -2), (1, 1, 1, 1, S5_CHUNK))
    ct_i = jnp.tile(jnp.swapaxes(c_i, -1, -2), (1, 1, 1, 1, S5_CHUNK))

    def spec(*tail):
        n = len(tail)
        return pl.BlockSpec((None, None) + tail, lambda l, g: (l, g) + (0,) * n)

    return pl.pallas_call(
        _s5_prep_kernel,
        grid=(DEPTH, S5_GROUPS),
        in_specs=[spec(2, 64), spec(2, 64), spec(2, 1), spec(2, 16, 64), spec(2, 16, 64),
                  spec(2, 16, 64), spec(2, 16, 64), spec(2, 64, S5_ROW), spec(2, 64, S5_ROW)],
        out_specs=[spec(S5_ROW, S5_ROW), spec(4, S5_ROW, 64), spec(4, S5_ROW, 64), spec(4, 64)],
        out_shape=[jax.ShapeDtypeStruct((DEPTH, S5_GROUPS, S5_ROW, S5_ROW), F32),
                   jax.ShapeDtypeStruct((DEPTH, S5_GROUPS, 4, S5_ROW, 64), F32),
                   jax.ShapeDtypeStruct((DEPTH, S5_GROUPS, 4, S5_ROW, 64), F32),
                   jax.ShapeDtypeStruct((DEPTH, S5_GROUPS, 4, 64), F32)],
        compiler_params=_cparams("parallel", "parallel"),
        name="s5_prep",
    )(lam_re_g, lam_im_g, ldt_g, bt_r, bt_i, c_r, c_i, ct_r, ct_i)


S5_CTX_CH = SEQ // S5_CHUNK
S5_LAT_CH = DEC_SEQ // S5_CHUNK
S5_CTX_ROWS = S5_CTX_CH * BATCH
S5_LAT_ROWS = S5_LAT_CH * DEC_BATCH
S5_ROWS = S5_CTX_ROWS + S5_LAT_ROWS


def _s5_kernel(u_ref, m_ref, pre_ref, pim_ref, qre_ref, qim_ref, are_ref, aim_ref, d_ref, h0r_ref, h0i_ref,
               y_ref, fr_ref, fi_ref, sre, sim, hfr, hfi, hbr, hbi):
    u = u_ref[...]
    ub = u.astype(BF16)
    sre[...] = jnp.dot(ub, pre_ref[...].astype(BF16), preferred_element_type=F32)
    sim[...] = jnp.dot(ub, pim_ref[...].astype(BF16), preferred_element_type=F32)
    ar = are_ref[...]
    ai = aim_ref[...]
    fwd = lax.broadcasted_iota(jnp.int32, (1, 2 * S5_STATE), 1) < S5_STATE

    def scan(base, nchunk, nseq, h_re, h_im):
        for k in range(nchunk):
            rf = base + k * nseq
            rb = base + (nchunk - 1 - k) * nseq
            hfr[rf:rf + nseq, :] = h_re
            hfi[rf:rf + nseq, :] = h_im
            hbr[rb:rb + nseq, :] = h_re
            hbi[rb:rb + nseq, :] = h_im
            s_r = jnp.where(fwd, sre[rf:rf + nseq, :], sre[rb:rb + nseq, :])
            s_i = jnp.where(fwd, sim[rf:rf + nseq, :], sim[rb:rb + nseq, :])
            h_re, h_im = ar * h_re - ai * h_im + s_r, ar * h_im + ai * h_re + s_i
        return h_re, h_im

    zero = jnp.zeros((BATCH, 2 * S5_STATE), F32)
    f_re, f_im = scan(0, S5_CTX_CH, BATCH, zero, zero)
    fr_ref[...] = f_re
    fi_ref[...] = f_im
    scan(S5_CTX_ROWS, S5_LAT_CH, DEC_BATCH, h0r_ref[...], h0i_ref[...])

    h_re = jnp.where(fwd, hfr[...], hbr[...])
    h_im = jnp.where(fwd, hfi[...], hbi[...])
    y = jnp.dot(ub, m_ref[...].astype(BF16), preferred_element_type=F32)
    y = y + _bdot_nt(h_re, qre_ref[...]) + _bdot_nt(h_im, qim_ref[...])
    y_ref[...] = y + d_ref[...] * u


def _s5_scan(u_g, mmat, p_re, p_im, q_re, q_im, a_re, a_im, d_row, h0_re, h0_im):
    def gspec(*tail):
        n = len(tail)
        return pl.BlockSpec((None,) + tail, lambda g: (g,) + (0,) * n)

    st = 2 * S5_STATE
    return pl.pallas_call(
        _s5_kernel,
        grid=(S5_GROUPS,),
        in_specs=[gspec(S5_ROWS, S5_ROW), gspec(S5_ROW, S5_ROW), gspec(S5_ROW, st), gspec(S5_ROW, st),
                  gspec(S5_ROW, st), gspec(S5_ROW, st), gspec(1, st), gspec(1, st), gspec(1, S5_ROW),
                  gspec(DEC_BATCH, st), gspec(DEC_BATCH, st)],
        out_specs=[gspec(S5_ROWS, S5_ROW), gspec(BATCH, st), gspec(BATCH, st)],
        out_shape=[jax.ShapeDtypeStruct((S5_GROUPS, S5_ROWS, S5_ROW), F32),
                   jax.ShapeDtypeStruct((S5_GROUPS, BATCH, st), F32),
                   jax.ShapeDtypeStruct((S5_GROUPS, BATCH, st), F32)],
        scratch_shapes=[pltpu.VMEM((S5_ROWS, st), F32) for _ in range(6)],
        compiler_params=_cparams("parallel"),
        name="s5_scan",
    )(u_g, mmat, p_re, p_im, q_re, q_im, a_re, a_im, d_row, h0_re, h0_im)


def _s5_to_groups(u_tok):
    uc = u_tok[:N_CTX].reshape(BATCH, S5_CTX_CH, S5_CHUNK, S5_GROUPS, S5_GROUP)
    uc = uc.transpose(3, 1, 0, 2, 4).reshape(S5_GROUPS, S5_CTX_ROWS, S5_ROW)
    ul = u_tok[N_CTX:].reshape(DEC_BATCH, S5_LAT_CH, S5_CHUNK, S5_GROUPS, S5_GROUP)
    ul = ul.transpose(3, 1, 0, 2, 4).reshape(S5_GROUPS, S5_LAT_ROWS, S5_ROW)
    return jnp.concatenate([uc, ul], axis=1)


def _s5_from_groups(y_g):
    yc = y_g[:, :S5_CTX_ROWS].reshape(S5_GROUPS, S5_CTX_CH, BATCH, S5_CHUNK, S5_GROUP)
    yc = yc.transpose(2, 1, 3, 0, 4).reshape(N_CTX, S5_WIDTH)
    yl = y_g[:, S5_CTX_ROWS:].reshape(S5_GROUPS, S5_LAT_CH, DEC_BATCH, S5_CHUNK, S5_GROUP)
    yl = yl.transpose(2, 1, 3, 0, 4).reshape(N_LAT, S5_WIDTH)
    return jnp.concatenate([yc, yl], axis=0)


def _short_conv(x, w_ref, width):
    n = x.shape[0]
    pos = lax.broadcasted_iota(jnp.int32, (n, 1), 0) % width
    xm = jnp.where(pos == 0, 0.0, pltpu.roll(x, 1, 0))
    xp = jnp.where(pos == width - 1, 0.0, pltpu.roll(x, n - 1, 0))
    return xm * w_ref[0:1, :] + x * w_ref[1:2, :] + xp * w_ref[2:3, :]


def _tri(n, lower):
    r = lax.broadcasted_iota(jnp.int32, (n, n), 0)
    c = lax.broadcasted_iota(jnp.int32, (n, n), 1)
    return (r >= c) if lower else (r <= c)


SSD_PAIR = 2 * SSD_HEADDIM


def _ssd_kernel(xbc_ref, sm_ref, smt_ref, cw_ref, cb_ref, dtb_r_ref, dtb_c_ref, alog_r_ref, alog_c_ref,
                dvec_ref, h0_ref, y_ref, hf_ref, xs_scr, bc_scr, dtc_scr, dac_scr, dar_scr, y_scr, h_scr,
                *, seq, width):
    nck = seq // SSD_CHUNK
    xc = _silu(_short_conv(xbc_ref[...], cw_ref, width) + cb_ref[...])
    xs = xc[:, :SSD_WIDTH]
    xs_scr[...] = xs
    bc_scr[...] = xc[:, SSD_WIDTH:]
    y_scr[...] = dvec_ref[...] * xs
    dt_c = _softplus(sm_ref[:, 0:16] + dtb_r_ref[...])
    dtc_scr[...] = dt_c
    dac_scr[...] = dt_c * (-jnp.exp(alog_r_ref[...]))
    da_r = _softplus(smt_ref[0:16, :] + dtb_c_ref[...]) * (-jnp.exp(alog_c_ref[...]))
    for ck in range(nck):
        dar_scr[ck] = da_r[:, ck * SSD_CHUNK:(ck + 1) * SSD_CHUNK]
    h_scr[...] = h0_ref[...]

    tril = _tri(SSD_CHUNK, True)
    triu = _tri(SSD_CHUNK, False)
    tril_b = tril.astype(BF16)
    triu_b = triu.astype(BF16)
    lo_half = lax.broadcasted_iota(jnp.int32, (1, SSD_PAIR), 1) < SSD_HEADDIM

    def chunk_step(k, carry):
        for d in range(N_DIR):
            c = k if d == 0 else nck - 1 - k
            r0 = pl.multiple_of(c * SSD_CHUNK, SSD_CHUNK)
            rows = pl.ds(r0, SSD_CHUNK)
            da_c = dac_scr[rows, :]
            da_r = dar_scr[c]
            mask = tril if d == 0 else triu
            ac = _dot_exact_lhs(tril_b if d == 0 else triu_b, da_c)
            at = _dot_exact_rhs(da_r, triu_b if d == 0 else tril_b)
            end_row = SSD_CHUNK - 1 if d == 0 else 0
            bcx = bc_scr[rows, :]
            dtc = dtc_scr[rows, :]
            gmat = []
            for g in range(SSD_GROUPS):
                bm = bcx[:, g * SSD_STATE:(g + 1) * SSD_STATE]
                cm = bcx[:, 2 * SSD_STATE + g * SSD_STATE:2 * SSD_STATE + (g + 1) * SSD_STATE]
                gmat.append((bm, cm, _bdot_nt(cm, bm)))
            for pr in range(SSD_HEADS // 2):
                xpair = xs_scr[rows, pr * SSD_PAIR:(pr + 1) * SSD_PAIR]
                ypair = jnp.zeros((SSD_CHUNK, SSD_PAIR), F32)
                for half in range(2):
                    h = 2 * pr + half
                    ln = d * SSD_HEADS + h
                    bm, cm, gm = gmat[h // (SSD_HEADS // SSD_GROUPS)]
                    col = ac[:, ln:ln + 1]
                    row = at[ln:ln + 1, :]
                    seg = jnp.where(mask, jnp.exp(jnp.where(mask, col - row, 0.0)), 0.0)
                    hmask = lo_half if half == 0 else jnp.logical_not(lo_half)
                    xm = jnp.where(hmask, xpair * dtc[:, ln:ln + 1], 0.0)
                    a_end = col[end_row:end_row + 1, :]
                    hs = h_scr[d, h]
                    ypair = ypair + _bdot(gm * seg, xm) + _bdot(cm * jnp.exp(col), hs)
                    h_scr[d, h] = hs * jnp.exp(a_end) + _bdot_tn(bm * jnp.exp(a_end - col), xm)
                y_scr[rows, pr * SSD_PAIR:(pr + 1) * SSD_PAIR] += ypair
        return carry

    lax.fori_loop(0, nck, chunk_step, 0)
    y_ref[...] = y_scr[...]
    hf_ref[...] = h_scr[...]


def _ssd(proj, small_t, row0, nseq, seq, width, conv_w, conv_b, dt_bias, a_log, dvec, h0):
    blk0 = row0 // seq
    kern = functools.partial(_ssd_kernel, seq=seq, width=width)
    full = lambda *shape: pl.BlockSpec(shape, lambda b: (0,) * len(shape))
    dtb_r = dt_bias.reshape(1, 16)
    dtb_c = dt_bias.reshape(16, 1)
    al_r = a_log.reshape(1, 16)
    al_c = a_log.reshape(16, 1)
    return pl.pallas_call(
        kern,
        grid=(nseq,),
        in_specs=[
            pl.BlockSpec((seq, SSD_CONV_DIM), lambda b: (blk0 + b, COL_XBC // SSD_CONV_DIM)),
            pl.BlockSpec((seq, SMALL_W), lambda b: (blk0 + b, COL_SMALL // SMALL_W)),
            pl.BlockSpec((32, seq), lambda b: (0, blk0 + b)),
            full(3, SSD_CONV_DIM), full(1, SSD_CONV_DIM), full(1, 16), full(16, 1), full(1, 16), full(16, 1),
            full(1, SSD_WIDTH),
            pl.BlockSpec((None, N_DIR, SSD_HEADS, SSD_STATE, SSD_PAIR), lambda b: (b, 0, 0, 0, 0)),
        ],
        out_specs=[
            pl.BlockSpec((seq, SSD_WIDTH), lambda b: (b, 0)),
            pl.BlockSpec((None, N_DIR, SSD_HEADS, SSD_STATE, SSD_PAIR), lambda b: (b, 0, 0, 0, 0)),
        ],
        out_shape=[jax.ShapeDtypeStruct((nseq * seq, SSD_WIDTH), F32),
                   jax.ShapeDtypeStruct((nseq, N_DIR, SSD_HEADS, SSD_STATE, SSD_PAIR), F32)],
        scratch_shapes=[
            pltpu.VMEM((seq, SSD_WIDTH), F32), pltpu.VMEM((seq, 4 * SSD_STATE), F32),
            pltpu.VMEM((seq, 16), F32), pltpu.VMEM((seq, 16), F32),
            pltpu.VMEM((seq // SSD_CHUNK, 16, SSD_CHUNK), F32),
            pltpu.VMEM((seq, SSD_WIDTH), F32), pltpu.VMEM((N_DIR, SSD_HEADS, SSD_STATE, SSD_PAIR), F32),
        ],
        compiler_params=_cparams("parallel"),
        name="ssd_seq%d" % seq,
    )(proj, proj, small_t, conv_w, conv_b.reshape(1, -1), dtb_r, dtb_c, al_r, al_c, dvec, h0)


def _ssd_state_to_pairs(h):
    ht = jnp.swapaxes(h, -1, -2)
    z = jnp.zeros_like(ht)
    even = jnp.concatenate([ht, z], axis=-1)
    odd = jnp.concatenate([z, ht], axis=-1)
    is_even = (jnp.arange(SSD_HEADS) % 2 == 0)[None, None, :, None, None]
    return jnp.where(is_even, even, odd)


def _ssd_state_from_pairs(hp):
    lo = hp[..., :SSD_HEADDIM]
    hi = hp[..., SSD_HEADDIM:]
    is_even = (jnp.arange(SSD_HEADS) % 2 == 0)[None, None, :, None, None]
    return jnp.swapaxes(jnp.where(is_even, lo, hi), -1, -2)


DN_ST = DN_HEADS * DN_CHUNK


def _dn_kernel(qkv_ref, sm_ref, arow_ref, cw_ref, dtb_r_ref, alog_r_ref, dtb_st_ref, alog_st_ref, s0_ref,
               o_ref, sf_ref, q_scr, k_scr, v_scr, b_scr, g_scr, o_scr, s_scr, *, seq, width):
    nck = seq // DN_CHUNK
    xc = _silu(_short_conv(qkv_ref[...], cw_ref, width))
    for h in range(DN_HEADS):
        q = xc[:, h * DN_DK:(h + 1) * DN_DK]
        k = xc[:, DN_QK + h * DN_DK:DN_QK + (h + 1) * DN_DK]
        q_scr[:, h * DN_DK:(h + 1) * DN_DK] = (
            q * lax.rsqrt(jnp.sum(q * q, axis=-1, keepdims=True) + EPS) * (DN_DK ** -0.5))
        k_scr[:, h * DN_DK:(h + 1) * DN_DK] = k * lax.rsqrt(jnp.sum(k * k, axis=-1, keepdims=True) + EPS)
    v_scr[...] = xc[:, 2 * DN_QK:]
    b_scr[...] = _sigmoid(sm_ref[:, 16:24])
    g_scr[...] = -jnp.exp(alog_r_ref[...]) * _softplus(sm_ref[:, 24:32] + dtb_r_ref[...])
    o_scr[...] = jnp.zeros_like(o_scr)
    s_scr[...] = s0_ref[...]

    r = lax.broadcasted_iota(jnp.int32, (DN_ST, DN_ST), 0)
    c = lax.broadcasted_iota(jnp.int32, (DN_ST, DN_ST), 1)
    same = (r // DN_CHUNK) == (c // DN_CHUNK)
    eye = (r == c).astype(F32)
    tril64 = _tri(DN_CHUNK, True).astype(BF16)
    triu64 = _tri(DN_CHUNK, False).astype(BF16)

    def chunk_step(kk, carry):
        for d in range(N_DIR):
            ci = kk if d == 0 else nck - 1 - kk
            r0 = pl.multiple_of(ci * DN_CHUNK, DN_CHUNK)
            rows = pl.ds(r0, DN_CHUNK)
            incl = jnp.logical_and(same, (r >= c) if d == 0 else (r <= c))
            strict = jnp.logical_and(same, (r > c) if d == 0 else (r < c))
            gc_c = _dot_exact_lhs(tril64 if d == 0 else triu64, g_scr[rows, :])
            g_row = -jnp.exp(alog_st_ref[d:d + 1, :]) * _softplus(arow_ref[ci, d:d + 1, :] + dtb_st_ref[d:d + 1, :])
            cum = jnp.logical_and(same, (r <= c) if d == 0 else (r >= c))
            gc_r = _dot_exact_rhs(jnp.broadcast_to(g_row, (8, DN_ST)),
                                  jnp.where(cum, 1.0, 0.0).astype(BF16))[0:1, :]
            beta = b_scr[rows, :]
            end_row = DN_CHUNK - 1 if d == 0 else 0
            k_st, q_st, v_st, bt_st, gc_st, gl_st = [], [], [], [], [], []
            for h in range(DN_HEADS):
                ln = d * DN_HEADS + h
                k_st.append(k_scr[rows, h * DN_DK:(h + 1) * DN_DK])
                q_st.append(q_scr[rows, h * DN_DK:(h + 1) * DN_DK])
                v_st.append(v_scr[rows, h * DN_DV:(h + 1) * DN_DV])
                bt_st.append(beta[:, ln:ln + 1])
                col = gc_c[:, ln:ln + 1]
                gc_st.append(col)
                gl_st.append(col[end_row:end_row + 1, :])
            kst = jnp.concatenate(k_st, axis=0)
            qst = jnp.concatenate(q_st, axis=0)
            vst = jnp.concatenate(v_st, axis=0)
            bst = jnp.concatenate(bt_st, axis=0)
            gst = jnp.concatenate(gc_st, axis=0)
            decay = jnp.where(incl, jnp.exp(jnp.where(incl, gst - gc_r, 0.0)), 0.0)
            kb = kst * bst
            m = jnp.where(strict, _bdot_nt(kb, kst) * decay, 0.0)
            t = eye - m
            p = _dot3(m, m)
            t = t + _dot3(t, p)
            for _ in range(4):
                p = _dot3(p, p)
                t = t + _dot3(t, p)
            rhs = jnp.concatenate([vst * bst, kb * jnp.exp(gst)], axis=1)
            uw = _dot3(t, rhs)
            attn = jnp.where(incl, _bdot_nt(qst, kst) * decay, 0.0)
            qg = qst * jnp.exp(gst)
            vnew, qs_all = [], []
            for h in range(DN_HEADS):
                hs = slice(h * DN_CHUNK, (h + 1) * DN_CHUNK)
                s_h = s_scr[d, h]
                wq = jnp.concatenate([uw[hs, DN_DV:], qg[hs, :]], axis=0)
                ws = _bdot(wq, s_h)
                vnew.append(uw[hs, :DN_DV] - ws[:DN_CHUNK])
                qs_all.append(ws[DN_CHUNK:])
            vn_st = jnp.concatenate(vnew, axis=0)
            o_st = jnp.concatenate(qs_all, axis=0) + _bdot(attn, vn_st)
            for h in range(DN_HEADS):
                hs = slice(h * DN_CHUNK, (h + 1) * DN_CHUNK)
                o_scr[rows, h * DN_DV:(h + 1) * DN_DV] += o_st[hs, :]
                gl = gl_st[h]
                kdec = k_st[h] * jnp.exp(gl - gc_st[h])
                s_scr[d, h] = s_scr[d, h] * jnp.exp(gl) + _bdot_tn(kdec, vnew[h])
        return carry

    lax.fori_loop(0, nck, chunk_step, 0)
    o_ref[...] = o_scr[...]
    sf_ref[...] = s_scr[...]


def _dn(proj, arow, row0, nseq, seq, width, conv_w, dt_bias, a_log, s0):
    blk0 = row0 // seq
    nck = seq // DN_CHUNK
    kern = functools.partial(_dn_kernel, seq=seq, width=width)
    full = lambda *shape: pl.BlockSpec(shape, lambda b: (0,) * len(shape))
    dtb_r = dt_bias.reshape(1, 8)
    al_r = a_log.reshape(1, 8)
    dtb_st = jnp.repeat(dt_bias, DN_CHUNK, axis=1)
    al_st = jnp.repeat(a_log, DN_CHUNK, axis=1)
    return pl.pallas_call(
        kern,
        grid=(nseq,),
        in_specs=[
            pl.BlockSpec((seq, DN_CONV_DIM), lambda b: (blk0 + b, COL_QKV // DN_CONV_DIM)),
            pl.BlockSpec((seq, SMALL_W), lambda b: (blk0 + b, COL_SMALL // SMALL_W)),
            pl.BlockSpec((None, nck, N_DIR, DN_ST), lambda b: (b, 0, 0, 0)),
            full(3, DN_CONV_DIM), full(1, 8), full(1, 8), full(2, DN_ST), full(2, DN_ST),
            pl.BlockSpec((None, N_DIR, DN_HEADS, DN_DK, DN_DV), lambda b: (b, 0, 0, 0, 0)),
        ],
        out_specs=[
            pl.BlockSpec((seq, DN_V), lambda b: (b, 0)),
            pl.BlockSpec((None, N_DIR, DN_HEADS, DN_DK, DN_DV), lambda b: (b, 0, 0, 0, 0)),
        ],
        out_shape=[jax.ShapeDtypeStruct((nseq * seq, DN_V), F32),
                   jax.ShapeDtypeStruct((nseq, N_DIR, DN_HEADS, DN_DK, DN_DV), F32)],
        scratch_shapes=[
            pltpu.VMEM((seq, DN_QK), F32), pltpu.VMEM((seq, DN_QK), F32), pltpu.VMEM((seq, DN_V), F32),
            pltpu.VMEM((seq, 8), F32), pltpu.VMEM((seq, 8), F32), pltpu.VMEM((seq, DN_V), F32),
            pltpu.VMEM((N_DIR, DN_HEADS, DN_DK, DN_DV), F32),
        ],
        compiler_params=_cparams("parallel"),
        name="dn_seq%d" % seq,
    )(proj, proj, arow, conv_w, dtb_r, al_r, dtb_st, al_st, s0)


def _dn_gate_rows(a_raw, nseq, seq):
    nck = seq // DN_CHUNK
    t = a_raw.reshape(nseq, nck, DN_CHUNK, N_DIR, DN_HEADS)
    return t.transpose(0, 1, 3, 4, 2).reshape(nseq, nck, N_DIR, DN_ST)


MRG_TM = 512


def _merge_kernel(x_ref, gt_ref, y5_ref, ys_ref, z_ref, od_ref, dg_ref, gr_ref, glu_ref, sng_ref, swo_ref,
                  dng_ref, dwo_ref, wo_ref, o_ref, glu_b, swo_b, dwo_b, wo_b):
    i = pl.program_id(0)
    grp = _row_group(i, MRG_TM)

    @pl.when(i == 0)
    def _():
        glu_b[...] = glu_ref[...].astype(BF16)
        swo_b[...] = swo_ref[...].astype(BF16)
        dwo_b[...] = dwo_ref[...].astype(BF16)
        wo_b[...] = wo_ref[...].astype(BF16)

    g5 = jax.nn.gelu(y5_ref[...]).astype(BF16)
    br_a = (jnp.dot(g5, glu_b[0], preferred_element_type=F32)
            * _sigmoid(jnp.dot(g5, glu_b[1], preferred_element_type=F32)))
    ys = ys_ref[...] * _silu(z_ref[...])
    ys = ys * lax.rsqrt(jnp.mean(ys * ys, axis=-1, keepdims=True) + EPS) * sng_ref[...]
    br_b = jnp.dot(ys.astype(BF16), swo_b[...], preferred_element_type=F32)
    od = od_ref[...]
    parts = []
    for h in range(DN_HEADS):
        oh = od[:, h * DN_DV:(h + 1) * DN_DV]
        parts.append(oh * lax.rsqrt(jnp.mean(oh * oh, axis=-1, keepdims=True) + EPS) * dng_ref[...])
    on = jnp.concatenate(parts, axis=1) * _silu(dg_ref[...])
    br_c = jnp.dot(on.astype(BF16), dwo_b[...], preferred_element_type=F32)
    merged = (_sigmoid(gr_ref[:, 0:D_MODEL]) * br_a
              + _sigmoid(gr_ref[:, D_MODEL:2 * D_MODEL]) * br_b
              + _sigmoid(gr_ref[:, 2 * D_MODEL:3 * D_MODEL]) * br_c)
    out = jnp.dot(merged.astype(BF16), wo_b[...], preferred_element_type=F32)
    o_ref[...] = x_ref[...] + gt_ref[pl.ds(grp, 1), :] * out


def _merge(x, mod, proj, y5, ys, od, s5_glu_l, ssd_norm_g_l, ssd_w_out_l, dn_norm_g_l, dn_w_out_l, w_out_l):
    rowblk = lambda w, col: pl.BlockSpec((MRG_TM, w), lambda i: (i, col // w))
    full = lambda *shape: pl.BlockSpec(shape, lambda i: (0,) * len(shape))
    return pl.pallas_call(
        _merge_kernel,
        grid=(N_ROWS // MRG_TM,),
        in_specs=[
            rowblk(D_MODEL, 0),
            pl.BlockSpec((8, D_MODEL), lambda i: (0, 5)),
            rowblk(S5_WIDTH, 0), rowblk(SSD_WIDTH, 0), rowblk(SSD_WIDTH, COL_Z), rowblk(DN_V, 0),
            rowblk(DN_V, COL_DNG), rowblk(3 * D_MODEL, COL_GATES),
            full(2, S5_WIDTH, D_MODEL), full(1, SSD_WIDTH), full(SSD_WIDTH, D_MODEL),
            full(1, DN_DV), full(DN_V, D_MODEL), full(D_MODEL, D_MODEL),
        ],
        out_specs=rowblk(D_MODEL, 0),
        out_shape=jax.ShapeDtypeStruct((N_ROWS, D_MODEL), F32),
        scratch_shapes=[pltpu.VMEM((2, S5_WIDTH, D_MODEL), BF16), pltpu.VMEM((SSD_WIDTH, D_MODEL), BF16),
                        pltpu.VMEM((DN_V, D_MODEL), BF16), pltpu.VMEM((D_MODEL, D_MODEL), BF16)],
        compiler_params=_cparams("arbitrary"),
        name="merge",
    )(x, mod, y5, ys, proj, od, proj, proj, s5_glu_l, ssd_norm_g_l.reshape(1, -1), ssd_w_out_l,
      dn_norm_g_l.reshape(1, -1), dn_w_out_l, w_out_l)


FIN_TM = 1024


def _final_norm_kernel(x_ref, g_ref, o_ref):
    x = x_ref[...]
    o_ref[...] = x * lax.rsqrt(jnp.mean(x * x, axis=-1, keepdims=True) + EPS) * g_ref[...]


def _final_norm(x, g):
    return pl.pallas_call(
        _final_norm_kernel,
        grid=(N_ROWS // FIN_TM,),
        in_specs=[pl.BlockSpec((FIN_TM, D_MODEL), lambda i: (i, 0)), pl.BlockSpec((1, D_MODEL), lambda i: (0, 0))],
        out_specs=pl.BlockSpec((FIN_TM, D_MODEL), lambda i: (i, 0)),
        out_shape=jax.ShapeDtypeStruct((N_ROWS, D_MODEL), F32),
        compiler_params=_cparams("parallel"),
        name="final_norm",
    )(x, g.reshape(1, -1))


def kernel(x_prompt, x_sample, state_s5_re, state_s5_im, state_ssd, state_dn, c, c_ctx, ada_w, ada_b, norm_g, ffn_wi, ffn_wo, w_in, s5_lam_re, s5_lam_im, s5_log_dt, s5_b_re, s5_b_im, s5_c_re, s5_c_im, s5_d, s5_glu, ssd_conv_w, ssd_conv_b, ssd_dt_bias, ssd_a_log, ssd_d, ssd_norm_g, ssd_w_out, dn_conv_w, dn_dt_bias, dn_a_log, dn_norm_g, dn_w_out, w_out, final_norm_g):
    x = jnp.concatenate([x_prompt.reshape(N_CTX, D_MODEL), x_sample.reshape(N_LAT, D_MODEL)], axis=0)
    cond8 = jnp.concatenate([c_ctx[None, :], c, jnp.zeros((8 - 1 - DEC_BATCH, D_MODEL), F32)], axis=0)
    mods = _ada_mods(cond8, ada_w, ada_b)

    mmat, ptab, qtab, atab = _s5_prep(s5_lam_re, s5_lam_im, s5_log_dt, s5_b_re, s5_b_im, s5_c_re, s5_c_im)
    p_re = jnp.concatenate([ptab[:, :, 0], ptab[:, :, 2]], axis=-1)
    p_im = jnp.concatenate([ptab[:, :, 1], ptab[:, :, 3]], axis=-1)
    q_re = jnp.concatenate([qtab[:, :, 0], qtab[:, :, 2]], axis=-1)
    q_im = jnp.concatenate([qtab[:, :, 1], qtab[:, :, 3]], axis=-1)
    a_re = jnp.concatenate([atab[:, :, 0], atab[:, :, 2]], axis=-1)[:, :, None, :]
    a_im = jnp.concatenate([atab[:, :, 1], atab[:, :, 3]], axis=-1)[:, :, None, :]
    s5_d_rows = jnp.tile(s5_d.reshape(DEPTH, S5_GROUPS, 1, S5_GROUP), (1, 1, 1, S5_CHUNK))
    ssd_d_rows = jnp.repeat(ssd_d, SSD_HEADDIM, axis=1).reshape(DEPTH, 1, SSD_WIDTH)

    def s5_h0(state):
        return state.transpose(1, 3, 0, 2, 4).reshape(DEPTH, S5_GROUPS, DEC_BATCH, 2 * S5_STATE)

    h0_re = s5_h0(state_s5_re)
    h0_im = s5_h0(state_s5_im)
    ssd_h0_lat = _ssd_state_to_pairs(jnp.swapaxes(state_ssd, 0, 1).reshape(
        DEPTH * DEC_BATCH, N_DIR, SSD_HEADS, SSD_HEADDIM, SSD_STATE)).reshape(
        DEPTH, DEC_BATCH, N_DIR, SSD_HEADS, SSD_STATE, SSD_PAIR)
    ssd_h0_ctx = jnp.zeros((BATCH, N_DIR, SSD_HEADS, SSD_STATE, SSD_PAIR), F32)
    dn_s0_lat = jnp.swapaxes(state_dn, 0, 1)
    dn_s0_ctx = jnp.zeros((BATCH, N_DIR, DN_HEADS, DN_DK, DN_DV), F32)

    new_s5_re, new_s5_im, new_ssd, new_dn = [], [], [], []
    for l in range(DEPTH):
        mod = mods[l]
        x = _ffn(x, mod, norm_g[l, 0:1], ffn_wi, ffn_wo, l, 0)
        proj = _inproj(x, mod, norm_g[l, 1:2], _repack_w_in(w_in[l]))

        y5_g, f_re, f_im = _s5_scan(_s5_to_groups(proj[:, COL_U:COL_U + S5_WIDTH]), mmat[l], p_re[l], p_im[l],
                                    q_re[l], q_im[l], a_re[l], a_im[l], s5_d_rows[l], h0_re[l], h0_im[l])
        y5 = _s5_from_groups(y5_g)
        new_s5_re.append(f_re.reshape(S5_GROUPS, BATCH, N_DIR, S5_STATE).transpose(1, 2, 0, 3))
        new_s5_im.append(f_im.reshape(S5_GROUPS, BATCH, N_DIR, S5_STATE).transpose(1, 2, 0, 3))

        small = proj[:, COL_SMALL:COL_SMALL + 32]
        small_t = small.T
        ys_c, hs_c = _ssd(proj, small_t, 0, BATCH, SEQ, SEQ, ssd_conv_w[l], ssd_conv_b[l], ssd_dt_bias[l],
                          ssd_a_log[l], ssd_d_rows[l], ssd_h0_ctx)
        ys_l, _ = _ssd(proj, small_t, N_CTX, DEC_BATCH, DEC_SEQ, GRID_W, ssd_conv_w[l], ssd_conv_b[l],
                       ssd_dt_bias[l], ssd_a_log[l], ssd_d_rows[l], ssd_h0_lat[l])
        new_ssd.append(_ssd_state_from_pairs(hs_c))
        a_raw = small[:, 24:32]
        od_c, sd_c = _dn(proj, _dn_gate_rows(a_raw[:N_CTX], BATCH, SEQ), 0, BATCH, SEQ, SEQ, dn_conv_w[l],
                         dn_dt_bias[l], dn_a_log[l], dn_s0_ctx)
        od_l, _ = _dn(proj, _dn_gate_rows(a_raw[N_CTX:], DEC_BATCH, DEC_SEQ), N_CTX, DEC_BATCH, DEC_SEQ, GRID_W,
                      dn_conv_w[l], dn_dt_bias[l], dn_a_log[l], dn_s0_lat[l])
        new_dn.append(sd_c)

        x = _merge(x, mod, proj, y5, jnp.concatenate([ys_c, ys_l], axis=0), jnp.concatenate([od_c, od_l], axis=0),
                   s5_glu[l], ssd_norm_g[l], ssd_w_out[l], dn_norm_g[l], dn_w_out[l], w_out[l])
        x = _ffn(x, mod, norm_g[l, 2:3], ffn_wi, ffn_wo, l, 1)

    y = _final_norm(x, final_norm_g)
    y_prompt = y[:N_CTX].reshape(BATCH, SEQ, D_MODEL)
    y_sample = y[N_CTX:].reshape(DEC_BATCH, DEC_SEQ, D_MODEL)
    return (y_prompt, y_sample, jnp.stack(new_s5_re, axis=1), jnp.stack(new_s5_im, axis=1),
            jnp.stack(new_ssd, axis=1), jnp.stack(new_dn, axis=1))
```

```python
import functools

import jax
import jax.numpy as jnp
import numpy as np
from jax import lax
from jax.experimental import pallas as pl
from jax.experimental.pallas import tpu as pltpu

F32 = jnp.float32
BF16 = jnp.bfloat16

D_MODEL = 1024
BATCH = 16
SEQ = 256
DEPTH = 4
DEC_BATCH = 2
DEC_SEQ = 1024
GRID_W = 64
N_DIR = 2
N_ADA = 9
D_FF = 2816
EPS = 1e-6

S5_WIDTH = 512
S5_GROUP = 16
S5_GROUPS = 32
S5_STATE = 64
S5_CHUNK = 16

SSD_WIDTH = 512
SSD_HEADDIM = 64
SSD_HEADS = 8
SSD_GROUPS = 2
SSD_STATE = 64
SSD_CHUNK = 128
SSD_CONV_DIM = 768

DN_HEADS = 4
DN_DK = 128
DN_DV = 128
DN_QK = 512
DN_V = 512
DN_CHUNK = 64
DN_CONV_DIM = 1536

IN_SEGMENTS = (512, 512, 768, 16, 1536, 8, 8, 512, 3072)
IN_SPLITS = tuple(int(s) for s in np.cumsum(IN_SEGMENTS)[:-1])

N_CTX = BATCH * SEQ
N_LAT = DEC_BATCH * DEC_SEQ
N_ROWS = N_CTX + N_LAT

COL_GATES = 0
COL_QKV = 3072
COL_U = 4608
COL_Z = 5120
COL_DNG = 5632
COL_XBC = 6144
COL_SMALL = 6912
PROJ_W = 7040
SMALL_W = 128

VMEM_LIMIT = 56 * 1024 * 1024


def _cparams(*sem):
    return pltpu.CompilerParams(dimension_semantics=sem, vmem_limit_bytes=VMEM_LIMIT)


def _sigmoid(x):
    return jax.nn.sigmoid(x)


def _silu(x):
    return x * jax.nn.sigmoid(x)


def _softplus(x):
    return jnp.maximum(x, 0.0) + jnp.log(1.0 + jnp.exp(-jnp.abs(x)))


def _bdot(a, b):
    return jnp.dot(a.astype(BF16), b.astype(BF16), preferred_element_type=F32)


def _bdot_nt(a, b):
    return lax.dot_general(a.astype(BF16), b.astype(BF16), (((1,), (1,)), ((), ())),
                           preferred_element_type=F32)


def _bdot_tn(a, b):
    return lax.dot_general(a.astype(BF16), b.astype(BF16), (((0,), (0,)), ((), ())),
                           preferred_element_type=F32)


def _split3(a):
    hi = a.astype(BF16)
    r = a - hi.astype(F32)
    mid = r.astype(BF16)
    lo = (r - mid.astype(F32)).astype(BF16)
    return hi, mid, lo


def _dot3(a, b):
    ah = a.astype(BF16)
    al = (a - ah.astype(F32)).astype(BF16)
    bh = b.astype(BF16)
    bl = (b - bh.astype(F32)).astype(BF16)
    out = jnp.dot(ah, bh, preferred_element_type=F32)
    out = out + jnp.dot(ah, bl, preferred_element_type=F32)
    out = out + jnp.dot(al, bh, preferred_element_type=F32)
    return out


def _dot_exact_lhs(t_bf16, x):
    hi, mid, lo = _split3(x)
    out = jnp.dot(t_bf16, hi, preferred_element_type=F32)
    out = out + jnp.dot(t_bf16, mid, preferred_element_type=F32)
    out = out + jnp.dot(t_bf16, lo, preferred_element_type=F32)
    return out


def _dot_exact_rhs(x, t_bf16):
    hi, mid, lo = _split3(x)
    out = jnp.dot(hi, t_bf16, preferred_element_type=F32)
    out = out + jnp.dot(mid, t_bf16, preferred_element_type=F32)
    out = out + jnp.dot(lo, t_bf16, preferred_element_type=F32)
    return out


def _norm_mod(x, g, sc, sh):
    ms = jnp.mean(x * x, axis=-1, keepdims=True)
    y = x * lax.rsqrt(ms + EPS) * g
    return y * (1.0 + sc) + sh


def _row_group(i, tm):
    nctx = N_CTX // tm
    per = DEC_SEQ // tm
    return jnp.where(i < nctx, 0, 1 + jnp.maximum(i - nctx, 0) // per)


ADA_TN = 1152


def _ada_kernel(c_ref, w_ref, b_ref, o_ref):
    c = c_ref[...]
    o_ref[...] = _bdot(_silu(c), w_ref[...]) + b_ref[...]


def _ada_mods(cond8, ada_w, ada_b):
    nj = (N_ADA * D_MODEL) // ADA_TN
    return pl.pallas_call(
        _ada_kernel,
        grid=(DEPTH, nj),
        in_specs=[
            pl.BlockSpec((8, D_MODEL), lambda l, j: (0, 0)),
            pl.BlockSpec((None, D_MODEL, ADA_TN), lambda l, j: (l, 0, j)),
            pl.BlockSpec((None, 1, ADA_TN), lambda l, j: (l, 0, j)),
        ],
        out_specs=pl.BlockSpec((None, 8, ADA_TN), lambda l, j: (l, 0, j)),
        out_shape=jax.ShapeDtypeStruct((DEPTH, 8, N_ADA * D_MODEL), F32),
        compiler_params=_cparams("parallel", "parallel"),
        name="ada_mods",
    )(cond8, ada_w, ada_b.reshape(DEPTH, 1, N_ADA * D_MODEL))


FFN_TM = 1024
FFN_TF = 256


def _ffn_kernel(x_ref, sh_ref, sc_ref, gt_ref, g_ref, wa_ref, wb_ref, wo_ref, o_ref, h_scr, acc_scr):
    i = pl.program_id(0)
    j = pl.program_id(1)
    grp = _row_group(i, FFN_TM)

    @pl.when(j == 0)
    def _():
        h = _norm_mod(x_ref[...], g_ref[...], sc_ref[pl.ds(grp, 1), :], sh_ref[pl.ds(grp, 1), :])
        h_scr[...] = h.astype(BF16)
        acc_scr[...] = jnp.zeros_like(acc_scr)

    h = h_scr[...]
    a = jnp.dot(h, wa_ref[...].astype(BF16), preferred_element_type=F32)
    b = jnp.dot(h, wb_ref[...].astype(BF16), preferred_element_type=F32)
    u = (_silu(a) * b).astype(BF16)
    acc_scr[...] += jnp.dot(u, wo_ref[...].astype(BF16), preferred_element_type=F32)

    @pl.when(j == pl.num_programs(1) - 1)
    def _():
        o_ref[...] = x_ref[...] + (0.5 * gt_ref[pl.ds(grp, 1), :]) * acc_scr[...]


def _ffn(x, mod, norm_g_row, ffn_wi, ffn_wo, layer, which):
    nf = D_FF // FFN_TF
    base = 0 if which == 0 else 6
    return pl.pallas_call(
        _ffn_kernel,
        grid=(N_ROWS // FFN_TM, nf),
        in_specs=[
            pl.BlockSpec((FFN_TM, D_MODEL), lambda i, j: (i, 0)),
            pl.BlockSpec((8, D_MODEL), lambda i, j: (0, base)),
            pl.BlockSpec((8, D_MODEL), lambda i, j: (0, base + 1)),
            pl.BlockSpec((8, D_MODEL), lambda i, j: (0, base + 2)),
            pl.BlockSpec((1, D_MODEL), lambda i, j: (0, 0)),
            pl.BlockSpec((None, None, D_MODEL, FFN_TF), lambda i, j: (layer, which, 0, j)),
            pl.BlockSpec((None, None, D_MODEL, FFN_TF), lambda i, j: (layer, which, 0, j + nf)),
            pl.BlockSpec((None, None, FFN_TF, D_MODEL), lambda i, j: (layer, which, j, 0)),
        ],
        out_specs=pl.BlockSpec((FFN_TM, D_MODEL), lambda i, j: (i, 0)),
        out_shape=jax.ShapeDtypeStruct((N_ROWS, D_MODEL), F32),
        scratch_shapes=[pltpu.VMEM((FFN_TM, D_MODEL), BF16), pltpu.VMEM((FFN_TM, D_MODEL), F32)],
        compiler_params=_cparams("parallel", "arbitrary"),
        name="ffn",
    )(x, mod, mod, mod, norm_g_row, ffn_wi, ffn_wi, ffn_wo)


INP_TM = 1024
INP_TN = 640


def _inproj_kernel(x_ref, sh_ref, sc_ref, g_ref, w_ref, o_ref, h_scr):
    i = pl.program_id(0)
    j = pl.program_id(1)
    grp = _row_group(i, INP_TM)

    @pl.when(j == 0)
    def _():
        h = _norm_mod(x_ref[...], g_ref[...], sc_ref[pl.ds(grp, 1), :], sh_ref[pl.ds(grp, 1), :])
        h_scr[...] = h.astype(BF16)

    o_ref[...] = jnp.dot(h_scr[...], w_ref[...], preferred_element_type=F32)


def _repack_w_in(w):
    u, z, xbc, dt, qkv, beta, a, dng, gates = jnp.split(w, IN_SPLITS, axis=-1)
    pad = jnp.zeros((w.shape[0], SMALL_W - 32), w.dtype)
    return jnp.concatenate([gates, qkv, u, z, dng, xbc, dt, beta, a, pad], axis=-1).astype(BF16)


def _inproj(x, mod, norm_g_row, w_packed):
    return pl.pallas_call(
        _inproj_kernel,
        grid=(N_ROWS // INP_TM, PROJ_W // INP_TN),
        in_specs=[
            pl.BlockSpec((INP_TM, D_MODEL), lambda i, j: (i, 0)),
            pl.BlockSpec((8, D_MODEL), lambda i, j: (0, 3)),
            pl.BlockSpec((8, D_MODEL), lambda i, j: (0, 4)),
            pl.BlockSpec((1, D_MODEL), lambda i, j: (0, 0)),
            pl.BlockSpec((D_MODEL, INP_TN), lambda i, j: (0, j)),
        ],
        out_specs=pl.BlockSpec((INP_TM, INP_TN), lambda i, j: (i, j)),
        out_shape=jax.ShapeDtypeStruct((N_ROWS, PROJ_W), F32),
        scratch_shapes=[pltpu.VMEM((INP_TM, D_MODEL), BF16)],
        compiler_params=_cparams("parallel", "arbitrary"),
        name="inproj",
    )(x, mod, mod, norm_g_row, w_packed)


S5_ROW = S5_CHUNK * S5_GROUP


def _s5_prep_kernel(lam_re_ref, lam_im_ref, ldt_ref, btr_ref, bti_ref, cr_ref, ci_ref, ctr_ref, cti_ref,
                    m_ref, p_ref, q_ref, a_ref):
    tau = lax.broadcasted_iota(jnp.int32, (S5_CHUNK, 1), 0).astype(F32)
    lane_t = lax.broadcasted_iota(jnp.int32, (1, S5_ROW), 1) // S5_GROUP
    mmat = jnp.zeros((S5_ROW, S5_ROW), F32)

    for d in range(N_DIR):
        lr = lam_re_ref[d:d + 1, :]
        li = lam_im_ref[d:d + 1, :]
        dt = jnp.exp(ldt_ref[d:d + 1, :])
        mag = jnp.exp(lr * dt)
        lb_re = mag * jnp.cos(li * dt)
        lb_im = mag * jnp.sin(li * dt)
        den = lr * lr + li * li
        cr = ((lb_re - 1.0) * lr + lb_im * li) / den
        ci = (lb_im * lr - (lb_re - 1.0) * li) / den
        bt_r = btr_ref[d]
        bt_i = bti_ref[d]
        bbt_r = cr * bt_r - ci * bt_i
        bbt_i = cr * bt_i + ci * bt_r
        c_r = cr_ref[d]
        c_i = ci_ref[d]

        def powtab(t):
            m = jnp.exp(t * (lr * dt))
            ang = t * (li * dt)
            return m * jnp.cos(ang), m * jnp.sin(ang)

        def outer(ar, ai, xr, xi):
            rr = ar[:, None, :] * xr[None, :, :] - ai[:, None, :] * xi[None, :, :]
            ii = ar[:, None, :] * xi[None, :, :] + ai[:, None, :] * xr[None, :, :]
            return rr.reshape(S5_ROW, S5_STATE), ii.reshape(S5_ROW, S5_STATE)

        t_in = (S5_CHUNK - 1) - tau if d == 0 else tau
        ar, ai = powtab(t_in)
        ba_r, ba_i = outer(ar, ai, bbt_r, bbt_i)
        p_ref[2 * d] = ba_r
        p_ref[2 * d + 1] = ba_i
        kt = _dot3(ba_r, ctr_ref[d]) - _dot3(ba_i, cti_ref[d])
        for s in range(S5_CHUNK):
            rows = s * S5_GROUP
            if d == 0:
                sh = kt if s == 0 else jnp.concatenate(
                    [kt[rows:, :], jnp.zeros((rows, S5_ROW), F32)], axis=0)
                mmat = mmat + jnp.where(lane_t == (S5_CHUNK - 1) - s, sh, 0.0)
            else:
                sh = kt if s == 0 else jnp.concatenate(
                    [jnp.zeros((rows, S5_ROW), F32), kt[:S5_ROW - rows, :]], axis=0)
                mmat = mmat + jnp.where(lane_t == s, sh, 0.0)
        t_out = tau + 1.0 if d == 0 else S5_CHUNK - tau
        ar, ai = powtab(t_out)
        qr, qi = outer(ar, ai, c_r, c_i)
        q_ref[2 * d] = qr
        q_ref[2 * d + 1] = -qi
        a16r, a16i = powtab(jnp.full((1, 1), float(S5_CHUNK), F32))
        a_ref[2 * d:2 * d + 1, :] = a16r
        a_ref[2 * d + 1:2 * d + 2, :] = a16i

    m_ref[...] = mmat


def _s5_prep(lam_re, lam_im, log_dt, b_re, b_im, c_re, c_im):
    tg = lambda t: jnp.swapaxes(t, 1, 2)
    lam_re_g = tg(lam_re)
    lam_im_g = tg(lam_im)
    ldt_g = tg(log_dt)[..., None]
    bt_r = jnp.swapaxes(tg(b_re), -1, -2)
    bt_i = jnp.swapaxes(tg(b_im), -1, -2)
    c_r = tg(c_re)
    c_i = tg(c_im)
    ct_r = jnp.tile(jnp.swapaxes(c_r, -1, -2), (1, 1, 1, 1, S5_CHUNK))
    ct_i = jnp.tile(jnp.swapaxes(c_i, -1, -2), (1, 1, 1, 1, S5_CHUNK))

    def spec(*tail):
        n = len(tail)
        return pl.BlockSpec((None, None) + tail, lambda l, g: (l, g) + (0,) * n)

    return pl.pallas_call(
        _s5_prep_kernel,
        grid=(DEPTH, S5_GROUPS),
        in_specs=[spec(2, 64), spec(2, 64), spec(2, 1), spec(2, 16, 64), spec(2, 16, 64),
                  spec(2, 16, 64), spec(2, 16, 64), spec(2, 64, S5_ROW), spec(2, 64, S5_ROW)],
        out_specs=[spec(S5_ROW, S5_ROW), spec(4, S5_ROW, 64), spec(4, S5_ROW, 64), spec(4, 64)],
        out_shape=[jax.ShapeDtypeStruct((DEPTH, S5_GROUPS, S5_ROW, S5_ROW), F32),
                   jax.ShapeDtypeStruct((DEPTH, S5_GROUPS, 4, S5_ROW, 64), F32),
                   jax.ShapeDtypeStruct((DEPTH, S5_GROUPS, 4, S5_ROW, 64), F32),
                   jax.ShapeDtypeStruct((DEPTH, S5_GROUPS, 4, 64), F32)],
        compiler_params=_cparams("parallel", "parallel"),
        name="s5_prep",
    )(lam_re_g, lam_im_g, ldt_g, bt_r, bt_i, c_r, c_i, ct_r, ct_i)


S5_CTX_CH = SEQ // S5_CHUNK
S5_LAT_CH = DEC_SEQ // S5_CHUNK
S5_CTX_ROWS = S5_CTX_CH * BATCH
S5_LAT_ROWS = S5_LAT_CH * DEC_BATCH
S5_ROWS = S5_CTX_ROWS + S5_LAT_ROWS


def _s5_kernel(u_ref, m_ref, pre_ref, pim_ref, qre_ref, qim_ref, are_ref, aim_ref, d_ref, h0r_ref, h0i_ref,
               y_ref, fr_ref, fi_ref, sre, sim, hfr, hfi, hbr, hbi):
    u = u_ref[...]
    ub = u.astype(BF16)
    sre[...] = jnp.dot(ub, pre_ref[...].astype(BF16), preferred_element_type=F32)
    sim[...] = jnp.dot(ub, pim_ref[...].astype(BF16), preferred_element_type=F32)
    ar = are_ref[...]
    ai = aim_ref[...]
    fwd = lax.broadcasted_iota(jnp.int32, (1, 2 * S5_STATE), 1) < S5_STATE

    def scan(base, nchunk, nseq, h_re, h_im):
        for k in range(nchunk):
            rf = base + k * nseq
            rb = base + (nchunk - 1 - k) * nseq
            hfr[rf:rf + nseq, :] = h_re
            hfi[rf:rf + nseq, :] = h_im
            hbr[rb:rb + nseq, :] = h_re
            hbi[rb:rb + nseq, :] = h_im
            s_r = jnp.where(fwd, sre[rf:rf + nseq, :], sre[rb:rb + nseq, :])
            s_i = jnp.where(fwd, sim[rf:rf + nseq, :], sim[rb:rb + nseq, :])
            h_re, h_im = ar * h_re - ai * h_im + s_r, ar * h_im + ai * h_re + s_i
        return h_re, h_im

    zero = jnp.zeros((BATCH, 2 * S5_STATE), F32)
    f_re, f_im = scan(0, S5_CTX_CH, BATCH, zero, zero)
    fr_ref[...] = f_re
    fi_ref[...] = f_im
    scan(S5_CTX_ROWS, S5_LAT_CH, DEC_BATCH, h0r_ref[...], h0i_ref[...])

    h_re = jnp.where(fwd, hfr[...], hbr[...])
    h_im = jnp.where(fwd, hfi[...], hbi[...])
    y = jnp.dot(ub, m_ref[...].astype(BF16), preferred_element_type=F32)
    y = y + _bdot_nt(h_re, qre_ref[...]) + _bdot_nt(h_im, qim_ref[...])
    y_ref[...] = y + d_ref[...] * u


def _s5_scan(u_g, mmat, p_re, p_im, q_re, q_im, a_re, a_im, d_row, h0_re, h0_im):
    def gspec(*tail):
        n = len(tail)
        return pl.BlockSpec((None,) + tail, lambda g: (g,) + (0,) * n)

    st = 2 * S5_STATE
    return pl.pallas_call(
        _s5_kernel,
        grid=(S5_GROUPS,),
        in_specs=[gspec(S5_ROWS, S5_ROW), gspec(S5_ROW, S5_ROW), gspec(S5_ROW, st), gspec(S5_ROW, st),
                  gspec(S5_ROW, st), gspec(S5_ROW, st), gspec(1, st), gspec(1, st), gspec(1, S5_ROW),
                  gspec(DEC_BATCH, st), gspec(DEC_BATCH, st)],
        out_specs=[gspec(S5_ROWS, S5_ROW), gspec(BATCH, st), gspec(BATCH, st)],
        out_shape=[jax.ShapeDtypeStruct((S5_GROUPS, S5_ROWS, S5_ROW), F32),
                   jax.ShapeDtypeStruct((S5_GROUPS, BATCH, st), F32),
                   jax.ShapeDtypeStruct((S5_GROUPS, BATCH, st), F32)],
        scratch_shapes=[pltpu.VMEM((S5_ROWS, st), F32) for _ in range(6)],
        compiler_params=_cparams("parallel"),
        name="s5_scan",
    )(u_g, mmat, p_re, p_im, q_re, q_im, a_re, a_im, d_row, h0_re, h0_im)


def _s5_to_groups(u_tok):
    uc = u_tok[:N_CTX].reshape(BATCH, S5_CTX_CH, S5_CHUNK, S5_GROUPS, S5_GROUP)
    uc = uc.transpose(3, 1, 0, 2, 4).reshape(S5_GROUPS, S5_CTX_ROWS, S5_ROW)
    ul = u_tok[N_CTX:].reshape(DEC_BATCH, S5_LAT_CH, S5_CHUNK, S5_GROUPS, S5_GROUP)
    ul = ul.transpose(3, 1, 0, 2, 4).reshape(S5_GROUPS, S5_LAT_ROWS, S5_ROW)
    return jnp.concatenate([uc, ul], axis=1)


def _s5_from_groups(y_g):
    yc = y_g[:, :S5_CTX_ROWS].reshape(S5_GROUPS, S5_CTX_CH, BATCH, S5_CHUNK, S5_GROUP)
    yc = yc.transpose(2, 1, 3, 0, 4).reshape(N_CTX, S5_WIDTH)
    yl = y_g[:, S5_CTX_ROWS:].reshape(S5_GROUPS, S5_LAT_CH, DEC_BATCH, S5_CHUNK, S5_GROUP)
    yl = yl.transpose(2, 1, 3, 0, 4).reshape(N_LAT, S5_WIDTH)
    return jnp.concatenate([yc, yl], axis=0)


PRE_ROWS = 128


def _conv_block(x_ref, w_ref, r0, seq, width):
    x = x_ref[pl.ds(r0, PRE_ROWS), :]
    prev = x_ref[pl.ds(jnp.maximum(r0 - 1, 0), 1), :]
    nxt = x_ref[pl.ds(jnp.minimum(r0 + PRE_ROWS, seq - 1), 1), :]
    rid = lax.broadcasted_iota(jnp.int32, (PRE_ROWS, 1), 0)
    pos = (r0 + rid) % width
    xm = jnp.where(rid == 0, prev, pltpu.roll(x, 1, 0))
    xm = jnp.where(pos == 0, 0.0, xm)
    xp = jnp.where(rid == PRE_ROWS - 1, nxt, pltpu.roll(x, PRE_ROWS - 1, 0))
    xp = jnp.where(pos == width - 1, 0.0, xp)
    return xm * w_ref[0:1, :] + x * w_ref[1:2, :] + xp * w_ref[2:3, :]


def _round_robin(problems):
    live = list(problems)
    while live:
        nxt = []
        for p in live:
            try:
                next(p)
                nxt.append(p)
            except StopIteration:
                pass
        live = nxt


def _tri(n, lower):
    r = lax.broadcasted_iota(jnp.int32, (n, n), 0)
    c = lax.broadcasted_iota(jnp.int32, (n, n), 1)
    return (r >= c) if lower else (r <= c)


SSD_PAIR = 2 * SSD_HEADDIM
SSD_PAIRS = SSD_HEADS // 2


def _ssd_kernel(xbc_ref, sm_ref, smt_ref, cw_ref, cb_ref, dtb_r_ref, dtb_c_ref, alog_r_ref, alog_c_ref,
                dvec_ref, h0_ref, y_ref, hf_ref, xs_scr, bc_scr, dac_scr, dar_scr, dtr_scr, h_scr,
                *, seq, width):
    nck = seq // SSD_CHUNK

    def pre(bi, carry):
        r0 = pl.multiple_of(bi * PRE_ROWS, PRE_ROWS)
        rows = pl.ds(r0, PRE_ROWS)
        xc = _silu(_conv_block(xbc_ref, cw_ref, r0, seq, width) + cb_ref[...])
        xs = xc[:, :SSD_WIDTH]
        xs_scr[rows, :] = xs
        bc_scr[rows, :] = xc[:, SSD_WIDTH:]
        y_ref[rows, :] = dvec_ref[...] * xs
        dt_c = _softplus(sm_ref[rows, 0:16] + dtb_r_ref[...])
        dac_scr[rows, :] = dt_c * (-jnp.exp(alog_r_ref[...]))
        return carry

    lax.fori_loop(0, seq // PRE_ROWS, pre, 0)
    dt_r = _softplus(smt_ref[0:16, :] + dtb_c_ref[...])
    da_r = dt_r * (-jnp.exp(alog_c_ref[...]))
    for ck in range(nck):
        dtr_scr[ck] = dt_r[:, ck * SSD_CHUNK:(ck + 1) * SSD_CHUNK]
        dar_scr[ck] = da_r[:, ck * SSD_CHUNK:(ck + 1) * SSD_CHUNK]
    h_scr[...] = h0_ref[...]

    tril = _tri(SSD_CHUNK, True)
    triu = _tri(SSD_CHUNK, False)
    tril_b = tril.astype(BF16)
    triu_b = triu.astype(BF16)
    lo_half = lax.broadcasted_iota(jnp.int32, (1, SSD_PAIR), 1) < SSD_HEADDIM

    def chunk_problem(dirs, k):
        for d in dirs:
            c = k if d == 0 else nck - 1 - k
            r0 = pl.multiple_of(c * SSD_CHUNK, SSD_CHUNK)
            rows = pl.ds(r0, SSD_CHUNK)
            mask = tril if d == 0 else triu
            ac = _dot_exact_lhs(tril_b if d == 0 else triu_b, dac_scr[rows, :])
            at = _dot_exact_rhs(dar_scr[c], triu_b if d == 0 else tril_b)
            dt_row = dtr_scr[c]
            end = SSD_CHUNK - 1 if d == 0 else 0
            bcx = bc_scr[rows, :]
            gmat = []
            for g in range(SSD_GROUPS):
                bm = bcx[:, g * SSD_STATE:(g + 1) * SSD_STATE]
                cm = bcx[:, 2 * SSD_STATE + g * SSD_STATE:2 * SSD_STATE + (g + 1) * SSD_STATE]
                gmat.append((bm.T, cm, _bdot_nt(cm, bm)))
            yield
            for pr in range(SSD_PAIRS):
                bmt, cm, gm = gmat[pr // (SSD_PAIRS // SSD_GROUPS)]
                xpair = xs_scr[rows, pr * SSD_PAIR:(pr + 1) * SSD_PAIR]
                sc, bt, es, dec, xh = [], [], [], [], []
                for half in range(2):
                    ln = d * SSD_HEADS + 2 * pr + half
                    colb = jnp.broadcast_to(ac[:, ln:ln + 1], (SSD_CHUNK, SSD_CHUNK))
                    row = at[ln:ln + 1, :]
                    dtr = dt_row[ln:ln + 1, :]
                    seg = jnp.where(mask, jnp.exp(jnp.where(mask, colb - row, 0.0)), 0.0)
                    a_end = row[:, end:end + 1]
                    sc.append(gm * seg * dtr)
                    bt.append(bmt * (jnp.exp(a_end - row) * dtr))
                    es.append(jnp.exp(colb))
                    dec.append(jnp.exp(a_end))
                    xh.append(jnp.where(lo_half if half == 0 else jnp.logical_not(lo_half), xpair, 0.0))
                xst = jnp.concatenate(xh, axis=0)
                hs = h_scr[d, pr]
                y = _bdot(jnp.concatenate(sc, axis=1), xst)
                y = y + _bdot(cm, hs) * jnp.where(lo_half, es[0], es[1])
                y_ref[rows, pr * SSD_PAIR:(pr + 1) * SSD_PAIR] += y
                h_scr[d, pr] = (hs * jnp.where(lo_half, dec[0], dec[1])
                                + _bdot(jnp.concatenate(bt, axis=1), xst))
                yield

    def chunk_step(k, carry):
        _round_robin([chunk_problem((d,), k) for d in range(N_DIR)])
        return carry

    lax.fori_loop(0, nck, chunk_step, 0)
    hf_ref[...] = h_scr[...]


def _ssd(proj, small_t, row0, nseq, seq, width, conv_w, conv_b, dt_bias, a_log, dvec, h0):
    blk0 = row0 // seq
    nck = seq // SSD_CHUNK
    kern = functools.partial(_ssd_kernel, seq=seq, width=width)
    full = lambda *shape: pl.BlockSpec(shape, lambda b: (0,) * len(shape))
    dtb_r = dt_bias.reshape(1, 16)
    dtb_c = dt_bias.reshape(16, 1)
    al_r = a_log.reshape(1, 16)
    al_c = a_log.reshape(16, 1)
    st_spec = pl.BlockSpec((None, N_DIR, SSD_PAIRS, SSD_STATE, SSD_PAIR), lambda b: (b, 0, 0, 0, 0))
    return pl.pallas_call(
        kern,
        grid=(nseq,),
        in_specs=[
            pl.BlockSpec((seq, SSD_CONV_DIM), lambda b: (blk0 + b, COL_XBC // SSD_CONV_DIM)),
            pl.BlockSpec((seq, SMALL_W), lambda b: (blk0 + b, COL_SMALL // SMALL_W)),
            pl.BlockSpec((32, seq), lambda b: (0, blk0 + b)),
            full(3, SSD_CONV_DIM), full(1, SSD_CONV_DIM), full(1, 16), full(16, 1), full(1, 16), full(16, 1),
            full(1, SSD_WIDTH), st_spec,
        ],
        out_specs=[pl.BlockSpec((seq, SSD_WIDTH), lambda b: (b, 0)), st_spec],
        out_shape=[jax.ShapeDtypeStruct((nseq * seq, SSD_WIDTH), F32),
                   jax.ShapeDtypeStruct((nseq, N_DIR, SSD_PAIRS, SSD_STATE, SSD_PAIR), F32)],
        scratch_shapes=[
            pltpu.VMEM((seq, SSD_WIDTH), F32), pltpu.VMEM((seq, 4 * SSD_STATE), F32),
            pltpu.VMEM((seq, 16), F32), pltpu.VMEM((nck, 16, SSD_CHUNK), F32),
            pltpu.VMEM((nck, 16, SSD_CHUNK), F32),
            pltpu.VMEM((N_DIR, SSD_PAIRS, SSD_STATE, SSD_PAIR), F32),
        ],
        compiler_params=_cparams("parallel"),
        name="ssd_seq%d" % seq,
    )(proj, proj, small_t, conv_w, conv_b.reshape(1, -1), dtb_r, dtb_c, al_r, al_c, dvec, h0)


def _ssd_state_to_pairs(h):
    lead = h.shape[:-3]
    t = h.reshape(lead + (SSD_PAIRS, 2, SSD_HEADDIM, SSD_STATE))
    t = jnp.moveaxis(t, -1, -3)
    return t.reshape(lead + (SSD_PAIRS, SSD_STATE, SSD_PAIR))


def _ssd_state_from_pairs(hp):
    lead = hp.shape[:-3]
    t = hp.reshape(lead + (SSD_PAIRS, SSD_STATE, 2, SSD_HEADDIM))
    t = jnp.moveaxis(t, -3, -1)
    return t.reshape(lead + (SSD_HEADS, SSD_HEADDIM, SSD_STATE))


DN_ST = DN_HEADS * DN_CHUNK
DN_PAR = 2


def _dn_kernel(qkv_ref, sm_ref, arow_ref, cw_ref, dtb_r_ref, alog_r_ref, dtb_st_ref, alog_st_ref, s0_ref,
               o_ref, sf_ref, q_scr, k_scr, v_scr, b_scr, g_scr, s_scr, u_scr, wq_scr, a_scr, kd_scr, gl_scr,
               *, seq, width):
    nck = seq // DN_CHUNK

    def pre(bi, carry):
        r0 = pl.multiple_of(bi * PRE_ROWS, PRE_ROWS)
        rows = pl.ds(r0, PRE_ROWS)
        xc = _silu(_conv_block(qkv_ref, cw_ref, r0, seq, width))
        for h in range(DN_HEADS):
            q = xc[:, h * DN_DK:(h + 1) * DN_DK]
            k = xc[:, DN_QK + h * DN_DK:DN_QK + (h + 1) * DN_DK]
            q_scr[rows, h * DN_DK:(h + 1) * DN_DK] = (
                q * lax.rsqrt(jnp.sum(q * q, axis=-1, keepdims=True) + EPS) * (DN_DK ** -0.5))
            k_scr[rows, h * DN_DK:(h + 1) * DN_DK] = (
                k * lax.rsqrt(jnp.sum(k * k, axis=-1, keepdims=True) + EPS))
        v_scr[rows, :] = xc[:, 2 * DN_QK:]
        b_scr[rows, :] = _sigmoid(sm_ref[rows, 16:24])
        g_scr[rows, :] = -jnp.exp(alog_r_ref[...]) * _softplus(sm_ref[rows, 24:32] + dtb_r_ref[...])
        o_ref[rows, :] = jnp.zeros((PRE_ROWS, DN_V), F32)
        return carry

    lax.fori_loop(0, seq // PRE_ROWS, pre, 0)
    s_scr[...] = s0_ref[...]

    r = lax.broadcasted_iota(jnp.int32, (DN_ST, DN_ST), 0)
    c = lax.broadcasted_iota(jnp.int32, (DN_ST, DN_ST), 1)
    same = (r // DN_CHUNK) == (c // DN_CHUNK)
    eye = (r == c).astype(F32)
    tril64 = _tri(DN_CHUNK, True).astype(BF16)
    triu64 = _tri(DN_CHUNK, False).astype(BF16)

    def chunk_problem(d, ci):
        r0 = pl.multiple_of(ci * DN_CHUNK, DN_CHUNK)
        rows = pl.ds(r0, DN_CHUNK)
        incl = jnp.logical_and(same, (r >= c) if d == 0 else (r <= c))
        strict = jnp.logical_and(same, (r > c) if d == 0 else (r < c))
        gc_c = _dot_exact_lhs(tril64 if d == 0 else triu64, g_scr[rows, :])
        g_row = -jnp.exp(alog_st_ref[d:d + 1, :]) * _softplus(arow_ref[ci, d:d + 1, :] + dtb_st_ref[d:d + 1, :])
        cum = jnp.logical_and(same, (r <= c) if d == 0 else (r >= c))
        gc_r = _dot_exact_rhs(jnp.broadcast_to(g_row, (8, DN_ST)),
                              jnp.where(cum, 1.0, 0.0).astype(BF16))[0:1, :]
        beta = b_scr[rows, :]
        end_row = DN_CHUNK - 1 if d == 0 else 0
        k_st, q_st, v_st, bt_st, gc_st, gl_st = [], [], [], [], [], []
        for h in range(DN_HEADS):
            ln = d * DN_HEADS + h
            k_st.append(k_scr[rows, h * DN_DK:(h + 1) * DN_DK])
            q_st.append(q_scr[rows, h * DN_DK:(h + 1) * DN_DK])
            v_st.append(v_scr[rows, h * DN_DV:(h + 1) * DN_DV])
            bt_st.append(beta[:, ln:ln + 1])
            col = gc_c[:, ln:ln + 1]
            gc_st.append(col)
            gl_st.append(col[end_row:end_row + 1, :])
        kst = jnp.concatenate(k_st, axis=0)
        qst = jnp.concatenate(q_st, axis=0)
        vst = jnp.concatenate(v_st, axis=0)
        bst = jnp.concatenate(bt_st, axis=0)
        gst = jnp.concatenate(gc_st, axis=0)
        decay = jnp.where(incl, jnp.exp(jnp.where(incl, gst - gc_r, 0.0)), 0.0)
        kb = kst * bst
        kstb = kst.astype(BF16)
        m = jnp.where(strict, _bdot_nt(kb, kstb) * decay, 0.0)
        attn = jnp.where(incl, _bdot_nt(qst, kstb) * decay, 0.0)
        a_scr[d, ci] = attn.astype(BF16)
        yield
        t = eye - m
        pb = m.astype(BF16)
        p = jnp.dot(pb, pb, preferred_element_type=F32)
        yield
        for lvl in range(5):
            pb = p.astype(BF16)
            t = t + jnp.dot(t.astype(BF16), pb, preferred_element_type=F32)
            if lvl < 4:
                p = jnp.dot(pb, pb, preferred_element_type=F32)
            yield
        tb = t.astype(BF16)
        rhs = jnp.concatenate([vst * bst, kb * jnp.exp(gst)], axis=1)
        x0 = jnp.dot(tb, rhs.astype(BF16), preferred_element_type=F32)
        yield
        res = rhs - x0 - _dot3(m, x0)
        yield
        uw = x0 + jnp.dot(tb, res.astype(BF16), preferred_element_type=F32)
        yield
        qg = qst * jnp.exp(gst)
        u_scr[d, ci] = uw[:, :DN_DV]
        kdec = []
        for h in range(DN_HEADS):
            hs = slice(h * DN_CHUNK, (h + 1) * DN_CHUNK)
            wq_scr[d, ci, h] = jnp.concatenate([uw[hs, DN_DV:], qg[hs, :]], axis=0).astype(BF16)
            gl = gl_st[h]
            kdec.append(k_st[h] * jnp.exp(gl - gc_st[h]))
            gl_scr[d, ci, h:h + 1, :] = jnp.broadcast_to(jnp.exp(gl), (1, DN_DV))
        kd_scr[d, ci] = jnp.concatenate(kdec, axis=0).astype(BF16)

    def chunk_step(kk, carry):
        _round_robin([chunk_problem(d, kk * DN_PAR + j) for j in range(DN_PAR) for d in range(N_DIR)])
        return carry

    lax.fori_loop(0, nck // DN_PAR, chunk_step, 0)

    def state_problem(d, ci):
        rows = pl.ds(pl.multiple_of(ci * DN_CHUNK, DN_CHUNK), DN_CHUNK)
        s_old, vnew, qs_all = [], [], []
        for h in range(DN_HEADS):
            hs = slice(h * DN_CHUNK, (h + 1) * DN_CHUNK)
            s_h = s_scr[d, h]
            ws = jnp.dot(wq_scr[d, ci, h], s_h.astype(BF16), preferred_element_type=F32)
            s_old.append(s_h)
            vnew.append((u_scr[d, ci, hs, :] - ws[:DN_CHUNK]).astype(BF16))
            qs_all.append(ws[DN_CHUNK:])
        yield
        o_st = jnp.concatenate(qs_all, axis=0) + jnp.dot(
            a_scr[d, ci], jnp.concatenate(vnew, axis=0), preferred_element_type=F32)
        for h in range(DN_HEADS):
            hs = slice(h * DN_CHUNK, (h + 1) * DN_CHUNK)
            s_scr[d, h] = s_old[h] * gl_scr[d, ci, h:h + 1, :] + lax.dot_general(
                kd_scr[d, ci, hs, :], vnew[h], (((0,), (0,)), ((), ())), preferred_element_type=F32)
        yield
        for h in range(DN_HEADS):
            hs = slice(h * DN_CHUNK, (h + 1) * DN_CHUNK)
            o_ref[rows, h * DN_DV:(h + 1) * DN_DV] += o_st[hs, :]

    def state_step(kk, carry):
        _round_robin([state_problem(0, kk), state_problem(1, nck - 1 - kk)])
        return carry

    lax.fori_loop(0, nck, state_step, 0)
    sf_ref[...] = s_scr[...]


def _dn(proj, arow, row0, nseq, seq, width, conv_w, dt_bias, a_log, s0):
    blk0 = row0 // seq
    nck = seq // DN_CHUNK
    kern = functools.partial(_dn_kernel, seq=seq, width=width)
    full = lambda *shape: pl.BlockSpec(shape, lambda b: (0,) * len(shape))
    dtb_r = dt_bias.reshape(1, 8)
    al_r = a_log.reshape(1, 8)
    dtb_st = jnp.repeat(dt_bias, DN_CHUNK, axis=1)
    al_st = jnp.repeat(a_log, DN_CHUNK, axis=1)
    return pl.pallas_call(
        kern,
        grid=(nseq,),
        in_specs=[
            pl.BlockSpec((seq, DN_CONV_DIM), lambda b: (blk0 + b, COL_QKV // DN_CONV_DIM)),
            pl.BlockSpec((seq, SMALL_W), lambda b: (blk0 + b, COL_SMALL // SMALL_W)),
            pl.BlockSpec((None, nck, N_DIR, DN_ST), lambda b: (b, 0, 0, 0)),
            full(3, DN_CONV_DIM), full(1, 8), full(1, 8), full(2, DN_ST), full(2, DN_ST),
            pl.BlockSpec((None, N_DIR, DN_HEADS, DN_DK, DN_DV), lambda b: (b, 0, 0, 0, 0)),
        ],
        out_specs=[
            pl.BlockSpec((seq, DN_V), lambda b: (b, 0)),
            pl.BlockSpec((None, N_DIR, DN_HEADS, DN_DK, DN_DV), lambda b: (b, 0, 0, 0, 0)),
        ],
        out_shape=[jax.ShapeDtypeStruct((nseq * seq, DN_V), F32),
                   jax.ShapeDtypeStruct((nseq, N_DIR, DN_HEADS, DN_DK, DN_DV), F32)],
        scratch_shapes=[
            pltpu.VMEM((seq, DN_QK), F32), pltpu.VMEM((seq, DN_QK), F32), pltpu.VMEM((seq, DN_V), F32),
            pltpu.VMEM((seq, 8), F32), pltpu.VMEM((seq, 8), F32),
            pltpu.VMEM((N_DIR, DN_HEADS, DN_DK, DN_DV), F32),
            pltpu.VMEM((N_DIR, nck, DN_ST, DN_DV), F32),
            pltpu.VMEM((N_DIR, nck, DN_HEADS, 2 * DN_CHUNK, DN_DK), BF16),
            pltpu.VMEM((N_DIR, nck, DN_ST, DN_ST), BF16),
            pltpu.VMEM((N_DIR, nck, DN_ST, DN_DK), BF16),
            pltpu.VMEM((N_DIR, nck, 8, DN_DV), F32),
        ],
        compiler_params=_cparams("parallel"),
        name="dn_seq%d" % seq,
    )(proj, proj, arow, conv_w, dtb_r, al_r, dtb_st, al_st, s0)


def _dn_gate_rows(a_raw, nseq, seq):
    nck = seq // DN_CHUNK
    t = a_raw.reshape(nseq, nck, DN_CHUNK, N_DIR, DN_HEADS)
    return t.transpose(0, 1, 3, 4, 2).reshape(nseq, nck, N_DIR, DN_ST)


MRG_TM = 512


def _merge_kernel(x_ref, gt_ref, y5_ref, ysc_ref, ysl_ref, z_ref, odc_ref, odl_ref, dg_ref, gr_ref, glu_b,
                  sng_ref, swo_b, dng_ref, dwo_b, wo_b, o_ref):
    i = pl.program_id(0)
    grp = _row_group(i, MRG_TM)
    is_ctx = i < N_CTX // MRG_TM

    g5 = jax.nn.gelu(y5_ref[...]).astype(BF16)
    br_a = (jnp.dot(g5, glu_b[0], preferred_element_type=F32)
            * _sigmoid(jnp.dot(g5, glu_b[1], preferred_element_type=F32)))
    ys = jnp.where(is_ctx, ysc_ref[...], ysl_ref[...]) * _silu(z_ref[...])
    ys = ys * lax.rsqrt(jnp.mean(ys * ys, axis=-1, keepdims=True) + EPS) * sng_ref[...]
    br_b = jnp.dot(ys.astype(BF16), swo_b[...], preferred_element_type=F32)
    od = jnp.where(is_ctx, odc_ref[...], odl_ref[...])
    parts = []
    for h in range(DN_HEADS):
        oh = od[:, h * DN_DV:(h + 1) * DN_DV]
        parts.append(oh * lax.rsqrt(jnp.mean(oh * oh, axis=-1, keepdims=True) + EPS) * dng_ref[...])
    on = jnp.concatenate(parts, axis=1) * _silu(dg_ref[...])
    br_c = jnp.dot(on.astype(BF16), dwo_b[...], preferred_element_type=F32)
    merged = (_sigmoid(gr_ref[:, 0:D_MODEL]) * br_a
              + _sigmoid(gr_ref[:, D_MODEL:2 * D_MODEL]) * br_b
              + _sigmoid(gr_ref[:, 2 * D_MODEL:3 * D_MODEL]) * br_c)
    out = jnp.dot(merged.astype(BF16), wo_b[...], preferred_element_type=F32)
    o_ref[...] = x_ref[...] + gt_ref[pl.ds(grp, 1), :] * out


def _merge(x, mod, proj, y5, ys_c, ys_l, od_c, od_l, s5_glu_l, ssd_norm_g_l, ssd_w_out_l, dn_norm_g_l,
           dn_w_out_l, w_out_l):
    nctx = N_CTX // MRG_TM
    rowblk = lambda w, col: pl.BlockSpec((MRG_TM, w), lambda i: (i, col // w))
    ctxblk = lambda w: pl.BlockSpec((MRG_TM, w), lambda i: (jnp.minimum(i, nctx - 1), 0))
    latblk = lambda w: pl.BlockSpec((MRG_TM, w), lambda i: (jnp.maximum(i - nctx, 0), 0))
    full = lambda *shape: pl.BlockSpec(shape, lambda i: (0,) * len(shape), pipeline_mode=pl.Buffered(1))
    return pl.pallas_call(
        _merge_kernel,
        grid=(N_ROWS // MRG_TM,),
        in_specs=[
            rowblk(D_MODEL, 0),
            pl.BlockSpec((8, D_MODEL), lambda i: (0, 5)),
            rowblk(S5_WIDTH, 0), ctxblk(SSD_WIDTH), latblk(SSD_WIDTH), rowblk(SSD_WIDTH, COL_Z),
            ctxblk(DN_V), latblk(DN_V), rowblk(DN_V, COL_DNG), rowblk(3 * D_MODEL, COL_GATES),
            full(2, S5_WIDTH, D_MODEL), full(1, SSD_WIDTH), full(SSD_WIDTH, D_MODEL),
            full(1, DN_DV), full(DN_V, D_MODEL), full(D_MODEL, D_MODEL),
        ],
        out_specs=rowblk(D_MODEL, 0),
        out_shape=jax.ShapeDtypeStruct((N_ROWS, D_MODEL), F32),
        compiler_params=_cparams("parallel"),
        name="merge",
    )(x, mod, y5, ys_c, ys_l, proj, od_c, od_l, proj, proj, s5_glu_l.astype(BF16),
      ssd_norm_g_l.reshape(1, -1), ssd_w_out_l.astype(BF16), dn_norm_g_l.reshape(1, -1),
      dn_w_out_l.astype(BF16), w_out_l.astype(BF16))


FIN_TM = 1024


def _final_norm_kernel(x_ref, g_ref, o_ref):
    x = x_ref[...]
    o_ref[...] = x * lax.rsqrt(jnp.mean(x * x, axis=-1, keepdims=True) + EPS) * g_ref[...]


def _final_norm(x, g):
    return pl.pallas_call(
        _final_norm_kernel,
        grid=(N_ROWS // FIN_TM,),
        in_specs=[pl.BlockSpec((FIN_TM, D_MODEL), lambda i: (i, 0)), pl.BlockSpec((1, D_MODEL), lambda i: (0, 0))],
        out_specs=pl.BlockSpec((FIN_TM, D_MODEL), lambda i: (i, 0)),
        out_shape=jax.ShapeDtypeStruct((N_ROWS, D_MODEL), F32),
        compiler_params=_cparams("parallel"),
        name="final_norm",
    )(x, g.reshape(1, -1))


def kernel(x_prompt, x_sample, state_s5_re, state_s5_im, state_ssd, state_dn, c, c_ctx, ada_w, ada_b, norm_g, ffn_wi, ffn_wo, w_in, s5_lam_re, s5_lam_im, s5_log_dt, s5_b_re, s5_b_im, s5_c_re, s5_c_im, s5_d, s5_glu, ssd_conv_w, ssd_conv_b, ssd_dt_bias, ssd_a_log, ssd_d, ssd_norm_g, ssd_w_out, dn_conv_w, dn_dt_bias, dn_a_log, dn_norm_g, dn_w_out, w_out, final_norm_g):
    x = jnp.concatenate([x_prompt.reshape(N_CTX, D_MODEL), x_sample.reshape(N_LAT, D_MODEL)], axis=0)
    cond8 = jnp.concatenate([c_ctx[None, :], c, jnp.zeros((8 - 1 - DEC_BATCH, D_MODEL), F32)], axis=0)
    mods = _ada_mods(cond8, ada_w, ada_b)

    mmat, ptab, qtab, atab = _s5_prep(s5_lam_re, s5_lam_im, s5_log_dt, s5_b_re, s5_b_im, s5_c_re, s5_c_im)
    p_re = jnp.concatenate([ptab[:, :, 0], ptab[:, :, 2]], axis=-1)
    p_im = jnp.concatenate([ptab[:, :, 1], ptab[:, :, 3]], axis=-1)
    q_re = jnp.concatenate([qtab[:, :, 0], qtab[:, :, 2]], axis=-1)
    q_im = jnp.concatenate([qtab[:, :, 1], qtab[:, :, 3]], axis=-1)
    a_re = jnp.concatenate([atab[:, :, 0], atab[:, :, 2]], axis=-1)[:, :, None, :]
    a_im = jnp.concatenate([atab[:, :, 1], atab[:, :, 3]], axis=-1)[:, :, None, :]
    s5_d_rows = jnp.tile(s5_d.reshape(DEPTH, S5_GROUPS, 1, S5_GROUP), (1, 1, 1, S5_CHUNK))
    ssd_d_rows = jnp.repeat(ssd_d, SSD_HEADDIM, axis=1).reshape(DEPTH, 1, SSD_WIDTH)

    def s5_h0(state):
        return state.transpose(1, 3, 0, 2, 4).reshape(DEPTH, S5_GROUPS, DEC_BATCH, 2 * S5_STATE)

    h0_re = s5_h0(state_s5_re)
    h0_im = s5_h0(state_s5_im)
    ssd_h0_lat = _ssd_state_to_pairs(jnp.swapaxes(state_ssd, 0, 1))
    ssd_h0_ctx = jnp.zeros((BATCH, N_DIR, SSD_PAIRS, SSD_STATE, SSD_PAIR), F32)
    dn_s0_lat = jnp.swapaxes(state_dn, 0, 1)
    dn_s0_ctx = jnp.zeros((BATCH, N_DIR, DN_HEADS, DN_DK, DN_DV), F32)

    new_s5_re, new_s5_im, new_ssd, new_dn = [], [], [], []
    for l in range(DEPTH):
        mod = mods[l]
        x = _ffn(x, mod, norm_g[l, 0:1], ffn_wi, ffn_wo, l, 0)
        proj = _inproj(x, mod, norm_g[l, 1:2], _repack_w_in(w_in[l]))

        y5_g, f_re, f_im = _s5_scan(_s5_to_groups(proj[:, COL_U:COL_U + S5_WIDTH]), mmat[l], p_re[l], p_im[l],
                                    q_re[l], q_im[l], a_re[l], a_im[l], s5_d_rows[l], h0_re[l], h0_im[l])
        y5 = _s5_from_groups(y5_g)
        new_s5_re.append(f_re.reshape(S5_GROUPS, BATCH, N_DIR, S5_STATE).transpose(1, 2, 0, 3))
        new_s5_im.append(f_im.reshape(S5_GROUPS, BATCH, N_DIR, S5_STATE).transpose(1, 2, 0, 3))

        small = proj[:, COL_SMALL:COL_SMALL + 32]
        small_t = small.T
        ys_c, hs_c = _ssd(proj, small_t, 0, BATCH, SEQ, SEQ, ssd_conv_w[l], ssd_conv_b[l], ssd_dt_bias[l],
                          ssd_a_log[l], ssd_d_rows[l], ssd_h0_ctx)
        ys_l, _ = _ssd(proj, small_t, N_CTX, DEC_BATCH, DEC_SEQ, GRID_W, ssd_conv_w[l], ssd_conv_b[l],
                       ssd_dt_bias[l], ssd_a_log[l], ssd_d_rows[l], ssd_h0_lat[l])
        new_ssd.append(_ssd_state_from_pairs(hs_c))
        a_raw = small[:, 24:32]
        od_c, sd_c = _dn(proj, _dn_gate_rows(a_raw[:N_CTX], BATCH, SEQ), 0, BATCH, SEQ, SEQ, dn_conv_w[l],
                         dn_dt_bias[l], dn_a_log[l], dn_s0_ctx)
        od_l, _ = _dn(proj, _dn_gate_rows(a_raw[N_CTX:], DEC_BATCH, DEC_SEQ), N_CTX, DEC_BATCH, DEC_SEQ, GRID_W,
                      dn_conv_w[l], dn_dt_bias[l], dn_a_log[l], dn_s0_lat[l])
        new_dn.append(sd_c)

        x = _merge(x, mod, proj, y5, ys_c, ys_l, od_c, od_l, s5_glu[l], ssd_norm_g[l], ssd_w_out[l],
                   dn_norm_g[l], dn_w_out[l], w_out[l])
        x = _ffn(x, mod, norm_g[l, 2:3], ffn_wi, ffn_wo, l, 1)

    y = _final_norm(x, final_norm_g)
    y_prompt = y[:N_CTX].reshape(BATCH, SEQ, D_MODEL)
    y_sample = y[N_CTX:].reshape(DEC_BATCH, DEC_SEQ, D_MODEL)
    return (y_prompt, y_sample, jnp.stack(new_s5_re, axis=1), jnp.stack(new_s5_im, axis=1),
            jnp.stack(new_ssd, axis=1), jnp.stack(new_dn, axis=1))
```

```python
import functools

import jax
import jax.numpy as jnp
import numpy as np
from jax import lax
from jax.experimental import pallas as pl
from jax.experimental.pallas import tpu as pltpu

F32 = jnp.float32
BF16 = jnp.bfloat16

D_MODEL = 1024
BATCH = 16
SEQ = 256
DEPTH = 4
DEC_BATCH = 2
DEC_SEQ = 1024
GRID_W = 64
N_DIR = 2
N_ADA = 9
D_FF = 2816
EPS = 1e-6

S5_WIDTH = 512
S5_GROUP = 16
S5_GROUPS = 32
S5_STATE = 64
S5_CHUNK = 16

SSD_WIDTH = 512
SSD_HEADDIM = 64
SSD_HEADS = 8
SSD_GROUPS = 2
SSD_STATE = 64
SSD_CHUNK = 128
SSD_CONV_DIM = 768

DN_HEADS = 4
DN_DK = 128
DN_DV = 128
DN_QK = 512
DN_V = 512
DN_CHUNK = 64
DN_CONV_DIM = 1536

IN_SEGMENTS = (512, 512, 768, 16, 1536, 8, 8, 512, 3072)
IN_SPLITS = tuple(int(s) for s in np.cumsum(IN_SEGMENTS)[:-1])

N_CTX = BATCH * SEQ
N_LAT = DEC_BATCH * DEC_SEQ
N_ROWS = N_CTX + N_LAT

COL_GATES = 0
COL_QKV = 3072
COL_U = 4608
COL_Z = 5120
COL_DNG = 5632
COL_XBC = 6144
COL_SMALL = 6912
PROJ_W = 7040
SMALL_W = 128

VMEM_LIMIT = 56 * 1024 * 1024


def _cparams(*sem):
    return pltpu.CompilerParams(dimension_semantics=sem, vmem_limit_bytes=VMEM_LIMIT)


def _sigmoid(x):
    return jax.nn.sigmoid(x)


def _silu(x):
    return x * jax.nn.sigmoid(x)


def _softplus(x):
    return jnp.maximum(x, 0.0) + jnp.log(1.0 + jnp.exp(-jnp.abs(x)))


def _bdot(a, b):
    return jnp.dot(a.astype(BF16), b.astype(BF16), preferred_element_type=F32)


def _bdot_nt(a, b):
    return lax.dot_general(a.astype(BF16), b.astype(BF16), (((1,), (1,)), ((), ())),
                           preferred_element_type=F32)


def _bdot_tn(a, b):
    return lax.dot_general(a.astype(BF16), b.astype(BF16), (((0,), (0,)), ((), ())),
                           preferred_element_type=F32)


def _split3(a):
    hi = a.astype(BF16)
    r = a - hi.astype(F32)
    mid = r.astype(BF16)
    lo = (r - mid.astype(F32)).astype(BF16)
    return hi, mid, lo


def _dot3(a, b):
    ah = a.astype(BF16)
    al = (a - ah.astype(F32)).astype(BF16)
    bh = b.astype(BF16)
    bl = (b - bh.astype(F32)).astype(BF16)
    out = jnp.dot(ah, bh, preferred_element_type=F32)
    out = out + jnp.dot(ah, bl, preferred_element_type=F32)
    out = out + jnp.dot(al, bh, preferred_element_type=F32)
    return out


def _dot_exact_lhs(t_bf16, x):
    hi, mid, lo = _split3(x)
    out = jnp.dot(t_bf16, hi, preferred_element_type=F32)
    out = out + jnp.dot(t_bf16, mid, preferred_element_type=F32)
    out = out + jnp.dot(t_bf16, lo, preferred_element_type=F32)
    return out


def _dot_exact_rhs(x, t_bf16):
    hi, mid, lo = _split3(x)
    out = jnp.dot(hi, t_bf16, preferred_element_type=F32)
    out = out + jnp.dot(mid, t_bf16, preferred_element_type=F32)
    out = out + jnp.dot(lo, t_bf16, preferred_element_type=F32)
    return out


def _norm_mod(x, g, sc, sh):
    ms = jnp.mean(x * x, axis=-1, keepdims=True)
    y = x * lax.rsqrt(ms + EPS) * g
    return y * (1.0 + sc) + sh


def _row_group(i, tm):
    nctx = N_CTX // tm
    per = DEC_SEQ // tm
    return jnp.where(i < nctx, 0, 1 + jnp.maximum(i - nctx, 0) // per)


ADA_TN = 1152


def _ada_kernel(c_ref, w_ref, b_ref, o_ref):
    c = c_ref[...]
    o_ref[...] = _bdot(_silu(c), w_ref[...]) + b_ref[...]


def _ada_mods(cond8, ada_w, ada_b):
    nj = (N_ADA * D_MODEL) // ADA_TN
    return pl.pallas_call(
        _ada_kernel,
        grid=(DEPTH, nj),
        in_specs=[
            pl.BlockSpec((8, D_MODEL), lambda l, j: (0, 0)),
            pl.BlockSpec((None, D_MODEL, ADA_TN), lambda l, j: (l, 0, j)),
            pl.BlockSpec((None, 1, ADA_TN), lambda l, j: (l, 0, j)),
        ],
        out_specs=pl.BlockSpec((None, 8, ADA_TN), lambda l, j: (l, 0, j)),
        out_shape=jax.ShapeDtypeStruct((DEPTH, 8, N_ADA * D_MODEL), F32),
        compiler_params=_cparams("parallel", "parallel"),
        name="ada_mods",
    )(cond8, ada_w, ada_b.reshape(DEPTH, 1, N_ADA * D_MODEL))


FFN_TM = 1024
FFN_TF = 256


def _ffn_kernel(x_ref, sh_ref, sc_ref, gt_ref, g_ref, wa_ref, wb_ref, wo_ref, o_ref, h_scr, acc_scr):
    i = pl.program_id(0)
    j = pl.program_id(1)
    grp = _row_group(i, FFN_TM)

    @pl.when(j == 0)
    def _():
        h = _norm_mod(x_ref[...], g_ref[...], sc_ref[pl.ds(grp, 1), :], sh_ref[pl.ds(grp, 1), :])
        h_scr[...] = h.astype(BF16)
        acc_scr[...] = jnp.zeros_like(acc_scr)

    h = h_scr[...]
    a = jnp.dot(h, wa_ref[...].astype(BF16), preferred_element_type=F32)
    b = jnp.dot(h, wb_ref[...].astype(BF16), preferred_element_type=F32)
    u = (_silu(a) * b).astype(BF16)
    acc_scr[...] += jnp.dot(u, wo_ref[...].astype(BF16), preferred_element_type=F32)

    @pl.when(j == pl.num_programs(1) - 1)
    def _():
        o_ref[...] = x_ref[...] + (0.5 * gt_ref[pl.ds(grp, 1), :]) * acc_scr[...]


def _ffn(x, mod, norm_g_row, ffn_wi, ffn_wo, layer, which):
    nf = D_FF // FFN_TF
    base = 0 if which == 0 else 6
    return pl.pallas_call(
        _ffn_kernel,
        grid=(N_ROWS // FFN_TM, nf),
        in_specs=[
            pl.BlockSpec((FFN_TM, D_MODEL), lambda i, j: (i, 0)),
            pl.BlockSpec((8, D_MODEL), lambda i, j: (0, base)),
            pl.BlockSpec((8, D_MODEL), lambda i, j: (0, base + 1)),
            pl.BlockSpec((8, D_MODEL), lambda i, j: (0, base + 2)),
            pl.BlockSpec((1, D_MODEL), lambda i, j: (0, 0)),
            pl.BlockSpec((None, None, D_MODEL, FFN_TF), lambda i, j: (layer, which, 0, j)),
            pl.BlockSpec((None, None, D_MODEL, FFN_TF), lambda i, j: (layer, which, 0, j + nf)),
            pl.BlockSpec((None, None, FFN_TF, D_MODEL), lambda i, j: (layer, which, j, 0)),
        ],
        out_specs=pl.BlockSpec((FFN_TM, D_MODEL), lambda i, j: (i, 0)),
        out_shape=jax.ShapeDtypeStruct((N_ROWS, D_MODEL), F32),
        scratch_shapes=[pltpu.VMEM((FFN_TM, D_MODEL), BF16), pltpu.VMEM((FFN_TM, D_MODEL), F32)],
        compiler_params=_cparams("parallel", "arbitrary"),
        name="ffn",
    )(x, mod, mod, mod, norm_g_row, ffn_wi, ffn_wi, ffn_wo)


INP_TM = 1024
INP_TN = 640


def _inproj_kernel(x_ref, sh_ref, sc_ref, g_ref, w_ref, o_ref, ot_ref, h_scr):
    i = pl.program_id(0)
    j = pl.program_id(1)
    grp = _row_group(i, INP_TM)

    @pl.when(j == 0)
    def _():
        h = _norm_mod(x_ref[...], g_ref[...], sc_ref[pl.ds(grp, 1), :], sh_ref[pl.ds(grp, 1), :])
        h_scr[...] = h.astype(BF16)

    res = jnp.dot(h_scr[...], w_ref[...], preferred_element_type=F32)
    o_ref[...] = res

    @pl.when(j == pl.num_programs(1) - 1)
    def _():
        ot_ref[...] = res[:, INP_TN - SMALL_W:].T


def _repack_w_in(w):
    u, z, xbc, dt, qkv, beta, a, dng, gates = jnp.split(w, IN_SPLITS, axis=-1)
    pad = jnp.zeros((w.shape[0], SMALL_W - 32), w.dtype)
    return jnp.concatenate([gates, qkv, u, z, dng, xbc, dt, beta, a, pad], axis=-1).astype(BF16)


def _inproj(x, mod, norm_g_row, w_packed):
    return pl.pallas_call(
        _inproj_kernel,
        grid=(N_ROWS // INP_TM, PROJ_W // INP_TN),
        in_specs=[
            pl.BlockSpec((INP_TM, D_MODEL), lambda i, j: (i, 0)),
            pl.BlockSpec((8, D_MODEL), lambda i, j: (0, 3)),
            pl.BlockSpec((8, D_MODEL), lambda i, j: (0, 4)),
            pl.BlockSpec((1, D_MODEL), lambda i, j: (0, 0)),
            pl.BlockSpec((D_MODEL, INP_TN), lambda i, j: (0, j)),
        ],
        out_specs=[pl.BlockSpec((INP_TM, INP_TN), lambda i, j: (i, j)),
                   pl.BlockSpec((SMALL_W, INP_TM), lambda i, j: (0, i))],
        out_shape=[jax.ShapeDtypeStruct((N_ROWS, PROJ_W), F32),
                   jax.ShapeDtypeStruct((SMALL_W, N_ROWS), F32)],
        scratch_shapes=[pltpu.VMEM((INP_TM, D_MODEL), BF16)],
        compiler_params=_cparams("parallel", "arbitrary"),
        name="inproj",
    )(x, mod, mod, norm_g_row, w_packed)


S5_ROW = S5_CHUNK * S5_GROUP


def _s5_prep_kernel(lam_re_ref, lam_im_ref, ldt_ref, btr_ref, bti_ref, cr_ref, ci_ref, ctr_ref, cti_ref,
                    m_ref, p_ref, q_ref, a_ref):
    tau = lax.broadcasted_iota(jnp.int32, (S5_CHUNK, 1), 0).astype(F32)
    lane_t = lax.broadcasted_iota(jnp.int32, (1, S5_ROW), 1) // S5_GROUP
    mmat = jnp.zeros((S5_ROW, S5_ROW), F32)

    for d in range(N_DIR):
        lr = lam_re_ref[d:d + 1, :]
        li = lam_im_ref[d:d + 1, :]
        dt = jnp.exp(ldt_ref[d:d + 1, :])
        mag = jnp.exp(lr * dt)
        lb_re = mag * jnp.cos(li * dt)
        lb_im = mag * jnp.sin(li * dt)
        den = lr * lr + li * li
        cr = ((lb_re - 1.0) * lr + lb_im * li) / den
        ci = (lb_im * lr - (lb_re - 1.0) * li) / den
        bt_r = btr_ref[d]
        bt_i = bti_ref[d]
        bbt_r = cr * bt_r - ci * bt_i
        bbt_i = cr * bt_i + ci * bt_r
        c_r = cr_ref[d]
        c_i = ci_ref[d]

        def powtab(t):
            m = jnp.exp(t * (lr * dt))
            ang = t * (li * dt)
            return m * jnp.cos(ang), m * jnp.sin(ang)

        def outer(ar, ai, xr, xi):
            rr = ar[:, None, :] * xr[None, :, :] - ai[:, None, :] * xi[None, :, :]
            ii = ar[:, None, :] * xi[None, :, :] + ai[:, None, :] * xr[None, :, :]
            return rr.reshape(S5_ROW, S5_STATE), ii.reshape(S5_ROW, S5_STATE)

        t_in = (S5_CHUNK - 1) - tau if d == 0 else tau
        ar, ai = powtab(t_in)
        ba_r, ba_i = outer(ar, ai, bbt_r, bbt_i)
        p_ref[2 * d] = ba_r
        p_ref[2 * d + 1] = ba_i
        kt = _dot3(ba_r, ctr_ref[d]) - _dot3(ba_i, cti_ref[d])
        for s in range(S5_CHUNK):
            rows = s * S5_GROUP
            if d == 0:
                sh = kt if s == 0 else jnp.concatenate(
                    [kt[rows:, :], jnp.zeros((rows, S5_ROW), F32)], axis=0)
                mmat = mmat + jnp.where(lane_t == (S5_CHUNK - 1) - s, sh, 0.0)
            else:
                sh = kt if s == 0 else jnp.concatenate(
                    [jnp.zeros((rows, S5_ROW), F32), kt[:S5_ROW - rows, :]], axis=0)
                mmat = mmat + jnp.where(lane_t == s, sh, 0.0)
        t_out = tau + 1.0 if d == 0 else S5_CHUNK - tau
        ar, ai = powtab(t_out)
        qr, qi = outer(ar, ai, c_r, c_i)
        q_ref[2 * d] = qr
        q_ref[2 * d + 1] = -qi
        a16r, a16i = powtab(jnp.full((1, 1), float(S5_CHUNK), F32))
        a_ref[2 * d:2 * d + 1, :] = a16r
        a_ref[2 * d + 1:2 * d + 2, :] = a16i

    m_ref[...] = mmat


def _s5_prep(lam_re, lam_im, log_dt, b_re, b_im, c_re, c_im):
    tg = lambda t: jnp.swapaxes(t, 1, 2)
    lam_re_g = tg(lam_re)
    lam_im_g = tg(lam_im)
    ldt_g = tg(log_dt)[..., None]
    bt_r = jnp.swapaxes(tg(b_re), -1, -2)
    bt_i = jnp.swapaxes(tg(b_im), -1, -2)
    c_r = tg(c_re)
    c_i = tg(c_im)
    ct_r = jnp.tile(jnp.swapaxes(c_r, -1, -2), (1, 1, 1, 1, S5_CHUNK))
    ct_i = jnp.tile(jnp.swapaxes(c_i, -1, -2), (1, 1, 1, 1, S5_CHUNK))

    def spec(*tail):
        n = len(tail)
        return pl.BlockSpec((None, None) + tail, lambda l, g: (l, g) + (0,) * n)

    return pl.pallas_call(
        _s5_prep_kernel,
        grid=(DEPTH, S5_GROUPS),
        in_specs=[spec(2, 64), spec(2, 64), spec(2, 1), spec(2, 16, 64), spec(2, 16, 64),
                  spec(2, 16, 64), spec(2, 16, 64), spec(2, 64, S5_ROW), spec(2, 64, S5_ROW)],
        out_specs=[spec(S5_ROW, S5_ROW), spec(4, S5_ROW, 64), spec(4, S5_ROW, 64), spec(4, 64)],
        out_shape=[jax.ShapeDtypeStruct((DEPTH, S5_GROUPS, S5_ROW, S5_ROW), F32),
                   jax.ShapeDtypeStruct((DEPTH, S5_GROUPS, 4, S5_ROW, 64), F32),
                   jax.ShapeDtypeStruct((DEPTH, S5_GROUPS, 4, S5_ROW, 64), F32),
                   jax.ShapeDtypeStruct((DEPTH, S5_GROUPS, 4, 64), F32)],
        compiler_params=_cparams("parallel", "parallel"),
        name="s5_prep",
    )(lam_re_g, lam_im_g, ldt_g, bt_r, bt_i, c_r, c_i, ct_r, ct_i)


S5_CTX_CH = SEQ // S5_CHUNK
S5_LAT_CH = DEC_SEQ // S5_CHUNK
S5_CTX_ROWS = S5_CTX_CH * BATCH
S5_LAT_ROWS = S5_LAT_CH * DEC_BATCH
S5_ROWS = S5_CTX_ROWS + S5_LAT_ROWS
S5_GB = 128 // S5_GROUP
S5_PERM = S5_GB * 128


def _s5_perm():
    src = np.arange(S5_PERM)
    s, g, j = src // 128, (src % 128) // S5_GROUP, src % S5_GROUP
    p = np.zeros((S5_PERM, S5_PERM), np.float32)
    p[src, g * 128 + s * S5_GROUP + j] = 1.0
    return jnp.asarray(p, BF16)


def _s5_kernel(u_ref, perm_ref, m_ref, pre_ref, pim_ref, qre_ref, qim_ref, are_ref, aim_ref, h0r_ref, h0i_ref,
               y_ref, fr_ref, fi_ref, ug, sre, sim, hfr, hfi, hbr, hbi, ys):
    perm = perm_ref[...]
    half_w = S5_ROW // 2
    for half in range(2):
        x = jnp.concatenate([u_ref[pl.ds(half * 8 + s, S5_ROWS, stride=S5_CHUNK), :] for s in range(8)], axis=1)
        z = jnp.dot(x.astype(BF16), perm, preferred_element_type=F32)
        for g in range(S5_GB):
            ug[g, :, half * half_w:(half + 1) * half_w] = z[:, g * 128:(g + 1) * 128].astype(BF16)
    for g in range(S5_GB):
        sre[g] = jnp.dot(ug[g], pre_ref[g].astype(BF16), preferred_element_type=F32)
        sim[g] = jnp.dot(ug[g], pim_ref[g].astype(BF16), preferred_element_type=F32)
    fwd = lax.broadcasted_iota(jnp.int32, (1, 2 * S5_STATE), 1) < S5_STATE

    def scan(base, nchunk, nseq, init):
        def step(k, hs):
            rf = pl.ds(base + k, nseq, stride=nchunk)
            rb = pl.ds(base + nchunk - 1 - k, nseq, stride=nchunk)
            out = []
            for g in range(S5_GB):
                h_re, h_im = hs[2 * g], hs[2 * g + 1]
                hfr.at[g][rf, :] = h_re
                hfi.at[g][rf, :] = h_im
                hbr.at[g][rb, :] = h_re
                hbi.at[g][rb, :] = h_im
                s_r = jnp.where(fwd, sre.at[g][rf, :], sre.at[g][rb, :])
                s_i = jnp.where(fwd, sim.at[g][rf, :], sim.at[g][rb, :])
                ar = are_ref[g]
                ai = aim_ref[g]
                out.append(ar * h_re - ai * h_im + s_r)
                out.append(ar * h_im + ai * h_re + s_i)
            return tuple(out)

        return lax.fori_loop(0, nchunk, step, init)

    zero = jnp.zeros((BATCH, 2 * S5_STATE), F32)
    fin = scan(0, S5_CTX_CH, BATCH, (zero,) * (2 * S5_GB))
    lat0 = []
    for g in range(S5_GB):
        fr_ref[g] = fin[2 * g]
        fi_ref[g] = fin[2 * g + 1]
        lat0 += [h0r_ref[g], h0i_ref[g]]
    scan(S5_CTX_ROWS, S5_LAT_CH, DEC_BATCH, tuple(lat0))

    for g in range(S5_GB):
        h_re = jnp.where(fwd, hfr[g], hbr[g])
        h_im = jnp.where(fwd, hfi[g], hbi[g])
        y = jnp.dot(ug[g], m_ref[g].astype(BF16), preferred_element_type=F32)
        ys[g] = y + _bdot_nt(h_re, qre_ref[g]) + _bdot_nt(h_im, qim_ref[g])
    for half in range(2):
        w = jnp.concatenate([ys[g, :, half * half_w:(half + 1) * half_w] for g in range(S5_GB)], axis=1)
        zo = sum(lax.dot_general(piece, perm, (((1,), (1,)), ((), ())), preferred_element_type=F32)
                 for piece in _split3(w))
        for t in range(8):
            y_ref[pl.ds(half * 8 + t, S5_ROWS, stride=S5_CHUNK), :] = zo[:, t * 128:(t + 1) * 128]


def _s5_scan(proj, perm, mmat, p_re, p_im, q_re, q_im, a_re, a_im, h0_re, h0_im):
    def gspec(*tail):
        n = len(tail)
        return pl.BlockSpec((S5_GB,) + tail, lambda t: (t,) + (0,) * n)

    st = 2 * S5_STATE
    return pl.pallas_call(
        _s5_kernel,
        grid=(S5_GROUPS // S5_GB,),
        in_specs=[pl.BlockSpec((N_ROWS, 128), lambda t: (0, COL_U // 128 + t)),
                  pl.BlockSpec((S5_PERM, S5_PERM), lambda t: (0, 0)),
                  gspec(S5_ROW, S5_ROW), gspec(S5_ROW, st), gspec(S5_ROW, st), gspec(S5_ROW, st),
                  gspec(S5_ROW, st), gspec(1, st), gspec(1, st), gspec(DEC_BATCH, st), gspec(DEC_BATCH, st)],
        out_specs=[pl.BlockSpec((N_ROWS, 128), lambda t: (0, t)), gspec(BATCH, st), gspec(BATCH, st)],
        out_shape=[jax.ShapeDtypeStruct((N_ROWS, S5_WIDTH), F32),
                   jax.ShapeDtypeStruct((S5_GROUPS, BATCH, st), F32),
                   jax.ShapeDtypeStruct((S5_GROUPS, BATCH, st), F32)],
        scratch_shapes=([pltpu.VMEM((S5_GB, S5_ROWS, S5_ROW), BF16)]
                        + [pltpu.VMEM((S5_GB, S5_ROWS, st), F32) for _ in range(6)]
                        + [pltpu.VMEM((S5_GB, S5_ROWS, S5_ROW), F32)]),
        compiler_params=_cparams("parallel"),
        name="s5_scan",
    )(proj, perm, mmat, p_re, p_im, q_re, q_im, a_re, a_im, h0_re, h0_im)


PRE_ROWS = 128


def _conv_block(x_ref, w_ref, r0, seq, width):
    x = x_ref[pl.ds(r0, PRE_ROWS), :]
    prev = x_ref[pl.ds(jnp.maximum(r0 - 1, 0), 1), :]
    nxt = x_ref[pl.ds(jnp.minimum(r0 + PRE_ROWS, seq - 1), 1), :]
    rid = lax.broadcasted_iota(jnp.int32, (PRE_ROWS, 1), 0)
    pos = (r0 + rid) % width
    xm = jnp.where(rid == 0, prev, pltpu.roll(x, 1, 0))
    xm = jnp.where(pos == 0, 0.0, xm)
    xp = jnp.where(rid == PRE_ROWS - 1, nxt, pltpu.roll(x, PRE_ROWS - 1, 0))
    xp = jnp.where(pos == width - 1, 0.0, xp)
    return xm * w_ref[0:1, :] + x * w_ref[1:2, :] + xp * w_ref[2:3, :]


def _round_robin(problems):
    live = list(problems)
    while live:
        nxt = []
        for p in live:
            try:
                next(p)
                nxt.append(p)
            except StopIteration:
                pass
        live = nxt


def _tri(n, lower):
    r = lax.broadcasted_iota(jnp.int32, (n, n), 0)
    c = lax.broadcasted_iota(jnp.int32, (n, n), 1)
    return (r >= c) if lower else (r <= c)


SSD_PAIR = 2 * SSD_HEADDIM
SSD_PAIRS = SSD_HEADS // 2


def _ssd_kernel(xbc_ref, sm_ref, smt_ref, cw_ref, cb_ref, dtb_r_ref, dtb_c_ref, alog_r_ref, alog_c_ref,
                dvec_ref, h0_ref, y_ref, hf_ref, xs_scr, bc_scr, dac_scr, dar_scr, dtr_scr, h_scr,
                *, seq, width):
    nck = seq // SSD_CHUNK

    def pre(bi, carry):
        r0 = pl.multiple_of(bi * PRE_ROWS, PRE_ROWS)
        rows = pl.ds(r0, PRE_ROWS)
        xc = _silu(_conv_block(xbc_ref, cw_ref, r0, seq, width) + cb_ref[...])
        xs = xc[:, :SSD_WIDTH]
        xs_scr[rows, :] = xs
        bc_scr[rows, :] = xc[:, SSD_WIDTH:]
        y_ref[rows, :] = dvec_ref[...] * xs
        dt_c = _softplus(sm_ref[rows, 0:16] + dtb_r_ref[...])
        dac_scr[rows, :] = dt_c * (-jnp.exp(alog_r_ref[...]))
        return carry

    lax.fori_loop(0, seq // PRE_ROWS, pre, 0)
    dt_r = _softplus(smt_ref[0:16, :] + dtb_c_ref[...])
    da_r = dt_r * (-jnp.exp(alog_c_ref[...]))
    for ck in range(nck):
        dtr_scr[ck] = dt_r[:, ck * SSD_CHUNK:(ck + 1) * SSD_CHUNK]
        dar_scr[ck] = da_r[:, ck * SSD_CHUNK:(ck + 1) * SSD_CHUNK]
    h_scr[...] = h0_ref[...]

    tril = _tri(SSD_CHUNK, True)
    triu = _tri(SSD_CHUNK, False)
    tril_b = tril.astype(BF16)
    triu_b = triu.astype(BF16)
    lo_half = lax.broadcasted_iota(jnp.int32, (1, SSD_PAIR), 1) < SSD_HEADDIM

    def chunk_problem(dirs, k):
        for d in dirs:
            c = k if d == 0 else nck - 1 - k
            r0 = pl.multiple_of(c * SSD_CHUNK, SSD_CHUNK)
            rows = pl.ds(r0, SSD_CHUNK)
            mask = tril if d == 0 else triu
            ac = _dot_exact_lhs(tril_b if d == 0 else triu_b, dac_scr[rows, :])
            at = _dot_exact_rhs(dar_scr[c], triu_b if d == 0 else tril_b)
            dt_row = dtr_scr[c]
            end = SSD_CHUNK - 1 if d == 0 else 0
            bcx = bc_scr[rows, :]
            gmat = []
            for g in range(SSD_GROUPS):
                bm = bcx[:, g * SSD_STATE:(g + 1) * SSD_STATE]
                cm = bcx[:, 2 * SSD_STATE + g * SSD_STATE:2 * SSD_STATE + (g + 1) * SSD_STATE]
                gmat.append((bm.T, cm, _bdot_nt(cm, bm)))
            yield
            for pr in range(SSD_PAIRS):
                bmt, cm, gm = gmat[pr // (SSD_PAIRS // SSD_GROUPS)]
                xpair = xs_scr[rows, pr * SSD_PAIR:(pr + 1) * SSD_PAIR]
                sc, bt, es, dec, xh = [], [], [], [], []
                for half in range(2):
                    ln = d * SSD_HEADS + 2 * pr + half
                    colb = jnp.broadcast_to(ac[:, ln:ln + 1], (SSD_CHUNK, SSD_CHUNK))
                    row = at[ln:ln + 1, :]
                    dtr = dt_row[ln:ln + 1, :]
                    seg = jnp.where(mask, jnp.exp(jnp.where(mask, colb - row, 0.0)), 0.0)
                    a_end = row[:, end:end + 1]
                    sc.append(gm * seg * dtr)
                    bt.append(bmt * (jnp.exp(a_end - row) * dtr))
                    es.append(jnp.exp(colb))
                    dec.append(jnp.exp(a_end))
                    xh.append(jnp.where(lo_half if half == 0 else jnp.logical_not(lo_half), xpair, 0.0))
                xst = jnp.concatenate(xh, axis=0)
                hs = h_scr[d, pr]
                y = _bdot(jnp.concatenate(sc, axis=1), xst)
                y = y + _bdot(cm, hs) * jnp.where(lo_half, es[0], es[1])
                y_ref[rows, pr * SSD_PAIR:(pr + 1) * SSD_PAIR] += y
                h_scr[d, pr] = (hs * jnp.where(lo_half, dec[0], dec[1])
                                + _bdot(jnp.concatenate(bt, axis=1), xst))
                yield

    def chunk_step(k, carry):
        _round_robin([chunk_problem((d,), k) for d in range(N_DIR)])
        return carry

    lax.fori_loop(0, nck, chunk_step, 0)
    hf_ref[...] = h_scr[...]


def _ssd(proj, small_t, row0, nseq, seq, width, conv_w, conv_b, dt_bias, a_log, dvec, h0):
    blk0 = row0 // seq
    nck = seq // SSD_CHUNK
    kern = functools.partial(_ssd_kernel, seq=seq, width=width)
    full = lambda *shape: pl.BlockSpec(shape, lambda b: (0,) * len(shape))
    dtb_r = dt_bias.reshape(1, 16)
    dtb_c = dt_bias.reshape(16, 1)
    al_r = a_log.reshape(1, 16)
    al_c = a_log.reshape(16, 1)
    st_spec = pl.BlockSpec((None, N_DIR, SSD_PAIRS, SSD_STATE, SSD_PAIR), lambda b: (b, 0, 0, 0, 0))
    return pl.pallas_call(
        kern,
        grid=(nseq,),
        in_specs=[
            pl.BlockSpec((seq, SSD_CONV_DIM), lambda b: (blk0 + b, COL_XBC // SSD_CONV_DIM)),
            pl.BlockSpec((seq, SMALL_W), lambda b: (blk0 + b, COL_SMALL // SMALL_W)),
            pl.BlockSpec((32, seq), lambda b: (0, blk0 + b)),
            full(3, SSD_CONV_DIM), full(1, SSD_CONV_DIM), full(1, 16), full(16, 1), full(1, 16), full(16, 1),
            full(1, SSD_WIDTH), st_spec,
        ],
        out_specs=[pl.BlockSpec((seq, SSD_WIDTH), lambda b: (b, 0)), st_spec],
        out_shape=[jax.ShapeDtypeStruct((nseq * seq, SSD_WIDTH), F32),
                   jax.ShapeDtypeStruct((nseq, N_DIR, SSD_PAIRS, SSD_STATE, SSD_PAIR), F32)],
        scratch_shapes=[
            pltpu.VMEM((seq, SSD_WIDTH), F32), pltpu.VMEM((seq, 4 * SSD_STATE), F32),
            pltpu.VMEM((seq, 16), F32), pltpu.VMEM((nck, 16, SSD_CHUNK), F32),
            pltpu.VMEM((nck, 16, SSD_CHUNK), F32),
            pltpu.VMEM((N_DIR, SSD_PAIRS, SSD_STATE, SSD_PAIR), F32),
        ],
        compiler_params=_cparams("parallel"),
        name="ssd_seq%d" % seq,
    )(proj, proj, small_t, conv_w, conv_b.reshape(1, -1), dtb_r, dtb_c, al_r, al_c, dvec, h0)


def _ssd_state_to_pairs(h):
    lead = h.shape[:-3]
    t = h.reshape(lead + (SSD_PAIRS, 2, SSD_HEADDIM, SSD_STATE))
    t = jnp.moveaxis(t, -1, -3)
    return t.reshape(lead + (SSD_PAIRS, SSD_STATE, SSD_PAIR))


def _ssd_state_from_pairs(hp):
    lead = hp.shape[:-3]
    t = hp.reshape(lead + (SSD_PAIRS, SSD_STATE, 2, SSD_HEADDIM))
    t = jnp.moveaxis(t, -3, -1)
    return t.reshape(lead + (SSD_HEADS, SSD_HEADDIM, SSD_STATE))


DN_ST = DN_HEADS * DN_CHUNK
DN_PAR = 2


def _dn_kernel(qkv_ref, sm_ref, smt_ref, cw_ref, dtb_r_ref, alog_r_ref, dtb_c_ref, alog_c_ref, s0_ref,
               o_ref, sf_ref, q_scr, k_scr, v_scr, b_scr, g_scr, s_scr, u_scr, wq_scr, a_scr, kd_scr, gl_scr,
               grow_scr, *, seq, width):
    nck = seq // DN_CHUNK
    g_rows = -jnp.exp(alog_c_ref[...]) * _softplus(smt_ref[24:32, :] + dtb_c_ref[...])
    for ck in range(nck):
        grow_scr[ck] = g_rows[:, ck * DN_CHUNK:(ck + 1) * DN_CHUNK]

    def pre(bi, carry):
        r0 = pl.multiple_of(bi * PRE_ROWS, PRE_ROWS)
        rows = pl.ds(r0, PRE_ROWS)
        xc = _silu(_conv_block(qkv_ref, cw_ref, r0, seq, width))
        for h in range(DN_HEADS):
            q = xc[:, h * DN_DK:(h + 1) * DN_DK]
            k = xc[:, DN_QK + h * DN_DK:DN_QK + (h + 1) * DN_DK]
            q_scr[rows, h * DN_DK:(h + 1) * DN_DK] = (
                q * lax.rsqrt(jnp.sum(q * q, axis=-1, keepdims=True) + EPS) * (DN_DK ** -0.5))
            k_scr[rows, h * DN_DK:(h + 1) * DN_DK] = (
                k * lax.rsqrt(jnp.sum(k * k, axis=-1, keepdims=True) + EPS))
        v_scr[rows, :] = xc[:, 2 * DN_QK:]
        b_scr[rows, :] = _sigmoid(sm_ref[rows, 16:24])
        g_scr[rows, :] = -jnp.exp(alog_r_ref[...]) * _softplus(sm_ref[rows, 24:32] + dtb_r_ref[...])
        o_ref[rows, :] = jnp.zeros((PRE_ROWS, DN_V), F32)
        return carry

    lax.fori_loop(0, seq // PRE_ROWS, pre, 0)
    s_scr[...] = s0_ref[...]

    r = lax.broadcasted_iota(jnp.int32, (DN_ST, DN_ST), 0)
    c = lax.broadcasted_iota(jnp.int32, (DN_ST, DN_ST), 1)
    same = (r // DN_CHUNK) == (c // DN_CHUNK)
    eye = (r == c).astype(F32)
    tril64 = _tri(DN_CHUNK, True).astype(BF16)
    triu64 = _tri(DN_CHUNK, False).astype(BF16)
    tj = lax.broadcasted_iota(jnp.int32, (DN_CHUNK, DN_ST), 0)
    ti = lax.broadcasted_iota(jnp.int32, (DN_CHUNK, DN_ST), 1) % DN_CHUNK
    cum_f = (tj <= ti).astype(BF16)
    cum_b = (tj >= ti).astype(BF16)

    def chunk_problem(d, ci):
        r0 = pl.multiple_of(ci * DN_CHUNK, DN_CHUNK)
        rows = pl.ds(r0, DN_CHUNK)
        incl = jnp.logical_and(same, (r >= c) if d == 0 else (r <= c))
        strict = jnp.logical_and(same, (r > c) if d == 0 else (r < c))
        gc_c = _dot_exact_lhs(tril64 if d == 0 else triu64, g_scr[rows, :])
        gc_t = _dot_exact_rhs(grow_scr[ci], cum_f if d == 0 else cum_b)
        gc_r = jnp.concatenate(
            [jnp.broadcast_to(gc_t[d * DN_HEADS + h:d * DN_HEADS + h + 1, :], (DN_CHUNK, DN_ST))
             for h in range(DN_HEADS)], axis=0)
        beta = b_scr[rows, :]
        end_row = DN_CHUNK - 1 if d == 0 else 0
        k_st, q_st, v_st, bt_st, gc_st, gl_st = [], [], [], [], [], []
        for h in range(DN_HEADS):
            ln = d * DN_HEADS + h
            k_st.append(k_scr[rows, h * DN_DK:(h + 1) * DN_DK])
            q_st.append(q_scr[rows, h * DN_DK:(h + 1) * DN_DK])
            v_st.append(v_scr[rows, h * DN_DV:(h + 1) * DN_DV])
            bt_st.append(beta[:, ln:ln + 1])
            col = gc_c[:, ln:ln + 1]
            gc_st.append(col)
            gl_st.append(col[end_row:end_row + 1, :])
        kst = jnp.concatenate(k_st, axis=0)
        qst = jnp.concatenate(q_st, axis=0)
        vst = jnp.concatenate(v_st, axis=0)
        bst = jnp.concatenate(bt_st, axis=0)
        gst = jnp.concatenate(gc_st, axis=0)
        decay = jnp.where(incl, jnp.exp(jnp.where(incl, gst - gc_r, 0.0)), 0.0)
        kb = kst * bst
        kstb = kst.astype(BF16)
        m = jnp.where(strict, _bdot_nt(kb, kstb) * decay, 0.0)
        attn = jnp.where(incl, _bdot_nt(qst, kstb) * decay, 0.0)
        a_scr[d, ci] = attn.astype(BF16)
        yield
        t = eye - m
        pb = m.astype(BF16)
        p = jnp.dot(pb, pb, preferred_element_type=F32)
        yield
        for lvl in range(5):
            pb = p.astype(BF16)
            t = t + jnp.dot(t.astype(BF16), pb, preferred_element_type=F32)
            if lvl < 4:
                p = jnp.dot(pb, pb, preferred_element_type=F32)
            yield
        tb = t.astype(BF16)
        rhs = jnp.concatenate([vst * bst, kb * jnp.exp(gst)], axis=1)
        x0 = jnp.dot(tb, rhs.astype(BF16), preferred_element_type=F32)
        yield
        res = rhs - x0 - _dot3(m, x0)
        yield
        uw = x0 + jnp.dot(tb, res.astype(BF16), preferred_element_type=F32)
        yield
        qg = qst * jnp.exp(gst)
        u_scr[d, ci] = uw[:, :DN_DV]
        kdec = []
        for h in range(DN_HEADS):
            hs = slice(h * DN_CHUNK, (h + 1) * DN_CHUNK)
            wq_scr[d, ci, h] = jnp.concatenate([uw[hs, DN_DV:], qg[hs, :]], axis=0).astype(BF16)
            gl = gl_st[h]
            kdec.append(k_st[h] * jnp.exp(gl - gc_st[h]))
            gl_scr[d, ci, h:h + 1, :] = jnp.broadcast_to(jnp.exp(gl), (1, DN_DV))
        kd_scr[d, ci] = jnp.concatenate(kdec, axis=0).astype(BF16)

    def chunk_step(kk, carry):
        _round_robin([chunk_problem(d, kk * DN_PAR + j) for j in range(DN_PAR) for d in range(N_DIR)])
        return carry

    lax.fori_loop(0, nck // DN_PAR, chunk_step, 0)

    def state_problem(d, ci):
        rows = pl.ds(pl.multiple_of(ci * DN_CHUNK, DN_CHUNK), DN_CHUNK)
        s_old, vnew, qs_all = [], [], []
        for h in range(DN_HEADS):
            hs = slice(h * DN_CHUNK, (h + 1) * DN_CHUNK)
            s_h = s_scr[d, h]
            ws = jnp.dot(wq_scr[d, ci, h], s_h.astype(BF16), preferred_element_type=F32)
            s_old.append(s_h)
            vnew.append((u_scr[d, ci, hs, :] - ws[:DN_CHUNK]).astype(BF16))
            qs_all.append(ws[DN_CHUNK:])
        yield
        o_st = jnp.concatenate(qs_all, axis=0) + jnp.dot(
            a_scr[d, ci], jnp.concatenate(vnew, axis=0), preferred_element_type=F32)
        for h in range(DN_HEADS):
            hs = slice(h * DN_CHUNK, (h + 1) * DN_CHUNK)
            s_scr[d, h] = s_old[h] * gl_scr[d, ci, h:h + 1, :] + lax.dot_general(
                kd_scr[d, ci, hs, :], vnew[h], (((0,), (0,)), ((), ())), preferred_element_type=F32)
        yield
        for h in range(DN_HEADS):
            hs = slice(h * DN_CHUNK, (h + 1) * DN_CHUNK)
            o_ref[rows, h * DN_DV:(h + 1) * DN_DV] += o_st[hs, :]

    def state_step(kk, carry):
        _round_robin([state_problem(0, kk), state_problem(1, nck - 1 - kk)])
        return carry

    lax.fori_loop(0, nck, state_step, 0)
    sf_ref[...] = s_scr[...]


def _dn(proj, small_t, row0, nseq, seq, width, conv_w, dt_bias, a_log, s0):
    blk0 = row0 // seq
    nck = seq // DN_CHUNK
    kern = functools.partial(_dn_kernel, seq=seq, width=width)
    full = lambda *shape: pl.BlockSpec(shape, lambda b: (0,) * len(shape))
    dtb_r = dt_bias.reshape(1, 8)
    al_r = a_log.reshape(1, 8)
    dtb_c = dt_bias.reshape(8, 1)
    al_c = a_log.reshape(8, 1)
    return pl.pallas_call(
        kern,
        grid=(nseq,),
        in_specs=[
            pl.BlockSpec((seq, DN_CONV_DIM), lambda b: (blk0 + b, COL_QKV // DN_CONV_DIM)),
            pl.BlockSpec((seq, SMALL_W), lambda b: (blk0 + b, COL_SMALL // SMALL_W)),
            pl.BlockSpec((32, seq), lambda b: (0, blk0 + b)),
            full(3, DN_CONV_DIM), full(1, 8), full(1, 8), full(8, 1), full(8, 1),
            pl.BlockSpec((None, N_DIR, DN_HEADS, DN_DK, DN_DV), lambda b: (b, 0, 0, 0, 0)),
        ],
        out_specs=[
            pl.BlockSpec((seq, DN_V), lambda b: (b, 0)),
            pl.BlockSpec((None, N_DIR, DN_HEADS, DN_DK, DN_DV), lambda b: (b, 0, 0, 0, 0)),
        ],
        out_shape=[jax.ShapeDtypeStruct((nseq * seq, DN_V), F32),
                   jax.ShapeDtypeStruct((nseq, N_DIR, DN_HEADS, DN_DK, DN_DV), F32)],
        scratch_shapes=[
            pltpu.VMEM((seq, DN_QK), F32), pltpu.VMEM((seq, DN_QK), F32), pltpu.VMEM((seq, DN_V), F32),
            pltpu.VMEM((seq, 8), F32), pltpu.VMEM((seq, 8), F32),
            pltpu.VMEM((N_DIR, DN_HEADS, DN_DK, DN_DV), F32),
            pltpu.VMEM((N_DIR, nck, DN_ST, DN_DV), F32),
            pltpu.VMEM((N_DIR, nck, DN_HEADS, 2 * DN_CHUNK, DN_DK), BF16),
            pltpu.VMEM((N_DIR, nck, DN_ST, DN_ST), BF16),
            pltpu.VMEM((N_DIR, nck, DN_ST, DN_DK), BF16),
            pltpu.VMEM((N_DIR, nck, 8, DN_DV), F32),
            pltpu.VMEM((nck, 8, DN_CHUNK), F32),
        ],
        compiler_params=_cparams("parallel"),
        name="dn_seq%d" % seq,
    )(proj, proj, small_t, conv_w, dtb_r, al_r, dtb_c, al_c, s0)


MRG_TM = 512


def _merge_kernel(x_ref, gt_ref, y5_ref, u5_ref, d5_ref, ysc_ref, ysl_ref, z_ref, odc_ref, odl_ref, dg_ref,
                  gr_ref, glu_b, sng_ref, swo_b, dng_ref, dwo_b, wo_b, o_ref):
    i = pl.program_id(0)
    grp = _row_group(i, MRG_TM)
    is_ctx = i < N_CTX // MRG_TM

    g5 = jax.nn.gelu(y5_ref[...] + d5_ref[...] * u5_ref[...]).astype(BF16)
    br_a = (jnp.dot(g5, glu_b[0], preferred_element_type=F32)
            * _sigmoid(jnp.dot(g5, glu_b[1], preferred_element_type=F32)))
    ys = jnp.where(is_ctx, ysc_ref[...], ysl_ref[...]) * _silu(z_ref[...])
    ys = ys * lax.rsqrt(jnp.mean(ys * ys, axis=-1, keepdims=True) + EPS) * sng_ref[...]
    br_b = jnp.dot(ys.astype(BF16), swo_b[...], preferred_element_type=F32)
    od = jnp.where(is_ctx, odc_ref[...], odl_ref[...])
    parts = []
    for h in range(DN_HEADS):
        oh = od[:, h * DN_DV:(h + 1) * DN_DV]
        parts.append(oh * lax.rsqrt(jnp.mean(oh * oh, axis=-1, keepdims=True) + EPS) * dng_ref[...])
    on = jnp.concatenate(parts, axis=1) * _silu(dg_ref[...])
    br_c = jnp.dot(on.astype(BF16), dwo_b[...], preferred_element_type=F32)
    merged = (_sigmoid(gr_ref[:, 0:D_MODEL]) * br_a
              + _sigmoid(gr_ref[:, D_MODEL:2 * D_MODEL]) * br_b
              + _sigmoid(gr_ref[:, 2 * D_MODEL:3 * D_MODEL]) * br_c)
    out = jnp.dot(merged.astype(BF16), wo_b[...], preferred_element_type=F32)
    o_ref[...] = x_ref[...] + gt_ref[pl.ds(grp, 1), :] * out


def _merge(x, mod, proj, y5, s5_d_l, ys_c, ys_l, od_c, od_l, s5_glu_l, ssd_norm_g_l, ssd_w_out_l, dn_norm_g_l,
           dn_w_out_l, w_out_l):
    nctx = N_CTX // MRG_TM
    rowblk = lambda w, col: pl.BlockSpec((MRG_TM, w), lambda i: (i, col // w))
    ctxblk = lambda w: pl.BlockSpec((MRG_TM, w), lambda i: (jnp.minimum(i, nctx - 1), 0))
    latblk = lambda w: pl.BlockSpec((MRG_TM, w), lambda i: (jnp.maximum(i - nctx, 0), 0))
    full = lambda *shape: pl.BlockSpec(shape, lambda i: (0,) * len(shape), pipeline_mode=pl.Buffered(1))
    return pl.pallas_call(
        _merge_kernel,
        grid=(N_ROWS // MRG_TM,),
        in_specs=[
            rowblk(D_MODEL, 0),
            pl.BlockSpec((8, D_MODEL), lambda i: (0, 5)),
            rowblk(S5_WIDTH, 0), rowblk(S5_WIDTH, COL_U), full(1, S5_WIDTH),
            ctxblk(SSD_WIDTH), latblk(SSD_WIDTH), rowblk(SSD_WIDTH, COL_Z),
            ctxblk(DN_V), latblk(DN_V), rowblk(DN_V, COL_DNG), rowblk(3 * D_MODEL, COL_GATES),
            full(2, S5_WIDTH, D_MODEL), full(1, SSD_WIDTH), full(SSD_WIDTH, D_MODEL),
            full(1, DN_DV), full(DN_V, D_MODEL), full(D_MODEL, D_MODEL),
        ],
        out_specs=rowblk(D_MODEL, 0),
        out_shape=jax.ShapeDtypeStruct((N_ROWS, D_MODEL), F32),
        compiler_params=_cparams("parallel"),
        name="merge",
    )(x, mod, y5, proj, s5_d_l.reshape(1, -1), ys_c, ys_l, proj, od_c, od_l, proj, proj, s5_glu_l.astype(BF16),
      ssd_norm_g_l.reshape(1, -1), ssd_w_out_l.astype(BF16), dn_norm_g_l.reshape(1, -1),
      dn_w_out_l.astype(BF16), w_out_l.astype(BF16))


FIN_TM = 1024


def _final_norm_kernel(x_ref, g_ref, oc_ref, ol_ref):
    i = pl.program_id(0)
    x = x_ref[...]
    y = x * lax.rsqrt(jnp.mean(x * x, axis=-1, keepdims=True) + EPS) * g_ref[...]

    @pl.when(i < N_CTX // FIN_TM)
    def _():
        oc_ref[...] = y

    @pl.when(i >= N_CTX // FIN_TM)
    def _():
        ol_ref[...] = y


def _final_norm(x, g):
    nctx = N_CTX // FIN_TM
    return pl.pallas_call(
        _final_norm_kernel,
        grid=(N_ROWS // FIN_TM,),
        in_specs=[pl.BlockSpec((FIN_TM, D_MODEL), lambda i: (i, 0)), pl.BlockSpec((1, D_MODEL), lambda i: (0, 0))],
        out_specs=[pl.BlockSpec((FIN_TM, D_MODEL), lambda i: (jnp.minimum(i, nctx - 1), 0)),
                   pl.BlockSpec((FIN_TM, D_MODEL), lambda i: (jnp.maximum(i - nctx, 0), 0))],
        out_shape=[jax.ShapeDtypeStruct((N_CTX, D_MODEL), F32), jax.ShapeDtypeStruct((N_LAT, D_MODEL), F32)],
        compiler_params=_cparams("arbitrary"),
        name="final_norm",
    )(x, g.reshape(1, -1))


def kernel(x_prompt, x_sample, state_s5_re, state_s5_im, state_ssd, state_dn, c, c_ctx, ada_w, ada_b, norm_g, ffn_wi, ffn_wo, w_in, s5_lam_re, s5_lam_im, s5_log_dt, s5_b_re, s5_b_im, s5_c_re, s5_c_im, s5_d, s5_glu, ssd_conv_w, ssd_conv_b, ssd_dt_bias, ssd_a_log, ssd_d, ssd_norm_g, ssd_w_out, dn_conv_w, dn_dt_bias, dn_a_log, dn_norm_g, dn_w_out, w_out, final_norm_g):
    x = jnp.concatenate([x_prompt.reshape(N_CTX, D_MODEL), x_sample.reshape(N_LAT, D_MODEL)], axis=0)
    cond8 = jnp.concatenate([c_ctx[None, :], c, jnp.zeros((8 - 1 - DEC_BATCH, D_MODEL), F32)], axis=0)
    mods = _ada_mods(cond8, ada_w, ada_b)

    mmat, ptab, qtab, atab = _s5_prep(s5_lam_re, s5_lam_im, s5_log_dt, s5_b_re, s5_b_im, s5_c_re, s5_c_im)
    p_re = jnp.concatenate([ptab[:, :, 0], ptab[:, :, 2]], axis=-1)
    p_im = jnp.concatenate([ptab[:, :, 1], ptab[:, :, 3]], axis=-1)
    q_re = jnp.concatenate([qtab[:, :, 0], qtab[:, :, 2]], axis=-1)
    q_im = jnp.concatenate([qtab[:, :, 1], qtab[:, :, 3]], axis=-1)
    a_re = jnp.concatenate([atab[:, :, 0], atab[:, :, 2]], axis=-1)[:, :, None, :]
    a_im = jnp.concatenate([atab[:, :, 1], atab[:, :, 3]], axis=-1)[:, :, None, :]
    s5_perm = _s5_perm()
    ssd_d_rows = jnp.repeat(ssd_d, SSD_HEADDIM, axis=1).reshape(DEPTH, 1, SSD_WIDTH)

    def s5_h0(state):
        return state.transpose(1, 3, 0, 2, 4).reshape(DEPTH, S5_GROUPS, DEC_BATCH, 2 * S5_STATE)

    h0_re = s5_h0(state_s5_re)
    h0_im = s5_h0(state_s5_im)
    ssd_h0_lat = _ssd_state_to_pairs(jnp.swapaxes(state_ssd, 0, 1))
    ssd_h0_ctx = jnp.zeros((BATCH, N_DIR, SSD_PAIRS, SSD_STATE, SSD_PAIR), F32)
    dn_s0_lat = jnp.swapaxes(state_dn, 0, 1)
    dn_s0_ctx = jnp.zeros((BATCH, N_DIR, DN_HEADS, DN_DK, DN_DV), F32)

    new_s5_re, new_s5_im, new_ssd, new_dn = [], [], [], []
    for l in range(DEPTH):
        mod = mods[l]
        x = _ffn(x, mod, norm_g[l, 0:1], ffn_wi, ffn_wo, l, 0)
        proj, small_t = _inproj(x, mod, norm_g[l, 1:2], _repack_w_in(w_in[l]))

        y5, f_re, f_im = _s5_scan(proj, s5_perm, mmat[l], p_re[l], p_im[l], q_re[l], q_im[l], a_re[l], a_im[l],
                                  h0_re[l], h0_im[l])
        new_s5_re.append(f_re.reshape(S5_GROUPS, BATCH, N_DIR, S5_STATE).transpose(1, 2, 0, 3))
        new_s5_im.append(f_im.reshape(S5_GROUPS, BATCH, N_DIR, S5_STATE).transpose(1, 2, 0, 3))

        ys_c, hs_c = _ssd(proj, small_t, 0, BATCH, SEQ, SEQ, ssd_conv_w[l], ssd_conv_b[l], ssd_dt_bias[l],
                          ssd_a_log[l], ssd_d_rows[l], ssd_h0_ctx)
        ys_l, _ = _ssd(proj, small_t, N_CTX, DEC_BATCH, DEC_SEQ, GRID_W, ssd_conv_w[l], ssd_conv_b[l],
                       ssd_dt_bias[l], ssd_a_log[l], ssd_d_rows[l], ssd_h0_lat[l])
        new_ssd.append(_ssd_state_from_pairs(hs_c))
        od_c, sd_c = _dn(proj, small_t, 0, BATCH, SEQ, SEQ, dn_conv_w[l], dn_dt_bias[l], dn_a_log[l], dn_s0_ctx)
        od_l, _ = _dn(proj, small_t, N_CTX, DEC_BATCH, DEC_SEQ, GRID_W, dn_conv_w[l], dn_dt_bias[l],
                      dn_a_log[l], dn_s0_lat[l])
        new_dn.append(sd_c)

        x = _merge(x, mod, proj, y5, s5_d[l], ys_c, ys_l, od_c, od_l, s5_glu[l], ssd_norm_g[l], ssd_w_out[l],
                   dn_norm_g[l], dn_w_out[l], w_out[l])
        x = _ffn(x, mod, norm_g[l, 2:3], ffn_wi, ffn_wo, l, 1)

    y_ctx, y_lat = _final_norm(x, final_norm_g)
    y_prompt = y_ctx.reshape(BATCH, SEQ, D_MODEL)
    y_sample = y_lat.reshape(DEC_BATCH, DEC_SEQ, D_MODEL)
    return (y_prompt, y_sample, jnp.stack(new_s5_re, axis=1), jnp.stack(new_s5_im, axis=1),
            jnp.stack(new_ssd, axis=1), jnp.stack(new_dn, axis=1))
```

```python
import functools

import jax
import jax.numpy as jnp
import numpy as np
from jax import lax
from jax.experimental import pallas as pl
from jax.experimental.pallas import tpu as pltpu

F32 = jnp.float32
BF16 = jnp.bfloat16

D_MODEL = 1024
BATCH = 16
SEQ = 256
DEPTH = 4
DEC_BATCH = 2
DEC_SEQ = 1024
GRID_W = 64
N_DIR = 2
N_ADA = 9
D_FF = 2816
EPS = 1e-6

S5_WIDTH = 512
S5_GROUP = 16
S5_GROUPS = 32
S5_STATE = 64
S5_CHUNK = 16

SSD_WIDTH = 512
SSD_HEADDIM = 64
SSD_HEADS = 8
SSD_GROUPS = 2
SSD_STATE = 64
SSD_CHUNK = 128
SSD_CONV_DIM = 768

DN_HEADS = 4
DN_DK = 128
DN_DV = 128
DN_QK = 512
DN_V = 512
DN_CHUNK = 64
DN_CONV_DIM = 1536

IN_SEGMENTS = (512, 512, 768, 16, 1536, 8, 8, 512, 3072)
IN_SPLITS = tuple(int(s) for s in np.cumsum(IN_SEGMENTS)[:-1])

N_CTX = BATCH * SEQ
N_LAT = DEC_BATCH * DEC_SEQ
N_ROWS = N_CTX + N_LAT

COL_GATES = 0
COL_QKV = 3072
COL_U = 4608
COL_Z = 5120
COL_DNG = 5632
COL_XBC = 6144
COL_SMALL = 6912
PROJ_W = 7040
SMALL_W = 128

VMEM_LIMIT = 56 * 1024 * 1024


def _cparams(*sem):
    return pltpu.CompilerParams(dimension_semantics=sem, vmem_limit_bytes=VMEM_LIMIT)


def _sigmoid(x):
    return jax.nn.sigmoid(x)


def _silu(x):
    return x * jax.nn.sigmoid(x)


def _softplus(x):
    return jnp.maximum(x, 0.0) + jnp.log(1.0 + jnp.exp(-jnp.abs(x)))


def _bdot(a, b):
    return jnp.dot(a.astype(BF16), b.astype(BF16), preferred_element_type=F32)


def _bdot_nt(a, b):
    return lax.dot_general(a.astype(BF16), b.astype(BF16), (((1,), (1,)), ((), ())),
                           preferred_element_type=F32)


def _bdot_tn(a, b):
    return lax.dot_general(a.astype(BF16), b.astype(BF16), (((0,), (0,)), ((), ())),
                           preferred_element_type=F32)


def _split3(a):
    hi = a.astype(BF16)
    r = a - hi.astype(F32)
    mid = r.astype(BF16)
    lo = (r - mid.astype(F32)).astype(BF16)
    return hi, mid, lo


def _dot3(a, b):
    ah = a.astype(BF16)
    al = (a - ah.astype(F32)).astype(BF16)
    bh = b.astype(BF16)
    bl = (b - bh.astype(F32)).astype(BF16)
    out = jnp.dot(ah, bh, preferred_element_type=F32)
    out = out + jnp.dot(ah, bl, preferred_element_type=F32)
    out = out + jnp.dot(al, bh, preferred_element_type=F32)
    return out


def _dot_exact_lhs(t_bf16, x):
    hi, mid, lo = _split3(x)
    out = jnp.dot(t_bf16, hi, preferred_element_type=F32)
    out = out + jnp.dot(t_bf16, mid, preferred_element_type=F32)
    out = out + jnp.dot(t_bf16, lo, preferred_element_type=F32)
    return out


def _dot_exact_rhs(x, t_bf16):
    hi, mid, lo = _split3(x)
    out = jnp.dot(hi, t_bf16, preferred_element_type=F32)
    out = out + jnp.dot(mid, t_bf16, preferred_element_type=F32)
    out = out + jnp.dot(lo, t_bf16, preferred_element_type=F32)
    return out


def _norm_mod(x, g, sc, sh):
    ms = jnp.mean(x * x, axis=-1, keepdims=True)
    y = x * lax.rsqrt(ms + EPS) * g
    return y * (1.0 + sc) + sh


def _row_group(i, tm):
    nctx = N_CTX // tm
    per = DEC_SEQ // tm
    return jnp.where(i < nctx, 0, 1 + jnp.maximum(i - nctx, 0) // per)


ADA_TN = 1152


def _ada_kernel(c_ref, w_ref, b_ref, o_ref):
    c = c_ref[...]
    o_ref[...] = _bdot(_silu(c), w_ref[...]) + b_ref[...]


def _ada_mods(cond8, ada_w, ada_b):
    nj = (N_ADA * D_MODEL) // ADA_TN
    return pl.pallas_call(
        _ada_kernel,
        grid=(DEPTH, nj),
        in_specs=[
            pl.BlockSpec((8, D_MODEL), lambda l, j: (0, 0)),
            pl.BlockSpec((None, D_MODEL, ADA_TN), lambda l, j: (l, 0, j)),
            pl.BlockSpec((None, 1, ADA_TN), lambda l, j: (l, 0, j)),
        ],
        out_specs=pl.BlockSpec((None, 8, ADA_TN), lambda l, j: (l, 0, j)),
        out_shape=jax.ShapeDtypeStruct((DEPTH, 8, N_ADA * D_MODEL), F32),
        compiler_params=_cparams("parallel", "parallel"),
        name="ada_mods",
    )(cond8, ada_w, ada_b.reshape(DEPTH, 1, N_ADA * D_MODEL))


FFN_TM = 2048
FFN_TF = 256
MOD_ROWS = 1024
FFN_SUB = FFN_TM // MOD_ROWS


def _ffn_kernel(x_ref, sh_ref, sc_ref, gt_ref, g_ref, wa_ref, wb_ref, wo_ref, o_ref, h_scr, acc_scr):
    i = pl.program_id(0)
    j = pl.program_id(1)

    @pl.when(j == 0)
    def _():
        for s in range(FFN_SUB):
            rows = slice(s * MOD_ROWS, (s + 1) * MOD_ROWS)
            grp = _row_group(i * FFN_SUB + s, MOD_ROWS)
            h = _norm_mod(x_ref[rows, :], g_ref[...], sc_ref[pl.ds(grp, 1), :], sh_ref[pl.ds(grp, 1), :])
            h_scr[rows, :] = h.astype(BF16)
        acc_scr[...] = jnp.zeros_like(acc_scr)

    h = h_scr[...]
    a = jnp.dot(h, wa_ref[...].astype(BF16), preferred_element_type=F32)
    b = jnp.dot(h, wb_ref[...].astype(BF16), preferred_element_type=F32)
    u = (_silu(a) * b).astype(BF16)
    acc_scr[...] += jnp.dot(u, wo_ref[...].astype(BF16), preferred_element_type=F32)

    @pl.when(j == pl.num_programs(1) - 1)
    def _():
        for s in range(FFN_SUB):
            rows = slice(s * MOD_ROWS, (s + 1) * MOD_ROWS)
            grp = _row_group(i * FFN_SUB + s, MOD_ROWS)
            o_ref[rows, :] = x_ref[rows, :] + (0.5 * gt_ref[pl.ds(grp, 1), :]) * acc_scr[rows, :]


def _ffn(x, mod, norm_g_row, ffn_wi, ffn_wo, layer, which):
    nf = D_FF // FFN_TF
    base = 0 if which == 0 else 6
    return pl.pallas_call(
        _ffn_kernel,
        grid=(N_ROWS // FFN_TM, nf),
        in_specs=[
            pl.BlockSpec((FFN_TM, D_MODEL), lambda i, j: (i, 0)),
            pl.BlockSpec((8, D_MODEL), lambda i, j: (0, base)),
            pl.BlockSpec((8, D_MODEL), lambda i, j: (0, base + 1)),
            pl.BlockSpec((8, D_MODEL), lambda i, j: (0, base + 2)),
            pl.BlockSpec((1, D_MODEL), lambda i, j: (0, 0)),
            pl.BlockSpec((None, None, D_MODEL, FFN_TF), lambda i, j: (layer, which, 0, j)),
            pl.BlockSpec((None, None, D_MODEL, FFN_TF), lambda i, j: (layer, which, 0, j + nf)),
            pl.BlockSpec((None, None, FFN_TF, D_MODEL), lambda i, j: (layer, which, j, 0)),
        ],
        out_specs=pl.BlockSpec((FFN_TM, D_MODEL), lambda i, j: (i, 0)),
        out_shape=jax.ShapeDtypeStruct((N_ROWS, D_MODEL), F32),
        scratch_shapes=[pltpu.VMEM((FFN_TM, D_MODEL), BF16), pltpu.VMEM((FFN_TM, D_MODEL), F32)],
        compiler_params=_cparams("parallel", "arbitrary"),
        name="ffn",
    )(x, mod, mod, mod, norm_g_row, ffn_wi, ffn_wi, ffn_wo)


INP_TM = 1024
INP_TN = 640


def _inproj_kernel(x_ref, sh_ref, sc_ref, g_ref, w_ref, o_ref, ot_ref, h_scr):
    i = pl.program_id(0)
    j = pl.program_id(1)
    grp = _row_group(i, INP_TM)

    @pl.when(j == 0)
    def _():
        h = _norm_mod(x_ref[...], g_ref[...], sc_ref[pl.ds(grp, 1), :], sh_ref[pl.ds(grp, 1), :])
        h_scr[...] = h.astype(BF16)

    res = jnp.dot(h_scr[...], w_ref[...], preferred_element_type=F32)
    o_ref[...] = res

    @pl.when(j == pl.num_programs(1) - 1)
    def _():
        ot_ref[...] = res[:, INP_TN - SMALL_W:].T


RPK_ROWS = 256
RPK_MOVES = ((COL_GATES, 3872, 3072), (COL_QKV, 1808, 1536), (COL_U, 0, 512), (COL_Z, 512, 512),
             (COL_DNG, 3360, 512), (COL_XBC, 1024, 768), (COL_SMALL, 1792, 16), (COL_SMALL + 16, 3344, 8),
             (COL_SMALL + 24, 3352, 8))


def _repack_kernel(w_ref, o_ref):
    for dst, src, n in RPK_MOVES:
        o_ref[:, dst:dst + n] = w_ref[:, src:src + n].astype(BF16)
    o_ref[:, COL_SMALL + 32:] = jnp.zeros((RPK_ROWS, SMALL_W - 32), BF16)


def _repack_w_in(w_in):
    in_w = w_in.shape[-1]
    return pl.pallas_call(
        _repack_kernel,
        grid=(DEPTH, D_MODEL // RPK_ROWS),
        in_specs=[pl.BlockSpec((None, RPK_ROWS, in_w), lambda l, i: (l, i, 0))],
        out_specs=pl.BlockSpec((None, RPK_ROWS, PROJ_W), lambda l, i: (l, i, 0)),
        out_shape=jax.ShapeDtypeStruct((DEPTH, D_MODEL, PROJ_W), BF16),
        compiler_params=_cparams("parallel", "parallel"),
        name="repack_w_in",
    )(w_in)


def _inproj(x, mod, norm_g_row, w_packed, layer):
    return pl.pallas_call(
        _inproj_kernel,
        grid=(N_ROWS // INP_TM, PROJ_W // INP_TN),
        in_specs=[
            pl.BlockSpec((INP_TM, D_MODEL), lambda i, j: (i, 0)),
            pl.BlockSpec((8, D_MODEL), lambda i, j: (0, 3)),
            pl.BlockSpec((8, D_MODEL), lambda i, j: (0, 4)),
            pl.BlockSpec((1, D_MODEL), lambda i, j: (0, 0)),
            pl.BlockSpec((None, D_MODEL, INP_TN), lambda i, j: (layer, 0, j)),
        ],
        out_specs=[pl.BlockSpec((INP_TM, INP_TN), lambda i, j: (i, j)),
                   pl.BlockSpec((SMALL_W, INP_TM), lambda i, j: (0, i))],
        out_shape=[jax.ShapeDtypeStruct((N_ROWS, PROJ_W), F32),
                   jax.ShapeDtypeStruct((SMALL_W, N_ROWS), F32)],
        scratch_shapes=[pltpu.VMEM((INP_TM, D_MODEL), BF16)],
        compiler_params=_cparams("parallel", "arbitrary"),
        name="inproj",
    )(x, mod, mod, norm_g_row, w_packed)


S5_ROW = S5_CHUNK * S5_GROUP


def _s5_prep_kernel(lam_re_ref, lam_im_ref, ldt_ref, btr_ref, bti_ref, cr_ref, ci_ref, ctr_ref, cti_ref,
                    m_ref, pre_ref, pim_ref, qre_ref, qim_ref, are_ref, aim_ref):
    tau = lax.broadcasted_iota(jnp.int32, (S5_CHUNK, 1), 0).astype(F32)
    lane = lax.broadcasted_iota(jnp.int32, (S5_GROUP, S5_ROW), 1)
    lane_t = lane[0:1, :] // S5_GROUP
    rep = (lax.broadcasted_iota(jnp.int32, (S5_GROUP, S5_ROW), 0) == lane % S5_GROUP).astype(BF16)
    mmat = jnp.zeros((S5_ROW, S5_ROW), F32)

    for d in range(N_DIR):
        lr = lam_re_ref[d:d + 1, :]
        li = lam_im_ref[d:d + 1, :]
        dt = jnp.exp(ldt_ref[d:d + 1, :])
        mag = jnp.exp(lr * dt)
        lb_re = mag * jnp.cos(li * dt)
        lb_im = mag * jnp.sin(li * dt)
        den = lr * lr + li * li
        cr = ((lb_re - 1.0) * lr + lb_im * li) / den
        ci = (lb_im * lr - (lb_re - 1.0) * li) / den
        bt_r = btr_ref[d]
        bt_i = bti_ref[d]
        bbt_r = cr * bt_r - ci * bt_i
        bbt_i = cr * bt_i + ci * bt_r
        c_r = cr_ref[d]
        c_i = ci_ref[d]

        def powtab(t):
            m = jnp.exp(t * (lr * dt))
            ang = t * (li * dt)
            return m * jnp.cos(ang), m * jnp.sin(ang)

        def outer(ar, ai, xr, xi):
            rr = ar[:, None, :] * xr[None, :, :] - ai[:, None, :] * xi[None, :, :]
            ii = ar[:, None, :] * xi[None, :, :] + ai[:, None, :] * xr[None, :, :]
            return rr.reshape(S5_ROW, S5_STATE), ii.reshape(S5_ROW, S5_STATE)

        t_in = (S5_CHUNK - 1) - tau if d == 0 else tau
        ar, ai = powtab(t_in)
        ba_r, ba_i = outer(ar, ai, bbt_r, bbt_i)
        lanes = slice(d * S5_STATE, (d + 1) * S5_STATE)
        pre_ref[:, lanes] = ba_r.astype(BF16)
        pim_ref[:, lanes] = ba_i.astype(BF16)
        kt = _dot_exact_rhs(_dot3(ba_r, ctr_ref[d]) - _dot3(ba_i, cti_ref[d]), rep)
        for s in range(S5_CHUNK):
            rows = s * S5_GROUP
            if d == 0:
                sh = kt if s == 0 else jnp.concatenate(
                    [kt[rows:, :], jnp.zeros((rows, S5_ROW), F32)], axis=0)
                mmat = mmat + jnp.where(lane_t == (S5_CHUNK - 1) - s, sh, 0.0)
            else:
                sh = kt if s == 0 else jnp.concatenate(
                    [jnp.zeros((rows, S5_ROW), F32), kt[:S5_ROW - rows, :]], axis=0)
                mmat = mmat + jnp.where(lane_t == s, sh, 0.0)
        t_out = tau + 1.0 if d == 0 else S5_CHUNK - tau
        ar, ai = powtab(t_out)
        qr, qi = outer(ar, ai, c_r, c_i)
        qre_ref[:, lanes] = qr.astype(BF16)
        qim_ref[:, lanes] = (-qi).astype(BF16)
        a16r, a16i = powtab(jnp.full((1, 1), float(S5_CHUNK), F32))
        are_ref[:, lanes] = a16r
        aim_ref[:, lanes] = a16i

    m_ref[...] = mmat.astype(BF16)


def _s5_prep(lam_re, lam_im, log_dt, b_re, b_im, c_re, c_im):
    tg = lambda t: jnp.swapaxes(t, 1, 2)
    lam_re_g = tg(lam_re)
    lam_im_g = tg(lam_im)
    ldt_g = tg(log_dt)[..., None]
    bt_r = jnp.swapaxes(tg(b_re), -1, -2)
    bt_i = jnp.swapaxes(tg(b_im), -1, -2)
    c_r = tg(c_re)
    c_i = tg(c_im)
    ct_r = jnp.swapaxes(c_r, -1, -2)
    ct_i = jnp.swapaxes(c_i, -1, -2)

    def spec(*tail):
        n = len(tail)
        return pl.BlockSpec((None, None) + tail, lambda l, g: (l, g) + (0,) * n)

    st = 2 * S5_STATE
    tab = jax.ShapeDtypeStruct((DEPTH, S5_GROUPS, S5_ROW, st), BF16)
    dec = jax.ShapeDtypeStruct((DEPTH, S5_GROUPS, 1, st), F32)
    return pl.pallas_call(
        _s5_prep_kernel,
        grid=(DEPTH, S5_GROUPS),
        in_specs=[spec(2, 64), spec(2, 64), spec(2, 1), spec(2, 16, 64), spec(2, 16, 64),
                  spec(2, 16, 64), spec(2, 16, 64), spec(2, 64, S5_GROUP), spec(2, 64, S5_GROUP)],
        out_specs=[spec(S5_ROW, S5_ROW), spec(S5_ROW, st), spec(S5_ROW, st), spec(S5_ROW, st), spec(S5_ROW, st),
                   spec(1, st), spec(1, st)],
        out_shape=[jax.ShapeDtypeStruct((DEPTH, S5_GROUPS, S5_ROW, S5_ROW), BF16), tab, tab, tab, tab, dec, dec],
        compiler_params=_cparams("parallel", "parallel"),
        name="s5_prep",
    )(lam_re_g, lam_im_g, ldt_g, bt_r, bt_i, c_r, c_i, ct_r, ct_i)


S5_CTX_CH = SEQ // S5_CHUNK
S5_LAT_CH = DEC_SEQ // S5_CHUNK
S5_CTX_ROWS = S5_CTX_CH * BATCH
S5_LAT_ROWS = S5_LAT_CH * DEC_BATCH
S5_ROWS = S5_CTX_ROWS + S5_LAT_ROWS
S5_GB = 128 // S5_GROUP
S5_PERM = S5_GB * 128


def _s5_perm():
    src = np.arange(S5_PERM)
    s, g, j = src // 128, (src % 128) // S5_GROUP, src % S5_GROUP
    p = np.zeros((S5_PERM, S5_PERM), np.float32)
    p[src, g * 128 + s * S5_GROUP + j] = 1.0
    return jnp.asarray(p, BF16)


def _s5_kernel(u_ref, perm_ref, m_ref, pre_ref, pim_ref, qre_ref, qim_ref, are_ref, aim_ref, h0r_ref, h0i_ref,
               y_ref, fr_ref, fi_ref, ug, sre, sim, hfr, hfi, hbr, hbi, ys):
    perm = perm_ref[...]
    half_w = S5_ROW // 2
    for half in range(2):
        x = jnp.concatenate([u_ref[pl.ds(half * 8 + s, S5_ROWS, stride=S5_CHUNK), :] for s in range(8)], axis=1)
        z = jnp.dot(x.astype(BF16), perm, preferred_element_type=F32)
        for g in range(S5_GB):
            ug[g, :, half * half_w:(half + 1) * half_w] = z[:, g * 128:(g + 1) * 128].astype(BF16)
    for g in range(S5_GB):
        sre[g] = jnp.dot(ug[g], pre_ref[g].astype(BF16), preferred_element_type=F32)
        sim[g] = jnp.dot(ug[g], pim_ref[g].astype(BF16), preferred_element_type=F32)
    fwd = lax.broadcasted_iota(jnp.int32, (1, 2 * S5_STATE), 1) < S5_STATE

    def scan(base, nchunk, nseq, init):
        def step(k, hs):
            rf = pl.ds(base + k, nseq, stride=nchunk)
            rb = pl.ds(base + nchunk - 1 - k, nseq, stride=nchunk)
            out = []
            for g in range(S5_GB):
                h_re, h_im = hs[2 * g], hs[2 * g + 1]
                hfr.at[g][rf, :] = h_re
                hfi.at[g][rf, :] = h_im
                hbr.at[g][rb, :] = h_re
                hbi.at[g][rb, :] = h_im
                s_r = jnp.where(fwd, sre.at[g][rf, :], sre.at[g][rb, :])
                s_i = jnp.where(fwd, sim.at[g][rf, :], sim.at[g][rb, :])
                ar = are_ref[g]
                ai = aim_ref[g]
                out.append(ar * h_re - ai * h_im + s_r)
                out.append(ar * h_im + ai * h_re + s_i)
            return tuple(out)

        return lax.fori_loop(0, nchunk, step, init)

    zero = jnp.zeros((BATCH, 2 * S5_STATE), F32)
    fin = scan(0, S5_CTX_CH, BATCH, (zero,) * (2 * S5_GB))
    lat0 = []
    for g in range(S5_GB):
        fr_ref[g] = fin[2 * g]
        fi_ref[g] = fin[2 * g + 1]
        lat0 += [h0r_ref[g], h0i_ref[g]]
    scan(S5_CTX_ROWS, S5_LAT_CH, DEC_BATCH, tuple(lat0))

    for g in range(S5_GB):
        h_re = jnp.where(fwd, hfr[g], hbr[g])
        h_im = jnp.where(fwd, hfi[g], hbi[g])
        y = jnp.dot(ug[g], m_ref[g].astype(BF16), preferred_element_type=F32)
        ys[g] = y + _bdot_nt(h_re, qre_ref[g]) + _bdot_nt(h_im, qim_ref[g])
    for half in range(2):
        w = jnp.concatenate([ys[g, :, half * half_w:(half + 1) * half_w] for g in range(S5_GB)], axis=1)
        zo = sum(lax.dot_general(piece, perm, (((1,), (1,)), ((), ())), preferred_element_type=F32)
                 for piece in _split3(w))
        for t in range(8):
            y_ref[pl.ds(half * 8 + t, S5_ROWS, stride=S5_CHUNK), :] = zo[:, t * 128:(t + 1) * 128]


def _s5_scan(proj, perm, tables, h0_re, h0_im, layer):
    def lspec(*tail):
        n = len(tail)
        return pl.BlockSpec((None, S5_GB) + tail, lambda t: (layer, t) + (0,) * n)

    def gspec(*tail):
        n = len(tail)
        return pl.BlockSpec((S5_GB,) + tail, lambda t: (t,) + (0,) * n)

    st = 2 * S5_STATE
    return pl.pallas_call(
        _s5_kernel,
        grid=(S5_GROUPS // S5_GB,),
        in_specs=[pl.BlockSpec((N_ROWS, 128), lambda t: (0, COL_U // 128 + t)),
                  pl.BlockSpec((S5_PERM, S5_PERM), lambda t: (0, 0)),
                  lspec(S5_ROW, S5_ROW), lspec(S5_ROW, st), lspec(S5_ROW, st), lspec(S5_ROW, st),
                  lspec(S5_ROW, st), lspec(1, st), lspec(1, st), lspec(DEC_BATCH, st), lspec(DEC_BATCH, st)],
        out_specs=[pl.BlockSpec((N_ROWS, 128), lambda t: (0, t)), gspec(BATCH, st), gspec(BATCH, st)],
        out_shape=[jax.ShapeDtypeStruct((N_ROWS, S5_WIDTH), F32),
                   jax.ShapeDtypeStruct((S5_GROUPS, BATCH, st), F32),
                   jax.ShapeDtypeStruct((S5_GROUPS, BATCH, st), F32)],
        scratch_shapes=([pltpu.VMEM((S5_GB, S5_ROWS, S5_ROW), BF16)]
                        + [pltpu.VMEM((S5_GB, S5_ROWS, st), F32) for _ in range(6)]
                        + [pltpu.VMEM((S5_GB, S5_ROWS, S5_ROW), F32)]),
        compiler_params=_cparams("parallel"),
        name="s5_scan",
    )(proj, perm, *tables, h0_re, h0_im)


PRE_ROWS = 128


def _conv_block(x_ref, w_ref, r0, seq, width):
    x = x_ref[pl.ds(r0, PRE_ROWS), :]
    prev = x_ref[pl.ds(jnp.maximum(r0 - 1, 0), 1), :]
    nxt = x_ref[pl.ds(jnp.minimum(r0 + PRE_ROWS, seq - 1), 1), :]
    rid = lax.broadcasted_iota(jnp.int32, (PRE_ROWS, 1), 0)
    pos = (r0 + rid) % width
    xm = jnp.where(rid == 0, prev, pltpu.roll(x, 1, 0))
    xm = jnp.where(pos == 0, 0.0, xm)
    xp = jnp.where(rid == PRE_ROWS - 1, nxt, pltpu.roll(x, PRE_ROWS - 1, 0))
    xp = jnp.where(pos == width - 1, 0.0, xp)
    return xm * w_ref[0:1, :] + x * w_ref[1:2, :] + xp * w_ref[2:3, :]


def _round_robin(problems):
    live = list(problems)
    while live:
        nxt = []
        for p in live:
            try:
                next(p)
                nxt.append(p)
            except StopIteration:
                pass
        live = nxt


def _tri(n, lower):
    r = lax.broadcasted_iota(jnp.int32, (n, n), 0)
    c = lax.broadcasted_iota(jnp.int32, (n, n), 1)
    return (r >= c) if lower else (r <= c)


SSD_PAIR = 2 * SSD_HEADDIM
SSD_PAIRS = SSD_HEADS // 2


def _ssd_kernel(xbc_ref, sm_ref, smt_ref, cw_ref, cb_ref, dtb_r_ref, dtb_c_ref, alog_r_ref, alog_c_ref,
                dvec_ref, h0_ref, y_ref, hf_ref, xs_scr, bc_scr, dac_scr, dar_scr, dtr_scr, h_scr,
                *, seq, width):
    nck = seq // SSD_CHUNK

    def pre(bi, carry):
        r0 = pl.multiple_of(bi * PRE_ROWS, PRE_ROWS)
        rows = pl.ds(r0, PRE_ROWS)
        xc = _silu(_conv_block(xbc_ref, cw_ref, r0, seq, width) + cb_ref[...])
        xs = xc[:, :SSD_WIDTH]
        xs_scr[rows, :] = xs
        bc_scr[rows, :] = xc[:, SSD_WIDTH:]
        y_ref[rows, :] = dvec_ref[...] * xs
        dt_c = _softplus(sm_ref[rows, 0:16] + dtb_r_ref[...])
        dac_scr[rows, :] = dt_c * (-jnp.exp(alog_r_ref[...]))
        return carry

    lax.fori_loop(0, seq // PRE_ROWS, pre, 0)
    dt_r = _softplus(smt_ref[0:16, :] + dtb_c_ref[...])
    da_r = dt_r * (-jnp.exp(alog_c_ref[...]))
    for ck in range(nck):
        dtr_scr[ck] = dt_r[:, ck * SSD_CHUNK:(ck + 1) * SSD_CHUNK]
        dar_scr[ck] = da_r[:, ck * SSD_CHUNK:(ck + 1) * SSD_CHUNK]
    h_scr[...] = h0_ref[...]

    tril = _tri(SSD_CHUNK, True)
    triu = _tri(SSD_CHUNK, False)
    tril_b = tril.astype(BF16)
    triu_b = triu.astype(BF16)
    lo_half = lax.broadcasted_iota(jnp.int32, (1, SSD_PAIR), 1) < SSD_HEADDIM

    def chunk_problem(dirs, k):
        for d in dirs:
            c = k if d == 0 else nck - 1 - k
            r0 = pl.multiple_of(c * SSD_CHUNK, SSD_CHUNK)
            rows = pl.ds(r0, SSD_CHUNK)
            mask = tril if d == 0 else triu
            ac = _dot_exact_lhs(tril_b if d == 0 else triu_b, dac_scr[rows, :])
            at = _dot_exact_rhs(dar_scr[c], triu_b if d == 0 else tril_b)
            dt_row = dtr_scr[c]
            end = SSD_CHUNK - 1 if d == 0 else 0
            bcx = bc_scr[rows, :]
            gmat = []
            for g in range(SSD_GROUPS):
                bm = bcx[:, g * SSD_STATE:(g + 1) * SSD_STATE]
                cm = bcx[:, 2 * SSD_STATE + g * SSD_STATE:2 * SSD_STATE + (g + 1) * SSD_STATE]
                gmat.append((bm.T, cm, _bdot_nt(cm, bm)))
            yield
            for pr in range(SSD_PAIRS):
                bmt, cm, gm = gmat[pr // (SSD_PAIRS // SSD_GROUPS)]
                xpair = xs_scr[rows, pr * SSD_PAIR:(pr + 1) * SSD_PAIR]
                sc, bt, es, dec, xh = [], [], [], [], []
                for half in range(2):
                    ln = d * SSD_HEADS + 2 * pr + half
                    colb = jnp.broadcast_to(ac[:, ln:ln + 1], (SSD_CHUNK, SSD_CHUNK))
                    row = at[ln:ln + 1, :]
                    dtr = dt_row[ln:ln + 1, :]
                    seg = jnp.where(mask, jnp.exp(jnp.where(mask, colb - row, 0.0)), 0.0)
                    a_end = row[:, end:end + 1]
                    sc.append(gm * seg * dtr)
                    bt.append(bmt * (jnp.exp(a_end - row) * dtr))
                    es.append(jnp.exp(colb))
                    dec.append(jnp.exp(a_end))
                    xh.append(jnp.where(lo_half if half == 0 else jnp.logical_not(lo_half), xpair, 0.0))
                xst = jnp.concatenate(xh, axis=0)
                hs = h_scr[d, pr]
                y = _bdot(jnp.concatenate(sc, axis=1), xst)
                y = y + _bdot(cm, hs) * jnp.where(lo_half, es[0], es[1])
                y_ref[rows, pr * SSD_PAIR:(pr + 1) * SSD_PAIR] += y
                h_scr[d, pr] = (hs * jnp.where(lo_half, dec[0], dec[1])
                                + _bdot(jnp.concatenate(bt, axis=1), xst))
                yield

    def chunk_step(k, carry):
        _round_robin([chunk_problem((d,), k) for d in range(N_DIR)])
        return carry

    lax.fori_loop(0, nck, chunk_step, 0)
    hf_ref[...] = h_scr[...]


def _ssd(proj, small_t, row0, nseq, seq, width, conv_w, conv_b, dt_bias, a_log, dvec, h0):
    blk0 = row0 // seq
    nck = seq // SSD_CHUNK
    kern = functools.partial(_ssd_kernel, seq=seq, width=width)
    full = lambda *shape: pl.BlockSpec(shape, lambda b: (0,) * len(shape))
    dtb_r = dt_bias.reshape(1, 16)
    dtb_c = dt_bias.reshape(16, 1)
    al_r = a_log.reshape(1, 16)
    al_c = a_log.reshape(16, 1)
    st_spec = pl.BlockSpec((None, N_DIR, SSD_PAIRS, SSD_STATE, SSD_PAIR), lambda b: (b, 0, 0, 0, 0))
    return pl.pallas_call(
        kern,
        grid=(nseq,),
        in_specs=[
            pl.BlockSpec((seq, SSD_CONV_DIM), lambda b: (blk0 + b, COL_XBC // SSD_CONV_DIM)),
            pl.BlockSpec((seq, SMALL_W), lambda b: (blk0 + b, COL_SMALL // SMALL_W)),
            pl.BlockSpec((32, seq), lambda b: (0, blk0 + b)),
            full(3, SSD_CONV_DIM), full(1, SSD_CONV_DIM), full(1, 16), full(16, 1), full(1, 16), full(16, 1),
            full(1, SSD_WIDTH), st_spec,
        ],
        out_specs=[pl.BlockSpec((seq, SSD_WIDTH), lambda b: (b, 0)), st_spec],
        out_shape=[jax.ShapeDtypeStruct((nseq * seq, SSD_WIDTH), F32),
                   jax.ShapeDtypeStruct((nseq, N_DIR, SSD_PAIRS, SSD_STATE, SSD_PAIR), F32)],
        scratch_shapes=[
            pltpu.VMEM((seq, SSD_WIDTH), F32), pltpu.VMEM((seq, 4 * SSD_STATE), F32),
            pltpu.VMEM((seq, 16), F32), pltpu.VMEM((nck, 16, SSD_CHUNK), F32),
            pltpu.VMEM((nck, 16, SSD_CHUNK), F32),
            pltpu.VMEM((N_DIR, SSD_PAIRS, SSD_STATE, SSD_PAIR), F32),
        ],
        compiler_params=_cparams("parallel"),
        name="ssd_seq%d" % seq,
    )(proj, proj, small_t, conv_w, conv_b.reshape(1, -1), dtb_r, dtb_c, al_r, al_c, dvec, h0)


def _ssd_state_to_pairs(h):
    lead = h.shape[:-3]
    t = h.reshape(lead + (SSD_PAIRS, 2, SSD_HEADDIM, SSD_STATE))
    t = jnp.moveaxis(t, -1, -3)
    return t.reshape(lead + (SSD_PAIRS, SSD_STATE, SSD_PAIR))


def _ssd_state_from_pairs(hp):
    lead = hp.shape[:-3]
    t = hp.reshape(lead + (SSD_PAIRS, SSD_STATE, 2, SSD_HEADDIM))
    t = jnp.moveaxis(t, -3, -1)
    return t.reshape(lead + (SSD_HEADS, SSD_HEADDIM, SSD_STATE))


DN_ST = DN_HEADS * DN_CHUNK
DN_PAR = 2


def _dn_kernel(qkv_ref, sm_ref, smt_ref, cw_ref, dtb_r_ref, alog_r_ref, dtb_c_ref, alog_c_ref, s0_ref,
               o_ref, sf_ref, q_scr, k_scr, v_scr, b_scr, g_scr, s_scr, u_scr, wq_scr, a_scr, kd_scr, gl_scr,
               grow_scr, *, seq, width):
    nck = seq // DN_CHUNK
    g_rows = -jnp.exp(alog_c_ref[...]) * _softplus(smt_ref[24:32, :] + dtb_c_ref[...])
    for ck in range(nck):
        grow_scr[ck] = g_rows[:, ck * DN_CHUNK:(ck + 1) * DN_CHUNK]

    def pre(bi, carry):
        r0 = pl.multiple_of(bi * PRE_ROWS, PRE_ROWS)
        rows = pl.ds(r0, PRE_ROWS)
        xc = _silu(_conv_block(qkv_ref, cw_ref, r0, seq, width))
        for h in range(DN_HEADS):
            q = xc[:, h * DN_DK:(h + 1) * DN_DK]
            k = xc[:, DN_QK + h * DN_DK:DN_QK + (h + 1) * DN_DK]
            q_scr[rows, h * DN_DK:(h + 1) * DN_DK] = (
                q * lax.rsqrt(jnp.sum(q * q, axis=-1, keepdims=True) + EPS) * (DN_DK ** -0.5))
            k_scr[rows, h * DN_DK:(h + 1) * DN_DK] = (
                k * lax.rsqrt(jnp.sum(k * k, axis=-1, keepdims=True) + EPS))
        v_scr[rows, :] = xc[:, 2 * DN_QK:]
        b_scr[rows, :] = _sigmoid(sm_ref[rows, 16:24])
        g_scr[rows, :] = -jnp.exp(alog_r_ref[...]) * _softplus(sm_ref[rows, 24:32] + dtb_r_ref[...])
        o_ref[rows, :] = jnp.zeros((PRE_ROWS, DN_V), F32)
        return carry

    lax.fori_loop(0, seq // PRE_ROWS, pre, 0)
    s_scr[...] = s0_ref[...]

    r = lax.broadcasted_iota(jnp.int32, (DN_ST, DN_ST), 0)
    c = lax.broadcasted_iota(jnp.int32, (DN_ST, DN_ST), 1)
    same = (r // DN_CHUNK) == (c // DN_CHUNK)
    eye = (r == c).astype(F32)
    tril64 = _tri(DN_CHUNK, True).astype(BF16)
    triu64 = _tri(DN_CHUNK, False).astype(BF16)
    tj = lax.broadcasted_iota(jnp.int32, (DN_CHUNK, DN_ST), 0)
    ti = lax.broadcasted_iota(jnp.int32, (DN_CHUNK, DN_ST), 1) % DN_CHUNK
    cum_f = (tj <= ti).astype(BF16)
    cum_b = (tj >= ti).astype(BF16)

    def chunk_problem(d, ci):
        r0 = pl.multiple_of(ci * DN_CHUNK, DN_CHUNK)
        rows = pl.ds(r0, DN_CHUNK)
        incl = jnp.logical_and(same, (r >= c) if d == 0 else (r <= c))
        strict = jnp.logical_and(same, (r > c) if d == 0 else (r < c))
        gc_c = _dot_exact_lhs(tril64 if d == 0 else triu64, g_scr[rows, :])
        gc_t = _dot_exact_rhs(grow_scr[ci], cum_f if d == 0 else cum_b)
        gc_r = jnp.concatenate(
            [jnp.broadcast_to(gc_t[d * DN_HEADS + h:d * DN_HEADS + h + 1, :], (DN_CHUNK, DN_ST))
             for h in range(DN_HEADS)], axis=0)
        beta = b_scr[rows, :]
        end_row = DN_CHUNK - 1 if d == 0 else 0
        k_st, q_st, v_st, bt_st, gc_st, gl_st = [], [], [], [], [], []
        for h in range(DN_HEADS):
            ln = d * DN_HEADS + h
            k_st.append(k_scr[rows, h * DN_DK:(h + 1) * DN_DK])
            q_st.append(q_scr[rows, h * DN_DK:(h + 1) * DN_DK])
            v_st.append(v_scr[rows, h * DN_DV:(h + 1) * DN_DV])
            bt_st.append(beta[:, ln:ln + 1])
            col = gc_c[:, ln:ln + 1]
            gc_st.append(col)
            gl_st.append(col[end_row:end_row + 1, :])
        kst = jnp.concatenate(k_st, axis=0)
        qst = jnp.concatenate(q_st, axis=0)
        vst = jnp.concatenate(v_st, axis=0)
        bst = jnp.concatenate(bt_st, axis=0)
        gst = jnp.concatenate(gc_st, axis=0)
        decay = jnp.where(incl, jnp.exp(jnp.where(incl, gst - gc_r, 0.0)), 0.0)
        kb = kst * bst
        kstb = kst.astype(BF16)
        m = jnp.where(strict, _bdot_nt(kb, kstb) * decay, 0.0)
        attn = jnp.where(incl, _bdot_nt(qst, kstb) * decay, 0.0)
        a_scr[d, ci] = attn.astype(BF16)
        yield
        t = eye - m
        pb = m.astype(BF16)
        p = jnp.dot(pb, pb, preferred_element_type=F32)
        yield
        for lvl in range(5):
            pb = p.astype(BF16)
            t = t + jnp.dot(t.astype(BF16), pb, preferred_element_type=F32)
            if lvl < 4:
                p = jnp.dot(pb, pb, preferred_element_type=F32)
            yield
        tb = t.astype(BF16)
        rhs = jnp.concatenate([vst * bst, kb * jnp.exp(gst)], axis=1)
        x0 = jnp.dot(tb, rhs.astype(BF16), preferred_element_type=F32)
        yield
        res = rhs - x0 - _dot3(m, x0)
        yield
        uw = x0 + jnp.dot(tb, res.astype(BF16), preferred_element_type=F32)
        yield
        qg = qst * jnp.exp(gst)
        u_scr[d, ci] = uw[:, :DN_DV]
        kdec = []
        for h in range(DN_HEADS):
            hs = slice(h * DN_CHUNK, (h + 1) * DN_CHUNK)
            wq_scr[d, ci, h] = jnp.concatenate([uw[hs, DN_DV:], qg[hs, :]], axis=0).astype(BF16)
            gl = gl_st[h]
            kdec.append(k_st[h] * jnp.exp(gl - gc_st[h]))
            gl_scr[d, ci, h:h + 1, :] = jnp.broadcast_to(jnp.exp(gl), (1, DN_DV))
        kd_scr[d, ci] = jnp.concatenate(kdec, axis=0).astype(BF16)

    def chunk_step(kk, carry):
        _round_robin([chunk_problem(d, kk * DN_PAR + j) for j in range(DN_PAR) for d in range(N_DIR)])
        return carry

    lax.fori_loop(0, nck // DN_PAR, chunk_step, 0)

    def state_problem(d, ci):
        rows = pl.ds(pl.multiple_of(ci * DN_CHUNK, DN_CHUNK), DN_CHUNK)
        s_old, vnew, qs_all = [], [], []
        for h in range(DN_HEADS):
            hs = slice(h * DN_CHUNK, (h + 1) * DN_CHUNK)
            s_h = s_scr[d, h]
            ws = jnp.dot(wq_scr[d, ci, h], s_h.astype(BF16), preferred_element_type=F32)
            s_old.append(s_h)
            vnew.append((u_scr[d, ci, hs, :] - ws[:DN_CHUNK]).astype(BF16))
            qs_all.append(ws[DN_CHUNK:])
        yield
        o_st = jnp.concatenate(qs_all, axis=0) + jnp.dot(
            a_scr[d, ci], jnp.concatenate(vnew, axis=0), preferred_element_type=F32)
        for h in range(DN_HEADS):
            hs = slice(h * DN_CHUNK, (h + 1) * DN_CHUNK)
            s_scr[d, h] = s_old[h] * gl_scr[d, ci, h:h + 1, :] + lax.dot_general(
                kd_scr[d, ci, hs, :], vnew[h], (((0,), (0,)), ((), ())), preferred_element_type=F32)
        yield
        for h in range(DN_HEADS):
            hs = slice(h * DN_CHUNK, (h + 1) * DN_CHUNK)
            o_ref[rows, h * DN_DV:(h + 1) * DN_DV] += o_st[hs, :]

    def state_step(kk, carry):
        _round_robin([state_problem(0, kk), state_problem(1, nck - 1 - kk)])
        return carry

    lax.fori_loop(0, nck, state_step, 0)
    sf_ref[...] = s_scr[...]


def _dn(proj, small_t, row0, nseq, seq, width, conv_w, dt_bias, a_log, s0):
    blk0 = row0 // seq
    nck = seq // DN_CHUNK
    kern = functools.partial(_dn_kernel, seq=seq, width=width)
    full = lambda *shape: pl.BlockSpec(shape, lambda b: (0,) * len(shape))
    dtb_r = dt_bias.reshape(1, 8)
    al_r = a_log.reshape(1, 8)
    dtb_c = dt_bias.reshape(8, 1)
    al_c = a_log.reshape(8, 1)
    return pl.pallas_call(
        kern,
        grid=(nseq,),
        in_specs=[
            pl.BlockSpec((seq, DN_CONV_DIM), lambda b: (blk0 + b, COL_QKV // DN_CONV_DIM)),
            pl.BlockSpec((seq, SMALL_W), lambda b: (blk0 + b, COL_SMALL // SMALL_W)),
            pl.BlockSpec((32, seq), lambda b: (0, blk0 + b)),
            full(3, DN_CONV_DIM), full(1, 8), full(1, 8), full(8, 1), full(8, 1),
            pl.BlockSpec((None, N_DIR, DN_HEADS, DN_DK, DN_DV), lambda b: (b, 0, 0, 0, 0)),
        ],
        out_specs=[
            pl.BlockSpec((seq, DN_V), lambda b: (b, 0)),
            pl.BlockSpec((None, N_DIR, DN_HEADS, DN_DK, DN_DV), lambda b: (b, 0, 0, 0, 0)),
        ],
        out_shape=[jax.ShapeDtypeStruct((nseq * seq, DN_V), F32),
                   jax.ShapeDtypeStruct((nseq, N_DIR, DN_HEADS, DN_DK, DN_DV), F32)],
        scratch_shapes=[
            pltpu.VMEM((seq, DN_QK), F32), pltpu.VMEM((seq, DN_QK), F32), pltpu.VMEM((seq, DN_V), F32),
            pltpu.VMEM((seq, 8), F32), pltpu.VMEM((seq, 8), F32),
            pltpu.VMEM((N_DIR, DN_HEADS, DN_DK, DN_DV), F32),
            pltpu.VMEM((N_DIR, nck, DN_ST, DN_DV), F32),
            pltpu.VMEM((N_DIR, nck, DN_HEADS, 2 * DN_CHUNK, DN_DK), BF16),
            pltpu.VMEM((N_DIR, nck, DN_ST, DN_ST), BF16),
            pltpu.VMEM((N_DIR, nck, DN_ST, DN_DK), BF16),
            pltpu.VMEM((N_DIR, nck, 8, DN_DV), F32),
            pltpu.VMEM((nck, 8, DN_CHUNK), F32),
        ],
        compiler_params=_cparams("parallel"),
        name="dn_seq%d" % seq,
    )(proj, proj, small_t, conv_w, dtb_r, al_r, dtb_c, al_c, s0)


MRG_TM = 512


def _merge_kernel(x_ref, gt_ref, y5_ref, u5_ref, d5_ref, ysc_ref, ysl_ref, z_ref, odc_ref, odl_ref, dg_ref,
                  gr_ref, glu_b, sng_ref, swo_b, dng_ref, dwo_b, wo_b, o_ref):
    i = pl.program_id(0)
    grp = _row_group(i, MRG_TM)
    is_ctx = i < N_CTX // MRG_TM

    g5 = jax.nn.gelu(y5_ref[...] + d5_ref[...] * u5_ref[...]).astype(BF16)
    br_a = (jnp.dot(g5, glu_b[0], preferred_element_type=F32)
            * _sigmoid(jnp.dot(g5, glu_b[1], preferred_element_type=F32)))
    ys = jnp.where(is_ctx, ysc_ref[...], ysl_ref[...]) * _silu(z_ref[...])
    ys = ys * lax.rsqrt(jnp.mean(ys * ys, axis=-1, keepdims=True) + EPS) * sng_ref[...]
    br_b = jnp.dot(ys.astype(BF16), swo_b[...], preferred_element_type=F32)
    od = jnp.where(is_ctx, odc_ref[...], odl_ref[...])
    parts = []
    for h in range(DN_HEADS):
        oh = od[:, h * DN_DV:(h + 1) * DN_DV]
        parts.append(oh * lax.rsqrt(jnp.mean(oh * oh, axis=-1, keepdims=True) + EPS) * dng_ref[...])
    on = jnp.concatenate(parts, axis=1) * _silu(dg_ref[...])
    br_c = jnp.dot(on.astype(BF16), dwo_b[...], preferred_element_type=F32)
    merged = (_sigmoid(gr_ref[:, 0:D_MODEL]) * br_a
              + _sigmoid(gr_ref[:, D_MODEL:2 * D_MODEL]) * br_b
              + _sigmoid(gr_ref[:, 2 * D_MODEL:3 * D_MODEL]) * br_c)
    out = jnp.dot(merged.astype(BF16), wo_b[...], preferred_element_type=F32)
    o_ref[...] = x_ref[...] + gt_ref[pl.ds(grp, 1), :] * out


def _merge(x, mod, proj, y5, s5_d_l, ys_c, ys_l, od_c, od_l, s5_glu_l, ssd_norm_g_l, ssd_w_out_l, dn_norm_g_l,
           dn_w_out_l, w_out_l):
    nctx = N_CTX // MRG_TM
    rowblk = lambda w, col: pl.BlockSpec((MRG_TM, w), lambda i: (i, col // w))
    ctxblk = lambda w: pl.BlockSpec((MRG_TM, w), lambda i: (jnp.minimum(i, nctx - 1), 0))
    latblk = lambda w: pl.BlockSpec((MRG_TM, w), lambda i: (jnp.maximum(i - nctx, 0), 0))
    full = lambda *shape: pl.BlockSpec(shape, lambda i: (0,) * len(shape), pipeline_mode=pl.Buffered(1))
    return pl.pallas_call(
        _merge_kernel,
        grid=(N_ROWS // MRG_TM,),
        in_specs=[
            rowblk(D_MODEL, 0),
            pl.BlockSpec((8, D_MODEL), lambda i: (0, 5)),
            rowblk(S5_WIDTH, 0), rowblk(S5_WIDTH, COL_U), full(1, S5_WIDTH),
            ctxblk(SSD_WIDTH), latblk(SSD_WIDTH), rowblk(SSD_WIDTH, COL_Z),
            ctxblk(DN_V), latblk(DN_V), rowblk(DN_V, COL_DNG), rowblk(3 * D_MODEL, COL_GATES),
            full(2, S5_WIDTH, D_MODEL), full(1, SSD_WIDTH), full(SSD_WIDTH, D_MODEL),
            full(1, DN_DV), full(DN_V, D_MODEL), full(D_MODEL, D_MODEL),
        ],
        out_specs=rowblk(D_MODEL, 0),
        out_shape=jax.ShapeDtypeStruct((N_ROWS, D_MODEL), F32),
        compiler_params=_cparams("parallel"),
        name="merge",
    )(x, mod, y5, proj, s5_d_l.reshape(1, -1), ys_c, ys_l, proj, od_c, od_l, proj, proj, s5_glu_l.astype(BF16),
      ssd_norm_g_l.reshape(1, -1), ssd_w_out_l.astype(BF16), dn_norm_g_l.reshape(1, -1),
      dn_w_out_l.astype(BF16), w_out_l.astype(BF16))


FIN_TM = 1024


def _final_norm_kernel(x_ref, g_ref, oc_ref, ol_ref):
    i = pl.program_id(0)
    x = x_ref[...]
    y = x * lax.rsqrt(jnp.mean(x * x, axis=-1, keepdims=True) + EPS) * g_ref[...]

    @pl.when(i < N_CTX // FIN_TM)
    def _():
        oc_ref[...] = y

    @pl.when(i >= N_CTX // FIN_TM)
    def _():
        ol_ref[...] = y


def _final_norm(x, g):
    nctx = N_CTX // FIN_TM
    return pl.pallas_call(
        _final_norm_kernel,
        grid=(N_ROWS // FIN_TM,),
        in_specs=[pl.BlockSpec((FIN_TM, D_MODEL), lambda i: (i, 0)), pl.BlockSpec((1, D_MODEL), lambda i: (0, 0))],
        out_specs=[pl.BlockSpec((FIN_TM, D_MODEL), lambda i: (jnp.minimum(i, nctx - 1), 0)),
                   pl.BlockSpec((FIN_TM, D_MODEL), lambda i: (jnp.maximum(i - nctx, 0), 0))],
        out_shape=[jax.ShapeDtypeStruct((N_CTX, D_MODEL), F32), jax.ShapeDtypeStruct((N_LAT, D_MODEL), F32)],
        compiler_params=_cparams("arbitrary"),
        name="final_norm",
    )(x, g.reshape(1, -1))


def kernel(x_prompt, x_sample, state_s5_re, state_s5_im, state_ssd, state_dn, c, c_ctx, ada_w, ada_b, norm_g, ffn_wi, ffn_wo, w_in, s5_lam_re, s5_lam_im, s5_log_dt, s5_b_re, s5_b_im, s5_c_re, s5_c_im, s5_d, s5_glu, ssd_conv_w, ssd_conv_b, ssd_dt_bias, ssd_a_log, ssd_d, ssd_norm_g, ssd_w_out, dn_conv_w, dn_dt_bias, dn_a_log, dn_norm_g, dn_w_out, w_out, final_norm_g):
    x = jnp.concatenate([x_prompt.reshape(N_CTX, D_MODEL), x_sample.reshape(N_LAT, D_MODEL)], axis=0)
    cond8 = jnp.concatenate([c_ctx[None, :], c, jnp.zeros((8 - 1 - DEC_BATCH, D_MODEL), F32)], axis=0)
    mods = _ada_mods(cond8, ada_w, ada_b)

    s5_tables = _s5_prep(s5_lam_re, s5_lam_im, s5_log_dt, s5_b_re, s5_b_im, s5_c_re, s5_c_im)
    s5_perm = _s5_perm()
    w_packed = _repack_w_in(w_in)
    ssd_d_rows = jnp.repeat(ssd_d, SSD_HEADDIM, axis=1).reshape(DEPTH, 1, SSD_WIDTH)

    def s5_h0(state):
        return state.transpose(1, 3, 0, 2, 4).reshape(DEPTH, S5_GROUPS, DEC_BATCH, 2 * S5_STATE)

    h0_re = s5_h0(state_s5_re)
    h0_im = s5_h0(state_s5_im)
    ssd_h0_lat = _ssd_state_to_pairs(jnp.swapaxes(state_ssd, 0, 1))
    ssd_h0_ctx = jnp.zeros((BATCH, N_DIR, SSD_PAIRS, SSD_STATE, SSD_PAIR), F32)
    dn_s0_lat = jnp.swapaxes(state_dn, 0, 1)
    dn_s0_ctx = jnp.zeros((BATCH, N_DIR, DN_HEADS, DN_DK, DN_DV), F32)

    new_s5_re, new_s5_im, new_ssd, new_dn = [], [], [], []
    for l in range(DEPTH):
        mod = mods[l]
        x = _ffn(x, mod, norm_g[l, 0:1], ffn_wi, ffn_wo, l, 0)
        proj, small_t = _inproj(x, mod, norm_g[l, 1:2], w_packed, l)

        y5, f_re, f_im = _s5_scan(proj, s5_perm, s5_tables, h0_re, h0_im, l)
        new_s5_re.append(f_re.reshape(S5_GROUPS, BATCH, N_DIR, S5_STATE).transpose(1, 2, 0, 3))
        new_s5_im.append(f_im.reshape(S5_GROUPS, BATCH, N_DIR, S5_STATE).transpose(1, 2, 0, 3))

        ys_c, hs_c = _ssd(proj, small_t, 0, BATCH, SEQ, SEQ, ssd_conv_w[l], ssd_conv_b[l], ssd_dt_bias[l],
                          ssd_a_log[l], ssd_d_rows[l], ssd_h0_ctx)
        ys_l, _ = _ssd(proj, small_t, N_CTX, DEC_BATCH, DEC_SEQ, GRID_W, ssd_conv_w[l], ssd_conv_b[l],
                       ssd_dt_bias[l], ssd_a_log[l], ssd_d_rows[l], ssd_h0_lat[l])
        new_ssd.append(_ssd_state_from_pairs(hs_c))
        od_c, sd_c = _dn(proj, small_t, 0, BATCH, SEQ, SEQ, dn_conv_w[l], dn_dt_bias[l], dn_a_log[l], dn_s0_ctx)
        od_l, _ = _dn(proj, small_t, N_CTX, DEC_BATCH, DEC_SEQ, GRID_W, dn_conv_w[l], dn_dt_bias[l],
                      dn_a_log[l], dn_s0_lat[l])
        new_dn.append(sd_c)

        x = _merge(x, mod, proj, y5, s5_d[l], ys_c, ys_l, od_c, od_l, s5_glu[l], ssd_norm_g[l], ssd_w_out[l],
                   dn_norm_g[l], dn_w_out[l], w_out[l])
        x = _ffn(x, mod, norm_g[l, 2:3], ffn_wi, ffn_wo, l, 1)

    y_ctx, y_lat = _final_norm(x, final_norm_g)
    y_prompt = y_ctx.reshape(BATCH, SEQ, D_MODEL)
    y_sample = y_lat.reshape(DEC_BATCH, DEC_SEQ, D_MODEL)
    return (y_prompt, y_sample, jnp.stack(new_s5_re, axis=1), jnp.stack(new_s5_im, axis=1),
            jnp.stack(new_ssd, axis=1), jnp.stack(new_dn, axis=1))
```

```python
import functools

import jax
import jax.numpy as jnp
import numpy as np
from jax import lax
from jax.experimental import pallas as pl
from jax.experimental.pallas import tpu as pltpu

F32 = jnp.float32
BF16 = jnp.bfloat16

D_MODEL = 1024
BATCH = 16
SEQ = 256
DEPTH = 4
DEC_BATCH = 2
DEC_SEQ = 1024
GRID_W = 64
N_DIR = 2
N_ADA = 9
D_FF = 2816
EPS = 1e-6

S5_WIDTH = 512
S5_GROUP = 16
S5_GROUPS = 32
S5_STATE = 64
S5_CHUNK = 16

SSD_WIDTH = 512
SSD_HEADDIM = 64
SSD_HEADS = 8
SSD_GROUPS = 2
SSD_STATE = 64
SSD_CHUNK = 128
SSD_CONV_DIM = 768

DN_HEADS = 4
DN_DK = 128
DN_DV = 128
DN_QK = 512
DN_V = 512
DN_CHUNK = 64
DN_CONV_DIM = 1536

IN_SEGMENTS = (512, 512, 768, 16, 1536, 8, 8, 512, 3072)
IN_SPLITS = tuple(int(s) for s in np.cumsum(IN_SEGMENTS)[:-1])

N_CTX = BATCH * SEQ
N_LAT = DEC_BATCH * DEC_SEQ
N_ROWS = N_CTX + N_LAT

COL_QKV = 0
COL_U = 1536
COL_Z = 2048
COL_DNG = 2560
COL_XBC = 3072
COL_SMALL = 3840
PROJ_W = 4096
COL_GATES = PROJ_W
PACK_W = PROJ_W + 3 * D_MODEL
SMALL_W = 128

VMEM_LIMIT = 56 * 1024 * 1024


def _cparams(*sem):
    return pltpu.CompilerParams(dimension_semantics=sem, vmem_limit_bytes=VMEM_LIMIT)


def _sigmoid(x):
    return 0.5 * (jnp.tanh(0.5 * x) + 1.0)


def _silu(x):
    return x * _sigmoid(x)


def _softplus(x):
    return jnp.maximum(x, 0.0) + jnp.log(1.0 + jnp.exp(-jnp.abs(x)))


def _bdot(a, b):
    return jnp.dot(a.astype(BF16), b.astype(BF16), preferred_element_type=F32)


def _bdot_nt(a, b):
    return lax.dot_general(a.astype(BF16), b.astype(BF16), (((1,), (1,)), ((), ())),
                           preferred_element_type=F32)


def _bdot_tn(a, b):
    return lax.dot_general(a.astype(BF16), b.astype(BF16), (((0,), (0,)), ((), ())),
                           preferred_element_type=F32)


def _split3(a):
    hi = a.astype(BF16)
    r = a - hi.astype(F32)
    mid = r.astype(BF16)
    lo = (r - mid.astype(F32)).astype(BF16)
    return hi, mid, lo


def _dot3(a, b):
    ah = a.astype(BF16)
    al = (a - ah.astype(F32)).astype(BF16)
    bh = b.astype(BF16)
    bl = (b - bh.astype(F32)).astype(BF16)
    out = jnp.dot(ah, bh, preferred_element_type=F32)
    out = out + jnp.dot(ah, bl, preferred_element_type=F32)
    out = out + jnp.dot(al, bh, preferred_element_type=F32)
    return out


def _dot_exact_lhs(t_bf16, x):
    hi, mid, lo = _split3(x)
    out = jnp.dot(t_bf16, hi, preferred_element_type=F32)
    out = out + jnp.dot(t_bf16, mid, preferred_element_type=F32)
    out = out + jnp.dot(t_bf16, lo, preferred_element_type=F32)
    return out


def _dot_exact_rhs(x, t_bf16):
    hi, mid, lo = _split3(x)
    out = jnp.dot(hi, t_bf16, preferred_element_type=F32)
    out = out + jnp.dot(mid, t_bf16, preferred_element_type=F32)
    out = out + jnp.dot(lo, t_bf16, preferred_element_type=F32)
    return out


def _norm_mod(x, g, sc, sh):
    ms = jnp.mean(x * x, axis=-1, keepdims=True)
    y = x * lax.rsqrt(ms + EPS) * g
    return y * (1.0 + sc) + sh


def _row_group(i, tm):
    nctx = N_CTX // tm
    per = DEC_SEQ // tm
    return jnp.where(i < nctx, 0, 1 + jnp.maximum(i - nctx, 0) // per)


ADA_TN = 1152


def _ada_kernel(c_ref, w_ref, b_ref, o_ref):
    c = c_ref[...]
    o_ref[...] = _bdot(_silu(c), w_ref[...]) + b_ref[...]


def _ada_mods(cond8, ada_w, ada_b):
    nj = (N_ADA * D_MODEL) // ADA_TN
    return pl.pallas_call(
        _ada_kernel,
        grid=(DEPTH, nj),
        in_specs=[
            pl.BlockSpec((8, D_MODEL), lambda l, j: (0, 0)),
            pl.BlockSpec((None, D_MODEL, ADA_TN), lambda l, j: (l, 0, j)),
            pl.BlockSpec((None, 1, ADA_TN), lambda l, j: (l, 0, j)),
        ],
        out_specs=pl.BlockSpec((None, 8, ADA_TN), lambda l, j: (l, 0, j)),
        out_shape=jax.ShapeDtypeStruct((DEPTH, 8, N_ADA * D_MODEL), F32),
        compiler_params=_cparams("parallel", "parallel"),
        name="ada_mods",
    )(cond8, ada_w, ada_b.reshape(DEPTH, 1, N_ADA * D_MODEL))


FFN_TM = 2048
FFN_TF = 256
MOD_ROWS = 1024
FFN_SUB = FFN_TM // MOD_ROWS


def _ffn_kernel(x_ref, sh_ref, sc_ref, gt_ref, g_ref, wa_ref, wb_ref, wo_ref, o_ref, h_scr, acc_scr):
    i = pl.program_id(0)
    j = pl.program_id(1)

    @pl.when(j == 0)
    def _():
        for s in range(FFN_SUB):
            rows = slice(s * MOD_ROWS, (s + 1) * MOD_ROWS)
            grp = _row_group(i * FFN_SUB + s, MOD_ROWS)
            h = _norm_mod(x_ref[rows, :], g_ref[...], sc_ref[pl.ds(grp, 1), :], sh_ref[pl.ds(grp, 1), :])
            h_scr[rows, :] = h.astype(BF16)
        acc_scr[...] = jnp.zeros_like(acc_scr)

    h = h_scr[...]
    a = jnp.dot(h, wa_ref[...].astype(BF16), preferred_element_type=F32)
    b = jnp.dot(h, wb_ref[...].astype(BF16), preferred_element_type=F32)
    u = (_silu(a) * b).astype(BF16)
    acc_scr[...] += jnp.dot(u, wo_ref[...].astype(BF16), preferred_element_type=F32)

    @pl.when(j == pl.num_programs(1) - 1)
    def _():
        for s in range(FFN_SUB):
            rows = slice(s * MOD_ROWS, (s + 1) * MOD_ROWS)
            grp = _row_group(i * FFN_SUB + s, MOD_ROWS)
            o_ref[rows, :] = x_ref[rows, :] + (0.5 * gt_ref[pl.ds(grp, 1), :]) * acc_scr[rows, :]


def _ffn(x, mod, norm_g_row, ffn_wi, ffn_wo, layer, which):
    nf = D_FF // FFN_TF
    base = 0 if which == 0 else 6
    return pl.pallas_call(
        _ffn_kernel,
        grid=(N_ROWS // FFN_TM, nf),
        in_specs=[
            pl.BlockSpec((FFN_TM, D_MODEL), lambda i, j: (i, 0)),
            pl.BlockSpec((8, D_MODEL), lambda i, j: (0, base)),
            pl.BlockSpec((8, D_MODEL), lambda i, j: (0, base + 1)),
            pl.BlockSpec((8, D_MODEL), lambda i, j: (0, base + 2)),
            pl.BlockSpec((1, D_MODEL), lambda i, j: (0, 0)),
            pl.BlockSpec((None, None, D_MODEL, FFN_TF), lambda i, j: (layer, which, 0, j)),
            pl.BlockSpec((None, None, D_MODEL, FFN_TF), lambda i, j: (layer, which, 0, j + nf)),
            pl.BlockSpec((None, None, FFN_TF, D_MODEL), lambda i, j: (layer, which, j, 0)),
        ],
        out_specs=pl.BlockSpec((FFN_TM, D_MODEL), lambda i, j: (i, 0)),
        out_shape=jax.ShapeDtypeStruct((N_ROWS, D_MODEL), F32),
        scratch_shapes=[pltpu.VMEM((FFN_TM, D_MODEL), BF16), pltpu.VMEM((FFN_TM, D_MODEL), F32)],
        compiler_params=_cparams("parallel", "arbitrary"),
        name="ffn",
    )(x, mod, mod, mod, norm_g_row, ffn_wi, ffn_wi, ffn_wo)


INP_TM = 512


def _inproj_kernel(x_ref, sh_ref, sc_ref, g_ref, w_ref, o_ref, ot_ref):
    grp = _row_group(pl.program_id(0), INP_TM)
    h = _norm_mod(x_ref[...], g_ref[...], sc_ref[pl.ds(grp, 1), :], sh_ref[pl.ds(grp, 1), :])
    res = lax.dot_general(h.astype(BF16), w_ref[...], (((1,), (1,)), ((), ())), preferred_element_type=F32)
    o_ref[...] = res
    ot_ref[...] = res[:, COL_SMALL:COL_SMALL + SMALL_W].T


RPK_LANES = 256
RPK_MOVES = ((COL_QKV, 1808, 1536), (COL_U, 0, 512), (COL_Z, 512, 512), (COL_DNG, 3360, 512),
             (COL_XBC, 1024, 768), (COL_SMALL, 1792, 16), (COL_SMALL + 16, 3344, 16), (COL_GATES, 3872, 3072))
RPK_PAD = (COL_SMALL + 32, PROJ_W)


def _repack_kernel(w_ref, o_ref):
    for dst, src, n in RPK_MOVES:
        o_ref[dst:dst + n, :] = w_ref[src:src + n, :].astype(BF16)
    o_ref[RPK_PAD[0]:RPK_PAD[1], :] = jnp.zeros((RPK_PAD[1] - RPK_PAD[0], RPK_LANES), BF16)


def _repack_w_in(w_in):
    w_t = jnp.swapaxes(w_in, 1, 2)
    in_w = w_t.shape[1]
    return pl.pallas_call(
        _repack_kernel,
        grid=(DEPTH, D_MODEL // RPK_LANES),
        in_specs=[pl.BlockSpec((None, in_w, RPK_LANES), lambda l, i: (l, 0, i))],
        out_specs=pl.BlockSpec((None, PACK_W, RPK_LANES), lambda l, i: (l, 0, i)),
        out_shape=jax.ShapeDtypeStruct((DEPTH, PACK_W, D_MODEL), BF16),
        compiler_params=_cparams("parallel", "parallel"),
        name="repack_w_in",
    )(w_t)


def _inproj(x, mod, norm_g_row, w_packed, layer):
    return pl.pallas_call(
        _inproj_kernel,
        grid=(N_ROWS // INP_TM,),
        in_specs=[
            pl.BlockSpec((INP_TM, D_MODEL), lambda i: (i, 0)),
            pl.BlockSpec((8, D_MODEL), lambda i: (0, 3)),
            pl.BlockSpec((8, D_MODEL), lambda i: (0, 4)),
            pl.BlockSpec((1, D_MODEL), lambda i: (0, 0)),
            pl.BlockSpec((None, PROJ_W, D_MODEL), lambda i: (layer, 0, 0)),
        ],
        out_specs=[pl.BlockSpec((INP_TM, PROJ_W), lambda i: (i, 0)),
                   pl.BlockSpec((SMALL_W, INP_TM), lambda i: (0, i))],
        out_shape=[jax.ShapeDtypeStruct((N_ROWS, PROJ_W), F32),
                   jax.ShapeDtypeStruct((SMALL_W, N_ROWS), F32)],
        compiler_params=_cparams("parallel"),
        name="inproj",
    )(x, mod, mod, norm_g_row, w_packed)


S5_ROW = S5_CHUNK * S5_GROUP


def _s5_prep_kernel(lam_re_ref, lam_im_ref, ldt_ref, btr_ref, bti_ref, cr_ref, ci_ref, ctr_ref, cti_ref,
                    m_ref, pre_ref, pim_ref, qre_ref, qim_ref, are_ref, aim_ref):
    tau = lax.broadcasted_iota(jnp.int32, (S5_CHUNK, 1), 0).astype(F32)
    lane = lax.broadcasted_iota(jnp.int32, (S5_GROUP, S5_ROW), 1)
    lane_t = lane[0:1, :] // S5_GROUP
    rep = (lax.broadcasted_iota(jnp.int32, (S5_GROUP, S5_ROW), 0) == lane % S5_GROUP).astype(BF16)
    taps = [None, None]

    def outer(ar, ai, xr, xi):
        rr = ar[:, None, :] * xr[None, :, :] - ai[:, None, :] * xi[None, :, :]
        ii = ar[:, None, :] * xi[None, :, :] + ai[:, None, :] * xr[None, :, :]
        return rr.reshape(S5_ROW, S5_STATE), ii.reshape(S5_ROW, S5_STATE)

    def direction(d):
        lr = lam_re_ref[d:d + 1, :]
        li = lam_im_ref[d:d + 1, :]
        dt = jnp.exp(ldt_ref[d:d + 1, :])
        mag = jnp.exp(lr * dt)
        lb_re = mag * jnp.cos(li * dt)
        lb_im = mag * jnp.sin(li * dt)
        den = lr * lr + li * li
        cr = ((lb_re - 1.0) * lr + lb_im * li) / den
        ci = (lb_im * lr - (lb_re - 1.0) * li) / den
        bt_r = btr_ref[d]
        bt_i = bti_ref[d]
        bbt_r = cr * bt_r - ci * bt_i
        bbt_i = cr * bt_i + ci * bt_r
        c_r = cr_ref[d]
        c_i = ci_ref[d]

        def powtab(t):
            m = jnp.exp(t * (lr * dt))
            ang = t * (li * dt)
            return m * jnp.cos(ang), m * jnp.sin(ang)

        t_in = (S5_CHUNK - 1) - tau if d == 0 else tau
        ar, ai = powtab(t_in)
        ba_r, ba_i = outer(ar, ai, bbt_r, bbt_i)
        lanes = slice(d * S5_STATE, (d + 1) * S5_STATE)
        pre_ref[:, lanes] = ba_r.astype(BF16)
        pim_ref[:, lanes] = ba_i.astype(BF16)
        yield
        kt = _dot3(ba_r, ctr_ref[d]) - _dot3(ba_i, cti_ref[d])
        yield
        taps[d] = _dot_exact_rhs(kt, rep)
        yield
        t_out = tau + 1.0 if d == 0 else S5_CHUNK - tau
        ar, ai = powtab(t_out)
        qr, qi = outer(ar, ai, c_r, c_i)
        qre_ref[:, lanes] = qr.astype(BF16)
        qim_ref[:, lanes] = (-qi).astype(BF16)
        a16r, a16i = powtab(jnp.full((1, 1), float(S5_CHUNK), F32))
        are_ref[:, lanes] = a16r
        aim_ref[:, lanes] = a16i

    _round_robin([direction(d) for d in range(N_DIR)])

    last = S5_ROW - S5_GROUP
    table = jnp.concatenate([taps[0][:last, :], taps[0][last:, :] + taps[1][:S5_GROUP, :], taps[1][S5_GROUP:, :]],
                            axis=0)
    mmat = table[last:last + S5_ROW, :]
    for t in range(1, S5_CHUNK):
        start = (S5_CHUNK - 1 - t) * S5_GROUP
        mmat = jnp.where(lane_t == t, table[start:start + S5_ROW, :], mmat)
    m_ref[...] = mmat.astype(BF16)


def _s5_prep(lam_re, lam_im, log_dt, b_re, b_im, c_re, c_im):
    tg = lambda t: jnp.swapaxes(t, 1, 2)
    lam_re_g = tg(lam_re)
    lam_im_g = tg(lam_im)
    ldt_g = tg(log_dt)[..., None]
    bt_r = jnp.swapaxes(tg(b_re), -1, -2)
    bt_i = jnp.swapaxes(tg(b_im), -1, -2)
    c_r = tg(c_re)
    c_i = tg(c_im)
    ct_r = jnp.swapaxes(c_r, -1, -2)
    ct_i = jnp.swapaxes(c_i, -1, -2)

    def spec(*tail):
        n = len(tail)
        return pl.BlockSpec((None, None) + tail, lambda l, g: (l, g) + (0,) * n)

    st = 2 * S5_STATE
    tab = jax.ShapeDtypeStruct((DEPTH, S5_GROUPS, S5_ROW, st), BF16)
    dec = jax.ShapeDtypeStruct((DEPTH, S5_GROUPS, 1, st), F32)
    return pl.pallas_call(
        _s5_prep_kernel,
        grid=(DEPTH, S5_GROUPS),
        in_specs=[spec(2, 64), spec(2, 64), spec(2, 1), spec(2, 16, 64), spec(2, 16, 64),
                  spec(2, 16, 64), spec(2, 16, 64), spec(2, 64, S5_GROUP), spec(2, 64, S5_GROUP)],
        out_specs=[spec(S5_ROW, S5_ROW), spec(S5_ROW, st), spec(S5_ROW, st), spec(S5_ROW, st), spec(S5_ROW, st),
                   spec(1, st), spec(1, st)],
        out_shape=[jax.ShapeDtypeStruct((DEPTH, S5_GROUPS, S5_ROW, S5_ROW), BF16), tab, tab, tab, tab, dec, dec],
        compiler_params=_cparams("parallel", "parallel"),
        name="s5_prep",
    )(lam_re_g, lam_im_g, ldt_g, bt_r, bt_i, c_r, c_i, ct_r, ct_i)


S5_CTX_CH = SEQ // S5_CHUNK
S5_LAT_CH = DEC_SEQ // S5_CHUNK
S5_CTX_ROWS = S5_CTX_CH * BATCH
S5_LAT_ROWS = S5_LAT_CH * DEC_BATCH
S5_ROWS = S5_CTX_ROWS + S5_LAT_ROWS
S5_GB = 128 // S5_GROUP
S5_PERM = S5_GB * 128


def _s5_perm():
    src = np.arange(S5_PERM)
    s, g, j = src // 128, (src % 128) // S5_GROUP, src % S5_GROUP
    p = np.zeros((S5_PERM, S5_PERM), np.float32)
    p[src, g * 128 + s * S5_GROUP + j] = 1.0
    return jnp.asarray(p, BF16)


def _s5_kernel(u_ref, perm_ref, m_ref, pre_ref, pim_ref, qre_ref, qim_ref, are_ref, aim_ref, h0r_ref, h0i_ref,
               y_ref, fr_ref, fi_ref, ug, sre, sim, hfr, hfi, hbr, hbi, ys):
    perm = perm_ref[...]
    half_w = S5_ROW // 2
    for half in range(2):
        x = jnp.concatenate([u_ref[pl.ds(half * 8 + s, S5_ROWS, stride=S5_CHUNK), :] for s in range(8)], axis=1)
        z = jnp.dot(x.astype(BF16), perm, preferred_element_type=F32)
        for g in range(S5_GB):
            ug[g, :, half * half_w:(half + 1) * half_w] = z[:, g * 128:(g + 1) * 128].astype(BF16)
    for g in range(S5_GB):
        sre[g] = jnp.dot(ug[g], pre_ref[g].astype(BF16), preferred_element_type=F32)
        sim[g] = jnp.dot(ug[g], pim_ref[g].astype(BF16), preferred_element_type=F32)
    fwd = lax.broadcasted_iota(jnp.int32, (1, 2 * S5_STATE), 1) < S5_STATE

    def scan(base, nchunk, nseq, init):
        def step(k, hs):
            rf = pl.ds(base + k, nseq, stride=nchunk)
            rb = pl.ds(base + nchunk - 1 - k, nseq, stride=nchunk)
            out = []
            for g in range(S5_GB):
                h_re, h_im = hs[2 * g], hs[2 * g + 1]
                hfr.at[g][rf, :] = h_re
                hfi.at[g][rf, :] = h_im
                hbr.at[g][rb, :] = h_re
                hbi.at[g][rb, :] = h_im
                s_r = jnp.where(fwd, sre.at[g][rf, :], sre.at[g][rb, :])
                s_i = jnp.where(fwd, sim.at[g][rf, :], sim.at[g][rb, :])
                ar = are_ref[g]
                ai = aim_ref[g]
                out.append(ar * h_re - ai * h_im + s_r)
                out.append(ar * h_im + ai * h_re + s_i)
            return tuple(out)

        return lax.fori_loop(0, nchunk, step, init)

    zero = jnp.zeros((BATCH, 2 * S5_STATE), F32)
    fin = scan(0, S5_CTX_CH, BATCH, (zero,) * (2 * S5_GB))
    lat0 = []
    for g in range(S5_GB):
        fr_ref[g] = fin[2 * g]
        fi_ref[g] = fin[2 * g + 1]
        lat0 += [h0r_ref[g], h0i_ref[g]]
    scan(S5_CTX_ROWS, S5_LAT_CH, DEC_BATCH, tuple(lat0))

    for g in range(S5_GB):
        h_re = jnp.where(fwd, hfr[g], hbr[g])
        h_im = jnp.where(fwd, hfi[g], hbi[g])
        y = jnp.dot(ug[g], m_ref[g].astype(BF16), preferred_element_type=F32)
        ys[g] = y + _bdot_nt(h_re, qre_ref[g]) + _bdot_nt(h_im, qim_ref[g])
    for half in range(2):
        w = jnp.concatenate([ys[g, :, half * half_w:(half + 1) * half_w] for g in range(S5_GB)], axis=1)
        zo = sum(lax.dot_general(piece, perm, (((1,), (1,)), ((), ())), preferred_element_type=F32)
                 for piece in _split3(w))
        for t in range(8):
            y_ref[pl.ds(half * 8 + t, S5_ROWS, stride=S5_CHUNK), :] = zo[:, t * 128:(t + 1) * 128]


def _s5_scan(proj, perm, tables, h0_re, h0_im, layer):
    def lspec(*tail):
        n = len(tail)
        return pl.BlockSpec((None, S5_GB) + tail, lambda t: (layer, t) + (0,) * n)

    def gspec(*tail):
        n = len(tail)
        return pl.BlockSpec((S5_GB,) + tail, lambda t: (t,) + (0,) * n)

    st = 2 * S5_STATE
    return pl.pallas_call(
        _s5_kernel,
        grid=(S5_GROUPS // S5_GB,),
        in_specs=[pl.BlockSpec((N_ROWS, 128), lambda t: (0, COL_U // 128 + t)),
                  pl.BlockSpec((S5_PERM, S5_PERM), lambda t: (0, 0)),
                  lspec(S5_ROW, S5_ROW), lspec(S5_ROW, st), lspec(S5_ROW, st), lspec(S5_ROW, st),
                  lspec(S5_ROW, st), lspec(1, st), lspec(1, st), lspec(DEC_BATCH, st), lspec(DEC_BATCH, st)],
        out_specs=[pl.BlockSpec((N_ROWS, 128), lambda t: (0, t)), gspec(BATCH, st), gspec(BATCH, st)],
        out_shape=[jax.ShapeDtypeStruct((N_ROWS, S5_WIDTH), F32),
                   jax.ShapeDtypeStruct((S5_GROUPS, BATCH, st), F32),
                   jax.ShapeDtypeStruct((S5_GROUPS, BATCH, st), F32)],
        scratch_shapes=([pltpu.VMEM((S5_GB, S5_ROWS, S5_ROW), BF16)]
                        + [pltpu.VMEM((S5_GB, S5_ROWS, st), F32) for _ in range(6)]
                        + [pltpu.VMEM((S5_GB, S5_ROWS, S5_ROW), F32)]),
        compiler_params=_cparams("parallel"),
        name="s5_scan",
    )(proj, perm, *tables, h0_re, h0_im)


PRE_ROWS = 128


def _conv_block(x_ref, w_ref, r0, seq, width):
    x = x_ref[pl.ds(r0, PRE_ROWS), :]
    prev = x_ref[pl.ds(jnp.maximum(r0 - 1, 0), 1), :]
    nxt = x_ref[pl.ds(jnp.minimum(r0 + PRE_ROWS, seq - 1), 1), :]
    rid = lax.broadcasted_iota(jnp.int32, (PRE_ROWS, 1), 0)
    pos = (r0 + rid) % width
    xm = jnp.where(rid == 0, prev, pltpu.roll(x, 1, 0))
    xm = jnp.where(pos == 0, 0.0, xm)
    xp = jnp.where(rid == PRE_ROWS - 1, nxt, pltpu.roll(x, PRE_ROWS - 1, 0))
    xp = jnp.where(pos == width - 1, 0.0, xp)
    return xm * w_ref[0:1, :] + x * w_ref[1:2, :] + xp * w_ref[2:3, :]


def _round_robin(problems):
    live = list(problems)
    while live:
        nxt = []
        for p in live:
            try:
                next(p)
                nxt.append(p)
            except StopIteration:
                pass
        live = nxt


def _tri(n, lower):
    r = lax.broadcasted_iota(jnp.int32, (n, n), 0)
    c = lax.broadcasted_iota(jnp.int32, (n, n), 1)
    return (r >= c) if lower else (r <= c)


SSD_PAIR = 2 * SSD_HEADDIM
SSD_PAIRS = SSD_HEADS // 2


def _ssd_kernel(xbc_ref, sm_ref, smt_ref, cw_ref, cb_ref, dtb_r_ref, dtb_c_ref, alog_r_ref, alog_c_ref,
                dvec_ref, h0_ref, y_ref, hf_ref, xs_scr, bc_scr, dac_scr, dar_scr, dtr_scr, h_scr,
                *, seq, width):
    nck = seq // SSD_CHUNK

    def pre(bi, carry):
        r0 = pl.multiple_of(bi * PRE_ROWS, PRE_ROWS)
        rows = pl.ds(r0, PRE_ROWS)
        xc = _silu(_conv_block(xbc_ref, cw_ref, r0, seq, width) + cb_ref[...])
        xs = xc[:, :SSD_WIDTH]
        xs_scr[rows, :] = xs
        bc_scr[rows, :] = xc[:, SSD_WIDTH:]
        y_ref[rows, :] = dvec_ref[...] * xs
        dt_c = _softplus(sm_ref[rows, 0:16] + dtb_r_ref[...])
        dac_scr[rows, :] = dt_c * (-jnp.exp(alog_r_ref[...]))
        return carry

    lax.fori_loop(0, seq // PRE_ROWS, pre, 0)
    dt_r = _softplus(smt_ref[0:16, :] + dtb_c_ref[...])
    da_r = dt_r * (-jnp.exp(alog_c_ref[...]))
    for ck in range(nck):
        dtr_scr[ck] = dt_r[:, ck * SSD_CHUNK:(ck + 1) * SSD_CHUNK]
        dar_scr[ck] = da_r[:, ck * SSD_CHUNK:(ck + 1) * SSD_CHUNK]
    h_scr[...] = h0_ref[...]

    tril = _tri(SSD_CHUNK, True)
    triu = _tri(SSD_CHUNK, False)
    tril_b = tril.astype(BF16)
    triu_b = triu.astype(BF16)
    lo_half = lax.broadcasted_iota(jnp.int32, (1, SSD_PAIR), 1) < SSD_HEADDIM

    def chunk_problem(dirs, k):
        for d in dirs:
            c = k if d == 0 else nck - 1 - k
            r0 = pl.multiple_of(c * SSD_CHUNK, SSD_CHUNK)
            rows = pl.ds(r0, SSD_CHUNK)
            mask = tril if d == 0 else triu
            ac = _dot_exact_lhs(tril_b if d == 0 else triu_b, dac_scr[rows, :])
            at = _dot_exact_rhs(dar_scr[c], triu_b if d == 0 else tril_b)
            dt_row = dtr_scr[c]
            end = SSD_CHUNK - 1 if d == 0 else 0
            bcx = bc_scr[rows, :]
            gmat = []
            for g in range(SSD_GROUPS):
                bm = bcx[:, g * SSD_STATE:(g + 1) * SSD_STATE]
                cm = bcx[:, 2 * SSD_STATE + g * SSD_STATE:2 * SSD_STATE + (g + 1) * SSD_STATE]
                gmat.append((bm.T, cm, _bdot_nt(cm, bm)))
            yield
            for pr in range(SSD_PAIRS):
                bmt, cm, gm = gmat[pr // (SSD_PAIRS // SSD_GROUPS)]
                xpair = xs_scr[rows, pr * SSD_PAIR:(pr + 1) * SSD_PAIR]
                sc, bt, es, dec, xh = [], [], [], [], []
                for half in range(2):
                    ln = d * SSD_HEADS + 2 * pr + half
                    colb = jnp.broadcast_to(ac[:, ln:ln + 1], (SSD_CHUNK, SSD_CHUNK))
                    row = at[ln:ln + 1, :]
                    dtr = dt_row[ln:ln + 1, :]
                    seg = jnp.where(mask, jnp.exp(jnp.where(mask, colb - row, 0.0)), 0.0)
                    a_end = row[:, end:end + 1]
                    sc.append(gm * seg * dtr)
                    bt.append(bmt * (jnp.exp(a_end - row) * dtr))
                    es.append(jnp.exp(colb))
                    dec.append(jnp.exp(a_end))
                    xh.append(jnp.where(lo_half if half == 0 else jnp.logical_not(lo_half), xpair, 0.0))
                xst = jnp.concatenate(xh, axis=0)
                hs = h_scr[d, pr]
                y = _bdot(jnp.concatenate(sc, axis=1), xst)
                y = y + _bdot(cm, hs) * jnp.where(lo_half, es[0], es[1])
                y_ref[rows, pr * SSD_PAIR:(pr + 1) * SSD_PAIR] += y
                h_scr[d, pr] = (hs * jnp.where(lo_half, dec[0], dec[1])
                                + _bdot(jnp.concatenate(bt, axis=1), xst))
                yield

    def chunk_step(k, carry):
        _round_robin([chunk_problem((d,), k) for d in range(N_DIR)])
        return carry

    lax.fori_loop(0, nck, chunk_step, 0)
    hf_ref[...] = h_scr[...]


def _ssd(proj, small_t, row0, nseq, seq, width, conv_w, conv_b, dt_bias, a_log, dvec, h0):
    blk0 = row0 // seq
    nck = seq // SSD_CHUNK
    kern = functools.partial(_ssd_kernel, seq=seq, width=width)
    full = lambda *shape: pl.BlockSpec(shape, lambda b: (0,) * len(shape))
    dtb_r = dt_bias.reshape(1, 16)
    dtb_c = dt_bias.reshape(16, 1)
    al_r = a_log.reshape(1, 16)
    al_c = a_log.reshape(16, 1)
    st_spec = pl.BlockSpec((None, N_DIR, SSD_PAIRS, SSD_STATE, SSD_PAIR), lambda b: (b, 0, 0, 0, 0))
    return pl.pallas_call(
        kern,
        grid=(nseq,),
        in_specs=[
            pl.BlockSpec((seq, SSD_CONV_DIM), lambda b: (blk0 + b, COL_XBC // SSD_CONV_DIM)),
            pl.BlockSpec((seq, SMALL_W), lambda b: (blk0 + b, COL_SMALL // SMALL_W)),
            pl.BlockSpec((32, seq), lambda b: (0, blk0 + b)),
            full(3, SSD_CONV_DIM), full(1, SSD_CONV_DIM), full(1, 16), full(16, 1), full(1, 16), full(16, 1),
            full(1, SSD_WIDTH), st_spec,
        ],
        out_specs=[pl.BlockSpec((seq, SSD_WIDTH), lambda b: (b, 0)), st_spec],
        out_shape=[jax.ShapeDtypeStruct((nseq * seq, SSD_WIDTH), F32),
                   jax.ShapeDtypeStruct((nseq, N_DIR, SSD_PAIRS, SSD_STATE, SSD_PAIR), F32)],
        scratch_shapes=[
            pltpu.VMEM((seq, SSD_WIDTH), F32), pltpu.VMEM((seq, 4 * SSD_STATE), F32),
            pltpu.VMEM((seq, 16), F32), pltpu.VMEM((nck, 16, SSD_CHUNK), F32),
            pltpu.VMEM((nck, 16, SSD_CHUNK), F32),
            pltpu.VMEM((N_DIR, SSD_PAIRS, SSD_STATE, SSD_PAIR), F32),
        ],
        compiler_params=_cparams("parallel"),
        name="ssd_seq%d" % seq,
    )(proj, proj, small_t, conv_w, conv_b.reshape(1, -1), dtb_r, dtb_c, al_r, al_c, dvec, h0)


def _ssd_state_to_pairs(h):
    lead = h.shape[:-3]
    t = h.reshape(lead + (SSD_PAIRS, 2, SSD_HEADDIM, SSD_STATE))
    t = jnp.moveaxis(t, -1, -3)
    return t.reshape(lead + (SSD_PAIRS, SSD_STATE, SSD_PAIR))


def _ssd_state_from_pairs(hp):
    lead = hp.shape[:-3]
    t = hp.reshape(lead + (SSD_PAIRS, SSD_STATE, 2, SSD_HEADDIM))
    t = jnp.moveaxis(t, -3, -1)
    return t.reshape(lead + (SSD_HEADS, SSD_HEADDIM, SSD_STATE))


DN_ST = DN_HEADS * DN_CHUNK
DN_PAR = 2


def _dn_kernel(qkv_ref, sm_ref, smt_ref, cw_ref, dtb_r_ref, alog_r_ref, dtb_c_ref, alog_c_ref, s0_ref,
               o_ref, sf_ref, q_scr, k_scr, v_scr, b_scr, g_scr, s_scr, u_scr, wq_scr, a_scr, kd_scr, gl_scr,
               grow_scr, *, seq, width):
    nck = seq // DN_CHUNK
    g_rows = -jnp.exp(alog_c_ref[...]) * _softplus(smt_ref[24:32, :] + dtb_c_ref[...])
    for ck in range(nck):
        grow_scr[ck] = g_rows[:, ck * DN_CHUNK:(ck + 1) * DN_CHUNK]

    def pre(bi, carry):
        r0 = pl.multiple_of(bi * PRE_ROWS, PRE_ROWS)
        rows = pl.ds(r0, PRE_ROWS)
        xc = _silu(_conv_block(qkv_ref, cw_ref, r0, seq, width))
        for h in range(DN_HEADS):
            q = xc[:, h * DN_DK:(h + 1) * DN_DK]
            k = xc[:, DN_QK + h * DN_DK:DN_QK + (h + 1) * DN_DK]
            q_scr[rows, h * DN_DK:(h + 1) * DN_DK] = (
                q * lax.rsqrt(jnp.sum(q * q, axis=-1, keepdims=True) + EPS) * (DN_DK ** -0.5))
            k_scr[rows, h * DN_DK:(h + 1) * DN_DK] = (
                k * lax.rsqrt(jnp.sum(k * k, axis=-1, keepdims=True) + EPS))
        v_scr[rows, :] = xc[:, 2 * DN_QK:]
        b_scr[rows, :] = _sigmoid(sm_ref[rows, 16:24])
        g_scr[rows, :] = -jnp.exp(alog_r_ref[...]) * _softplus(sm_ref[rows, 24:32] + dtb_r_ref[...])
        o_ref[rows, :] = jnp.zeros((PRE_ROWS, DN_V), F32)
        return carry

    lax.fori_loop(0, seq // PRE_ROWS, pre, 0)
    s_scr[...] = s0_ref[...]

    r = lax.broadcasted_iota(jnp.int32, (DN_ST, DN_ST), 0)
    c = lax.broadcasted_iota(jnp.int32, (DN_ST, DN_ST), 1)
    same = (r // DN_CHUNK) == (c // DN_CHUNK)
    eye = (r == c).astype(F32)
    tril64 = _tri(DN_CHUNK, True).astype(BF16)
    triu64 = _tri(DN_CHUNK, False).astype(BF16)
    tj = lax.broadcasted_iota(jnp.int32, (DN_CHUNK, DN_ST), 0)
    ti = lax.broadcasted_iota(jnp.int32, (DN_CHUNK, DN_ST), 1) % DN_CHUNK
    cum_f = (tj <= ti).astype(BF16)
    cum_b = (tj >= ti).astype(BF16)

    def chunk_problem(d, ci):
        r0 = pl.multiple_of(ci * DN_CHUNK, DN_CHUNK)
        rows = pl.ds(r0, DN_CHUNK)
        incl = jnp.logical_and(same, (r >= c) if d == 0 else (r <= c))
        strict = jnp.logical_and(same, (r > c) if d == 0 else (r < c))
        gc_c = _dot_exact_lhs(tril64 if d == 0 else triu64, g_scr[rows, :])
        gc_t = _dot_exact_rhs(grow_scr[ci], cum_f if d == 0 else cum_b)
        gc_r = jnp.concatenate(
            [jnp.broadcast_to(gc_t[d * DN_HEADS + h:d * DN_HEADS + h + 1, :], (DN_CHUNK, DN_ST))
             for h in range(DN_HEADS)], axis=0)
        beta = b_scr[rows, :]
        end_row = DN_CHUNK - 1 if d == 0 else 0
        k_st, q_st, v_st, bt_st, gc_st, gl_st = [], [], [], [], [], []
        for h in range(DN_HEADS):
            ln = d * DN_HEADS + h
            k_st.append(k_scr[rows, h * DN_DK:(h + 1) * DN_DK])
            q_st.append(q_scr[rows, h * DN_DK:(h + 1) * DN_DK])
            v_st.append(v_scr[rows, h * DN_DV:(h + 1) * DN_DV])
            bt_st.append(beta[:, ln:ln + 1])
            col = gc_c[:, ln:ln + 1]
            gc_st.append(col)
            gl_st.append(col[end_row:end_row + 1, :])
        kst = jnp.concatenate(k_st, axis=0)
        qst = jnp.concatenate(q_st, axis=0)
        vst = jnp.concatenate(v_st, axis=0)
        bst = jnp.concatenate(bt_st, axis=0)
        gst = jnp.concatenate(gc_st, axis=0)
        decay = jnp.where(incl, jnp.exp(jnp.where(incl, gst - gc_r, 0.0)), 0.0)
        kb = kst * bst
        kstb = kst.astype(BF16)
        m = jnp.where(strict, _bdot_nt(kb, kstb) * decay, 0.0)
        attn = jnp.where(incl, _bdot_nt(qst, kstb) * decay, 0.0)
        a_scr[d, ci] = attn.astype(BF16)
        yield
        t = eye - m
        pb = m.astype(BF16)
        p = jnp.dot(pb, pb, preferred_element_type=F32)
        yield
        for lvl in range(5):
            pb = p.astype(BF16)
            t = t + jnp.dot(t.astype(BF16), pb, preferred_element_type=F32)
            if lvl < 4:
                p = jnp.dot(pb, pb, preferred_element_type=F32)
            yield
        tb = t.astype(BF16)
        rhs = jnp.concatenate([vst * bst, kb * jnp.exp(gst)], axis=1)
        x0 = jnp.dot(tb, rhs.astype(BF16), preferred_element_type=F32)
        yield
        res = rhs - x0 - _dot3(m, x0)
        yield
        uw = x0 + jnp.dot(tb, res.astype(BF16), preferred_element_type=F32)
        yield
        qg = qst * jnp.exp(gst)
        u_scr[d, ci] = uw[:, :DN_DV]
        kdec = []
        for h in range(DN_HEADS):
            hs = slice(h * DN_CHUNK, (h + 1) * DN_CHUNK)
            wq_scr[d, ci, h] = jnp.concatenate([uw[hs, DN_DV:], qg[hs, :]], axis=0).astype(BF16)
            gl = gl_st[h]
            kdec.append(k_st[h] * jnp.exp(gl - gc_st[h]))
            gl_scr[d, ci, h:h + 1, :] = jnp.broadcast_to(jnp.exp(gl), (1, DN_DV))
        kd_scr[d, ci] = jnp.concatenate(kdec, axis=0).astype(BF16)

    def chunk_step(kk, carry):
        _round_robin([chunk_problem(d, kk * DN_PAR + j) for j in range(DN_PAR) for d in range(N_DIR)])
        return carry

    lax.fori_loop(0, nck // DN_PAR, chunk_step, 0)

    def state_problem(d, ci):
        rows = pl.ds(pl.multiple_of(ci * DN_CHUNK, DN_CHUNK), DN_CHUNK)
        s_old, vnew, qs_all = [], [], []
        for h in range(DN_HEADS):
            hs = slice(h * DN_CHUNK, (h + 1) * DN_CHUNK)
            s_h = s_scr[d, h]
            ws = jnp.dot(wq_scr[d, ci, h], s_h.astype(BF16), preferred_element_type=F32)
            s_old.append(s_h)
            vnew.append((u_scr[d, ci, hs, :] - ws[:DN_CHUNK]).astype(BF16))
            qs_all.append(ws[DN_CHUNK:])
        yield
        o_st = jnp.concatenate(qs_all, axis=0) + jnp.dot(
            a_scr[d, ci], jnp.concatenate(vnew, axis=0), preferred_element_type=F32)
        for h in range(DN_HEADS):
            hs = slice(h * DN_CHUNK, (h + 1) * DN_CHUNK)
            s_scr[d, h] = s_old[h] * gl_scr[d, ci, h:h + 1, :] + lax.dot_general(
                kd_scr[d, ci, hs, :], vnew[h], (((0,), (0,)), ((), ())), preferred_element_type=F32)
        yield
        for h in range(DN_HEADS):
            hs = slice(h * DN_CHUNK, (h + 1) * DN_CHUNK)
            o_ref[rows, h * DN_DV:(h + 1) * DN_DV] += o_st[hs, :]

    def state_step(kk, carry):
        _round_robin([state_problem(0, kk), state_problem(1, nck - 1 - kk)])
        return carry

    lax.fori_loop(0, nck, state_step, 0)
    sf_ref[...] = s_scr[...]


def _dn(proj, small_t, row0, nseq, seq, width, conv_w, dt_bias, a_log, s0):
    blk0 = row0 // seq
    nck = seq // DN_CHUNK
    kern = functools.partial(_dn_kernel, seq=seq, width=width)
    full = lambda *shape: pl.BlockSpec(shape, lambda b: (0,) * len(shape))
    dtb_r = dt_bias.reshape(1, 8)
    al_r = a_log.reshape(1, 8)
    dtb_c = dt_bias.reshape(8, 1)
    al_c = a_log.reshape(8, 1)
    return pl.pallas_call(
        kern,
        grid=(nseq,),
        in_specs=[
            pl.BlockSpec((seq, DN_CONV_DIM), lambda b: (blk0 + b, COL_QKV // DN_CONV_DIM)),
            pl.BlockSpec((seq, SMALL_W), lambda b: (blk0 + b, COL_SMALL // SMALL_W)),
            pl.BlockSpec((32, seq), lambda b: (0, blk0 + b)),
            full(3, DN_CONV_DIM), full(1, 8), full(1, 8), full(8, 1), full(8, 1),
            pl.BlockSpec((None, N_DIR, DN_HEADS, DN_DK, DN_DV), lambda b: (b, 0, 0, 0, 0)),
        ],
        out_specs=[
            pl.BlockSpec((seq, DN_V), lambda b: (b, 0)),
            pl.BlockSpec((None, N_DIR, DN_HEADS, DN_DK, DN_DV), lambda b: (b, 0, 0, 0, 0)),
        ],
        out_shape=[jax.ShapeDtypeStruct((nseq * seq, DN_V), F32),
                   jax.ShapeDtypeStruct((nseq, N_DIR, DN_HEADS, DN_DK, DN_DV), F32)],
        scratch_shapes=[
            pltpu.VMEM((seq, DN_QK), F32), pltpu.VMEM((seq, DN_QK), F32), pltpu.VMEM((seq, DN_V), F32),
            pltpu.VMEM((seq, 8), F32), pltpu.VMEM((seq, 8), F32),
            pltpu.VMEM((N_DIR, DN_HEADS, DN_DK, DN_DV), F32),
            pltpu.VMEM((N_DIR, nck, DN_ST, DN_DV), F32),
            pltpu.VMEM((N_DIR, nck, DN_HEADS, 2 * DN_CHUNK, DN_DK), BF16),
            pltpu.VMEM((N_DIR, nck, DN_ST, DN_ST), BF16),
            pltpu.VMEM((N_DIR, nck, DN_ST, DN_DK), BF16),
            pltpu.VMEM((N_DIR, nck, 8, DN_DV), F32),
            pltpu.VMEM((nck, 8, DN_CHUNK), F32),
        ],
        compiler_params=_cparams("parallel"),
        name="dn_seq%d" % seq,
    )(proj, proj, small_t, conv_w, dtb_r, al_r, dtb_c, al_c, s0)


MRG_TM = 512
MRG_TD = 256


def _merge_kernel(x_ref, sh_ref, sc_ref, gt_ref, ng_ref, y5_ref, u5_ref, d5_ref, ysc_ref, ysl_ref, z_ref,
                  odc_ref, odl_ref, dg_ref, wg0_ref, wg1_ref, wg2_ref, glu_ref, sng_ref, swo_ref, dng_ref,
                  dwo_ref, wo_ref, o_ref, h_scr, a_scr, b_scr, c_scr, acc_scr):
    i = pl.program_id(0)
    j = pl.program_id(1)
    grp = _row_group(i, MRG_TM)

    @pl.when(j == 0)
    def _():
        is_ctx = i < N_CTX // MRG_TM
        h = _norm_mod(x_ref[...], ng_ref[...], sc_ref[pl.ds(grp, 1), :], sh_ref[pl.ds(grp, 1), :])
        h_scr[...] = h.astype(BF16)
        a_scr[...] = jax.nn.gelu(y5_ref[...] + d5_ref[...] * u5_ref[...]).astype(BF16)
        ys = jnp.where(is_ctx, ysc_ref[...], ysl_ref[...]) * _silu(z_ref[...])
        ys = ys * lax.rsqrt(jnp.mean(ys * ys, axis=-1, keepdims=True) + EPS) * sng_ref[...]
        b_scr[...] = ys.astype(BF16)
        od = jnp.where(is_ctx, odc_ref[...], odl_ref[...])
        for hd in range(DN_HEADS):
            cols = slice(hd * DN_DV, (hd + 1) * DN_DV)
            oh = od[:, cols]
            on = oh * lax.rsqrt(jnp.mean(oh * oh, axis=-1, keepdims=True) + EPS) * dng_ref[...]
            c_scr[:, cols] = (on * _silu(dg_ref[:, cols])).astype(BF16)
        acc_scr[...] = jnp.zeros_like(acc_scr)

    h = h_scr[...]
    nt = (((1,), (1,)), ((), ()))
    gate = lambda w_ref: _sigmoid(lax.dot_general(h, w_ref[...], nt, preferred_element_type=F32))
    g5 = a_scr[...]
    br_a = (jnp.dot(g5, glu_ref[0], preferred_element_type=F32)
            * _sigmoid(jnp.dot(g5, glu_ref[1], preferred_element_type=F32)))
    br_b = jnp.dot(b_scr[...], swo_ref[...], preferred_element_type=F32)
    br_c = jnp.dot(c_scr[...], dwo_ref[...], preferred_element_type=F32)
    merged = gate(wg0_ref) * br_a + gate(wg1_ref) * br_b + gate(wg2_ref) * br_c
    acc_scr[...] += jnp.dot(merged.astype(BF16), wo_ref[...], preferred_element_type=F32)

    @pl.when(j == pl.num_programs(1) - 1)
    def _():
        o_ref[...] = x_ref[...] + gt_ref[pl.ds(grp, 1), :] * acc_scr[...]


def _merge(x, mod, norm_g_row, proj, w_packed, y5, s5_d_l, ys_c, ys_l, od_c, od_l, glu_b, ssd_norm_g_l, swo_b,
           dn_norm_g_l, dwo_b, wo_b, layer):
    nctx = N_CTX // MRG_TM
    rowblk = lambda w, col: pl.BlockSpec((MRG_TM, w), lambda i, j: (i, col // w))
    ctxblk = lambda w: pl.BlockSpec((MRG_TM, w), lambda i, j: (jnp.minimum(i, nctx - 1), 0))
    latblk = lambda w: pl.BlockSpec((MRG_TM, w), lambda i, j: (jnp.maximum(i - nctx, 0), 0))
    row1 = lambda w: pl.BlockSpec((1, w), lambda i, j: (0, 0))
    modblk = lambda k: pl.BlockSpec((8, D_MODEL), lambda i, j: (0, k))
    gateblk = lambda k: pl.BlockSpec((None, MRG_TD, D_MODEL),
                                     lambda i, j: (layer, (COL_GATES + k * D_MODEL) // MRG_TD + j, 0))
    return pl.pallas_call(
        _merge_kernel,
        grid=(N_ROWS // MRG_TM, D_MODEL // MRG_TD),
        in_specs=[
            rowblk(D_MODEL, 0), modblk(3), modblk(4), modblk(5), row1(D_MODEL),
            rowblk(S5_WIDTH, 0), rowblk(S5_WIDTH, COL_U), row1(S5_WIDTH),
            ctxblk(SSD_WIDTH), latblk(SSD_WIDTH), rowblk(SSD_WIDTH, COL_Z),
            ctxblk(DN_V), latblk(DN_V), rowblk(DN_V, COL_DNG),
            gateblk(0), gateblk(1), gateblk(2),
            pl.BlockSpec((None, 2, S5_WIDTH, MRG_TD), lambda i, j: (layer, 0, 0, j)),
            row1(SSD_WIDTH),
            pl.BlockSpec((None, SSD_WIDTH, MRG_TD), lambda i, j: (layer, 0, j)),
            row1(DN_DV),
            pl.BlockSpec((None, DN_V, MRG_TD), lambda i, j: (layer, 0, j)),
            pl.BlockSpec((None, MRG_TD, D_MODEL), lambda i, j: (layer, j, 0)),
        ],
        out_specs=rowblk(D_MODEL, 0),
        out_shape=jax.ShapeDtypeStruct((N_ROWS, D_MODEL), F32),
        scratch_shapes=[pltpu.VMEM((MRG_TM, D_MODEL), BF16), pltpu.VMEM((MRG_TM, S5_WIDTH), BF16),
                        pltpu.VMEM((MRG_TM, SSD_WIDTH), BF16), pltpu.VMEM((MRG_TM, DN_V), BF16),
                        pltpu.VMEM((MRG_TM, D_MODEL), F32)],
        compiler_params=_cparams("parallel", "arbitrary"),
        name="merge",
    )(x, mod, mod, mod, norm_g_row, y5, proj, s5_d_l.reshape(1, -1), ys_c, ys_l, proj, od_c, od_l, proj,
      w_packed, w_packed, w_packed, glu_b, ssd_norm_g_l.reshape(1, -1), swo_b, dn_norm_g_l.reshape(1, -1),
      dwo_b, wo_b)


FIN_TM = 1024


def _final_norm_kernel(x_ref, g_ref, oc_ref, ol_ref):
    i = pl.program_id(0)
    x = x_ref[...]
    y = x * lax.rsqrt(jnp.mean(x * x, axis=-1, keepdims=True) + EPS) * g_ref[...]

    @pl.when(i < N_CTX // FIN_TM)
    def _():
        oc_ref[...] = y

    @pl.when(i >= N_CTX // FIN_TM)
    def _():
        ol_ref[...] = y


def _final_norm(x, g):
    nctx = N_CTX // FIN_TM
    return pl.pallas_call(
        _final_norm_kernel,
        grid=(N_ROWS // FIN_TM,),
        in_specs=[pl.BlockSpec((FIN_TM, D_MODEL), lambda i: (i, 0)), pl.BlockSpec((1, D_MODEL), lambda i: (0, 0))],
        out_specs=[pl.BlockSpec((FIN_TM, D_MODEL), lambda i: (jnp.minimum(i, nctx - 1), 0)),
                   pl.BlockSpec((FIN_TM, D_MODEL), lambda i: (jnp.maximum(i - nctx, 0), 0))],
        out_shape=[jax.ShapeDtypeStruct((N_CTX, D_MODEL), F32), jax.ShapeDtypeStruct((N_LAT, D_MODEL), F32)],
        compiler_params=_cparams("arbitrary"),
        name="final_norm",
    )(x, g.reshape(1, -1))


def kernel(x_prompt, x_sample, state_s5_re, state_s5_im, state_ssd, state_dn, c, c_ctx, ada_w, ada_b, norm_g, ffn_wi, ffn_wo, w_in, s5_lam_re, s5_lam_im, s5_log_dt, s5_b_re, s5_b_im, s5_c_re, s5_c_im, s5_d, s5_glu, ssd_conv_w, ssd_conv_b, ssd_dt_bias, ssd_a_log, ssd_d, ssd_norm_g, ssd_w_out, dn_conv_w, dn_dt_bias, dn_a_log, dn_norm_g, dn_w_out, w_out, final_norm_g):
    x = jnp.concatenate([x_prompt.reshape(N_CTX, D_MODEL), x_sample.reshape(N_LAT, D_MODEL)], axis=0)
    cond8 = jnp.concatenate([c_ctx[None, :], c, jnp.zeros((8 - 1 - DEC_BATCH, D_MODEL), F32)], axis=0)
    mods = _ada_mods(cond8, ada_w, ada_b)

    s5_tables = _s5_prep(s5_lam_re, s5_lam_im, s5_log_dt, s5_b_re, s5_b_im, s5_c_re, s5_c_im)
    s5_perm = _s5_perm()
    w_packed = _repack_w_in(w_in)
    glu_b = s5_glu.astype(BF16)
    swo_b = ssd_w_out.astype(BF16)
    dwo_b = dn_w_out.astype(BF16)
    wo_b = w_out.astype(BF16)
    ssd_d_rows = jnp.repeat(ssd_d, SSD_HEADDIM, axis=1).reshape(DEPTH, 1, SSD_WIDTH)

    def s5_h0(state):
        return state.transpose(1, 3, 0, 2, 4).reshape(DEPTH, S5_GROUPS, DEC_BATCH, 2 * S5_STATE)

    h0_re = s5_h0(state_s5_re)
    h0_im = s5_h0(state_s5_im)
    ssd_h0_lat = _ssd_state_to_pairs(jnp.swapaxes(state_ssd, 0, 1))
    ssd_h0_ctx = jnp.zeros((BATCH, N_DIR, SSD_PAIRS, SSD_STATE, SSD_PAIR), F32)
    dn_s0_lat = jnp.swapaxes(state_dn, 0, 1)
    dn_s0_ctx = jnp.zeros((BATCH, N_DIR, DN_HEADS, DN_DK, DN_DV), F32)

    new_s5_re, new_s5_im, new_ssd, new_dn = [], [], [], []
    for l in range(DEPTH):
        mod = mods[l]
        x = _ffn(x, mod, norm_g[l, 0:1], ffn_wi, ffn_wo, l, 0)
        proj, small_t = _inproj(x, mod, norm_g[l, 1:2], w_packed, l)

        y5, f_re, f_im = _s5_scan(proj, s5_perm, s5_tables, h0_re, h0_im, l)
        new_s5_re.append(f_re.reshape(S5_GROUPS, BATCH, N_DIR, S5_STATE).transpose(1, 2, 0, 3))
        new_s5_im.append(f_im.reshape(S5_GROUPS, BATCH, N_DIR, S5_STATE).transpose(1, 2, 0, 3))

        ys_c, hs_c = _ssd(proj, small_t, 0, BATCH, SEQ, SEQ, ssd_conv_w[l], ssd_conv_b[l], ssd_dt_bias[l],
                          ssd_a_log[l], ssd_d_rows[l], ssd_h0_ctx)
        ys_l, _ = _ssd(proj, small_t, N_CTX, DEC_BATCH, DEC_SEQ, GRID_W, ssd_conv_w[l], ssd_conv_b[l],
                       ssd_dt_bias[l], ssd_a_log[l], ssd_d_rows[l], ssd_h0_lat[l])
        new_ssd.append(_ssd_state_from_pairs(hs_c))
        od_c, sd_c = _dn(proj, small_t, 0, BATCH, SEQ, SEQ, dn_conv_w[l], dn_dt_bias[l], dn_a_log[l], dn_s0_ctx)
        od_l, _ = _dn(proj, small_t, N_CTX, DEC_BATCH, DEC_SEQ, GRID_W, dn_conv_w[l], dn_dt_bias[l],
                      dn_a_log[l], dn_s0_lat[l])
        new_dn.append(sd_c)

        x = _merge(x, mod, norm_g[l, 1:2], proj, w_packed, y5, s5_d[l], ys_c, ys_l, od_c, od_l, glu_b,
                   ssd_norm_g[l], swo_b, dn_norm_g[l], dwo_b, wo_b, l)
        x = _ffn(x, mod, norm_g[l, 2:3], ffn_wi, ffn_wo, l, 1)

    y_ctx, y_lat = _final_norm(x, final_norm_g)
    y_prompt = y_ctx.reshape(BATCH, SEQ, D_MODEL)
    y_sample = y_lat.reshape(DEC_BATCH, DEC_SEQ, D_MODEL)
    return (y_prompt, y_sample, jnp.stack(new_s5_re, axis=1), jnp.stack(new_s5_im, axis=1),
            jnp.stack(new_ssd, axis=1), jnp.stack(new_dn, axis=1))
```

```python
import functools

import jax
import jax.numpy as jnp
import numpy as np
from jax import lax
from jax.experimental import pallas as pl
from jax.experimental.pallas import tpu as pltpu

F32 = jnp.float32
BF16 = jnp.bfloat16

D_MODEL = 1024
BATCH = 16
SEQ = 256
DEPTH = 4
DEC_BATCH = 2
DEC_SEQ = 1024
GRID_W = 64
N_DIR = 2
N_ADA = 9
D_FF = 2816
EPS = 1e-6

S5_WIDTH = 512
S5_GROUP = 16
S5_GROUPS = 32
S5_STATE = 64
S5_CHUNK = 16

SSD_WIDTH = 512
SSD_HEADDIM = 64
SSD_HEADS = 8
SSD_GROUPS = 2
SSD_STATE = 64
SSD_CHUNK = 128
SSD_CONV_DIM = 768

DN_HEADS = 4
DN_DK = 128
DN_DV = 128
DN_QK = 512
DN_V = 512
DN_CHUNK = 64
DN_CONV_DIM = 1536

IN_SEGMENTS = (512, 512, 768, 16, 1536, 8, 8, 512, 3072)
IN_SPLITS = tuple(int(s) for s in np.cumsum(IN_SEGMENTS)[:-1])

N_CTX = BATCH * SEQ
N_LAT = DEC_BATCH * DEC_SEQ
N_ROWS = N_CTX + N_LAT

COL_QKV = 0
COL_U = 1536
COL_Z = 2048
COL_DNG = 2560
COL_XBC = 3072
COL_SMALL = 3840
PROJ_W = 4096
COL_GATES = PROJ_W
PACK_W = PROJ_W + 3 * D_MODEL
SMALL_W = 128

VMEM_LIMIT = 56 * 1024 * 1024


def _cparams(*sem):
    return pltpu.CompilerParams(dimension_semantics=sem, vmem_limit_bytes=VMEM_LIMIT)


def _sigmoid(x):
    return 0.5 * (jnp.tanh(0.5 * x) + 1.0)


def _silu(x):
    return x * _sigmoid(x)


def _softplus(x):
    return jnp.maximum(x, 0.0) + jnp.log(1.0 + jnp.exp(-jnp.abs(x)))


def _bdot(a, b):
    return jnp.dot(a.astype(BF16), b.astype(BF16), preferred_element_type=F32)


def _bdot_nt(a, b):
    return lax.dot_general(a.astype(BF16), b.astype(BF16), (((1,), (1,)), ((), ())),
                           preferred_element_type=F32)


def _bdot_tn(a, b):
    return lax.dot_general(a.astype(BF16), b.astype(BF16), (((0,), (0,)), ((), ())),
                           preferred_element_type=F32)


def _split3(a):
    hi = a.astype(BF16)
    r = a - hi.astype(F32)
    mid = r.astype(BF16)
    lo = (r - mid.astype(F32)).astype(BF16)
    return hi, mid, lo


def _dot3(a, b):
    ah = a.astype(BF16)
    al = (a - ah.astype(F32)).astype(BF16)
    bh = b.astype(BF16)
    bl = (b - bh.astype(F32)).astype(BF16)
    out = jnp.dot(ah, bh, preferred_element_type=F32)
    out = out + jnp.dot(ah, bl, preferred_element_type=F32)
    out = out + jnp.dot(al, bh, preferred_element_type=F32)
    return out


def _dot_exact_lhs(t_bf16, x):
    hi, mid, lo = _split3(x)
    out = jnp.dot(t_bf16, hi, preferred_element_type=F32)
    out = out + jnp.dot(t_bf16, mid, preferred_element_type=F32)
    out = out + jnp.dot(t_bf16, lo, preferred_element_type=F32)
    return out


def _dot_exact_rhs(x, t_bf16):
    hi, mid, lo = _split3(x)
    out = jnp.dot(hi, t_bf16, preferred_element_type=F32)
    out = out + jnp.dot(mid, t_bf16, preferred_element_type=F32)
    out = out + jnp.dot(lo, t_bf16, preferred_element_type=F32)
    return out


def _norm_mod(x, g, sc, sh):
    ms = jnp.mean(x * x, axis=-1, keepdims=True)
    y = x * lax.rsqrt(ms + EPS) * g
    return y * (1.0 + sc) + sh


def _row_group(i, tm):
    nctx = N_CTX // tm
    per = DEC_SEQ // tm
    return jnp.where(i < nctx, 0, 1 + jnp.maximum(i - nctx, 0) // per)


ADA_TN = 1152


def _ada_kernel(c_ref, w_ref, b_ref, o_ref):
    c = c_ref[...]
    o_ref[...] = _bdot(_silu(c), w_ref[...]) + b_ref[...]


def _ada_mods(cond8, ada_w, ada_b):
    nj = (N_ADA * D_MODEL) // ADA_TN
    return pl.pallas_call(
        _ada_kernel,
        grid=(DEPTH, nj),
        in_specs=[
            pl.BlockSpec((8, D_MODEL), lambda l, j: (0, 0)),
            pl.BlockSpec((None, D_MODEL, ADA_TN), lambda l, j: (l, 0, j)),
            pl.BlockSpec((None, 1, ADA_TN), lambda l, j: (l, 0, j)),
        ],
        out_specs=pl.BlockSpec((None, 8, ADA_TN), lambda l, j: (l, 0, j)),
        out_shape=jax.ShapeDtypeStruct((DEPTH, 8, N_ADA * D_MODEL), F32),
        compiler_params=_cparams("parallel", "parallel"),
        name="ada_mods",
    )(cond8, ada_w, ada_b.reshape(DEPTH, 1, N_ADA * D_MODEL))


FFN_TM = 2048
FFN_TF = 256
MOD_ROWS = 1024
FFN_SUB = FFN_TM // MOD_ROWS


def _ffn_kernel(x_ref, sh_ref, sc_ref, gt_ref, g_ref, wa_ref, wb_ref, wo_ref, o_ref, h_scr, acc_scr):
    i = pl.program_id(0)
    j = pl.program_id(1)

    @pl.when(j == 0)
    def _():
        for s in range(FFN_SUB):
            rows = slice(s * MOD_ROWS, (s + 1) * MOD_ROWS)
            grp = _row_group(i * FFN_SUB + s, MOD_ROWS)
            h = _norm_mod(x_ref[rows, :], g_ref[...], sc_ref[pl.ds(grp, 1), :], sh_ref[pl.ds(grp, 1), :])
            h_scr[rows, :] = h.astype(BF16)
        acc_scr[...] = jnp.zeros_like(acc_scr)

    h = h_scr[...]
    a = jnp.dot(h, wa_ref[...].astype(BF16), preferred_element_type=F32)
    b = jnp.dot(h, wb_ref[...].astype(BF16), preferred_element_type=F32)
    u = (_silu(a) * b).astype(BF16)
    acc_scr[...] += jnp.dot(u, wo_ref[...].astype(BF16), preferred_element_type=F32)

    @pl.when(j == pl.num_programs(1) - 1)
    def _():
        for s in range(FFN_SUB):
            rows = slice(s * MOD_ROWS, (s + 1) * MOD_ROWS)
            grp = _row_group(i * FFN_SUB + s, MOD_ROWS)
            o_ref[rows, :] = x_ref[rows, :] + (0.5 * gt_ref[pl.ds(grp, 1), :]) * acc_scr[rows, :]


def _ffn(x, mod, norm_g_row, ffn_wi, ffn_wo, layer, which):
    nf = D_FF // FFN_TF
    base = 0 if which == 0 else 6
    return pl.pallas_call(
        _ffn_kernel,
        grid=(N_ROWS // FFN_TM, nf),
        in_specs=[
            pl.BlockSpec((FFN_TM, D_MODEL), lambda i, j: (i, 0)),
            pl.BlockSpec((8, D_MODEL), lambda i, j: (0, base)),
            pl.BlockSpec((8, D_MODEL), lambda i, j: (0, base + 1)),
            pl.BlockSpec((8, D_MODEL), lambda i, j: (0, base + 2)),
            pl.BlockSpec((1, D_MODEL), lambda i, j: (0, 0)),
            pl.BlockSpec((None, None, D_MODEL, FFN_TF), lambda i, j: (layer, which, 0, j)),
            pl.BlockSpec((None, None, D_MODEL, FFN_TF), lambda i, j: (layer, which, 0, j + nf)),
            pl.BlockSpec((None, None, FFN_TF, D_MODEL), lambda i, j: (layer, which, j, 0)),
        ],
        out_specs=pl.BlockSpec((FFN_TM, D_MODEL), lambda i, j: (i, 0)),
        out_shape=jax.ShapeDtypeStruct((N_ROWS, D_MODEL), F32),
        scratch_shapes=[pltpu.VMEM((FFN_TM, D_MODEL), BF16), pltpu.VMEM((FFN_TM, D_MODEL), F32)],
        compiler_params=_cparams("parallel", "arbitrary"),
        name="ffn",
    )(x, mod, mod, mod, norm_g_row, ffn_wi, ffn_wi, ffn_wo)


INP_TM = 512


def _inproj_kernel(x_ref, sh_ref, sc_ref, g_ref, w_ref, o_ref, ot_ref):
    grp = _row_group(pl.program_id(0), INP_TM)
    h = _norm_mod(x_ref[...], g_ref[...], sc_ref[pl.ds(grp, 1), :], sh_ref[pl.ds(grp, 1), :])
    res = lax.dot_general(h.astype(BF16), w_ref[...], (((1,), (1,)), ((), ())), preferred_element_type=F32)
    o_ref[...] = res
    ot_ref[...] = res[:, COL_SMALL:COL_SMALL + SMALL_W].T


RPK_LANES = 256
RPK_MOVES = ((COL_QKV, 1808, 1536), (COL_U, 0, 512), (COL_Z, 512, 512), (COL_DNG, 3360, 512),
             (COL_XBC, 1024, 768), (COL_SMALL, 1792, 16), (COL_SMALL + 16, 3344, 16), (COL_GATES, 3872, 3072))
RPK_PAD = (COL_SMALL + 32, PROJ_W)


def _repack_kernel(w_ref, o_ref):
    for dst, src, n in RPK_MOVES:
        o_ref[dst:dst + n, :] = w_ref[src:src + n, :].astype(BF16)
    o_ref[RPK_PAD[0]:RPK_PAD[1], :] = jnp.zeros((RPK_PAD[1] - RPK_PAD[0], RPK_LANES), BF16)


def _repack_w_in(w_in):
    w_t = jnp.swapaxes(w_in, 1, 2)
    in_w = w_t.shape[1]
    return pl.pallas_call(
        _repack_kernel,
        grid=(DEPTH, D_MODEL // RPK_LANES),
        in_specs=[pl.BlockSpec((None, in_w, RPK_LANES), lambda l, i: (l, 0, i))],
        out_specs=pl.BlockSpec((None, PACK_W, RPK_LANES), lambda l, i: (l, 0, i)),
        out_shape=jax.ShapeDtypeStruct((DEPTH, PACK_W, D_MODEL), BF16),
        compiler_params=_cparams("parallel", "parallel"),
        name="repack_w_in",
    )(w_t)


def _inproj(x, mod, norm_g_row, w_packed, layer):
    return pl.pallas_call(
        _inproj_kernel,
        grid=(N_ROWS // INP_TM,),
        in_specs=[
            pl.BlockSpec((INP_TM, D_MODEL), lambda i: (i, 0)),
            pl.BlockSpec((8, D_MODEL), lambda i: (0, 3)),
            pl.BlockSpec((8, D_MODEL), lambda i: (0, 4)),
            pl.BlockSpec((1, D_MODEL), lambda i: (0, 0)),
            pl.BlockSpec((None, PROJ_W, D_MODEL), lambda i: (layer, 0, 0)),
        ],
        out_specs=[pl.BlockSpec((INP_TM, PROJ_W), lambda i: (i, 0)),
                   pl.BlockSpec((SMALL_W, INP_TM), lambda i: (0, i))],
        out_shape=[jax.ShapeDtypeStruct((N_ROWS, PROJ_W), F32),
                   jax.ShapeDtypeStruct((SMALL_W, N_ROWS), F32)],
        compiler_params=_cparams("parallel"),
        name="inproj",
    )(x, mod, mod, norm_g_row, w_packed)


S5_ROW = S5_CHUNK * S5_GROUP


def _s5_prep_kernel(lam_re_ref, lam_im_ref, ldt_ref, btr_ref, bti_ref, cr_ref, ci_ref, ctr_ref, cti_ref,
                    m_ref, pre_ref, pim_ref, qre_ref, qim_ref, are_ref, aim_ref):
    tau = lax.broadcasted_iota(jnp.int32, (S5_CHUNK, 1), 0).astype(F32)
    lane = lax.broadcasted_iota(jnp.int32, (S5_GROUP, S5_ROW), 1)
    lane_t = lane[0:1, :] // S5_GROUP
    rep = (lax.broadcasted_iota(jnp.int32, (S5_GROUP, S5_ROW), 0) == lane % S5_GROUP).astype(BF16)
    taps = [None, None]

    def outer(ar, ai, xr, xi):
        rr = ar[:, None, :] * xr[None, :, :] - ai[:, None, :] * xi[None, :, :]
        ii = ar[:, None, :] * xi[None, :, :] + ai[:, None, :] * xr[None, :, :]
        return rr.reshape(S5_ROW, S5_STATE), ii.reshape(S5_ROW, S5_STATE)

    def direction(d):
        lr = lam_re_ref[d:d + 1, :]
        li = lam_im_ref[d:d + 1, :]
        dt = jnp.exp(ldt_ref[d:d + 1, :])
        mag = jnp.exp(lr * dt)
        lb_re = mag * jnp.cos(li * dt)
        lb_im = mag * jnp.sin(li * dt)
        den = lr * lr + li * li
        cr = ((lb_re - 1.0) * lr + lb_im * li) / den
        ci = (lb_im * lr - (lb_re - 1.0) * li) / den
        bt_r = btr_ref[d]
        bt_i = bti_ref[d]
        bbt_r = cr * bt_r - ci * bt_i
        bbt_i = cr * bt_i + ci * bt_r
        c_r = cr_ref[d]
        c_i = ci_ref[d]

        def powtab(t):
            m = jnp.exp(t * (lr * dt))
            ang = t * (li * dt)
            return m * jnp.cos(ang), m * jnp.sin(ang)

        t_in = (S5_CHUNK - 1) - tau if d == 0 else tau
        ar, ai = powtab(t_in)
        ba_r, ba_i = outer(ar, ai, bbt_r, bbt_i)
        lanes = slice(d * S5_STATE, (d + 1) * S5_STATE)
        pre_ref[:, lanes] = ba_r.astype(BF16)
        pim_ref[:, lanes] = ba_i.astype(BF16)
        yield
        kt = _dot3(ba_r, ctr_ref[d]) - _dot3(ba_i, cti_ref[d])
        yield
        taps[d] = _dot_exact_rhs(kt, rep)
        yield
        t_out = tau + 1.0 if d == 0 else S5_CHUNK - tau
        ar, ai = powtab(t_out)
        qr, qi = outer(ar, ai, c_r, c_i)
        qre_ref[:, lanes] = qr.astype(BF16)
        qim_ref[:, lanes] = (-qi).astype(BF16)
        a16r, a16i = powtab(jnp.full((1, 1), float(S5_CHUNK), F32))
        are_ref[:, lanes] = a16r
        aim_ref[:, lanes] = a16i

    _round_robin([direction(d) for d in range(N_DIR)])

    last = S5_ROW - S5_GROUP
    table = jnp.concatenate([taps[0][:last, :], taps[0][last:, :] + taps[1][:S5_GROUP, :], taps[1][S5_GROUP:, :]],
                            axis=0)
    mmat = table[last:last + S5_ROW, :]
    for t in range(1, S5_CHUNK):
        start = (S5_CHUNK - 1 - t) * S5_GROUP
        mmat = jnp.where(lane_t == t, table[start:start + S5_ROW, :], mmat)
    m_ref[...] = mmat.astype(BF16)


def _s5_prep(lam_re, lam_im, log_dt, b_re, b_im, c_re, c_im):
    tg = lambda t: jnp.swapaxes(t, 1, 2)
    lam_re_g = tg(lam_re)
    lam_im_g = tg(lam_im)
    ldt_g = tg(log_dt)[..., None]
    bt_r = jnp.swapaxes(tg(b_re), -1, -2)
    bt_i = jnp.swapaxes(tg(b_im), -1, -2)
    c_r = tg(c_re)
    c_i = tg(c_im)
    ct_r = jnp.swapaxes(c_r, -1, -2)
    ct_i = jnp.swapaxes(c_i, -1, -2)

    def spec(*tail):
        n = len(tail)
        return pl.BlockSpec((None, None) + tail, lambda l, g: (l, g) + (0,) * n)

    st = 2 * S5_STATE
    tab = jax.ShapeDtypeStruct((DEPTH, S5_GROUPS, S5_ROW, st), BF16)
    dec = jax.ShapeDtypeStruct((DEPTH, S5_GROUPS, 1, st), F32)
    return pl.pallas_call(
        _s5_prep_kernel,
        grid=(DEPTH, S5_GROUPS),
        in_specs=[spec(2, 64), spec(2, 64), spec(2, 1), spec(2, 16, 64), spec(2, 16, 64),
                  spec(2, 16, 64), spec(2, 16, 64), spec(2, 64, S5_GROUP), spec(2, 64, S5_GROUP)],
        out_specs=[spec(S5_ROW, S5_ROW), spec(S5_ROW, st), spec(S5_ROW, st), spec(S5_ROW, st), spec(S5_ROW, st),
                   spec(1, st), spec(1, st)],
        out_shape=[jax.ShapeDtypeStruct((DEPTH, S5_GROUPS, S5_ROW, S5_ROW), BF16), tab, tab, tab, tab, dec, dec],
        compiler_params=_cparams("parallel", "parallel"),
        name="s5_prep",
    )(lam_re_g, lam_im_g, ldt_g, bt_r, bt_i, c_r, c_i, ct_r, ct_i)


S5_CTX_CH = SEQ // S5_CHUNK
S5_LAT_CH = DEC_SEQ // S5_CHUNK
S5_CTX_ROWS = S5_CTX_CH * BATCH
S5_LAT_ROWS = S5_LAT_CH * DEC_BATCH
S5_ROWS = S5_CTX_ROWS + S5_LAT_ROWS
S5_GB = 128 // S5_GROUP
S5_PERM = S5_GB * 128


def _s5_perm():
    src = np.arange(S5_PERM)
    s, g, j = src // 128, (src % 128) // S5_GROUP, src % S5_GROUP
    p = np.zeros((S5_PERM, S5_PERM), np.float32)
    p[src, g * 128 + s * S5_GROUP + j] = 1.0
    return jnp.asarray(p, BF16)


def _s5_kernel(u_ref, perm_ref, m_ref, pre_ref, pim_ref, qre_ref, qim_ref, are_ref, aim_ref, h0r_ref, h0i_ref,
               y_ref, fr_ref, fi_ref, ug, sre, sim, hfr, hfi, hbr, hbi, ys):
    perm = perm_ref[...]
    half_w = S5_ROW // 2
    for half in range(2):
        x = jnp.concatenate([u_ref[pl.ds(half * 8 + s, S5_ROWS, stride=S5_CHUNK), :] for s in range(8)], axis=1)
        z = jnp.dot(x.astype(BF16), perm, preferred_element_type=F32)
        for g in range(S5_GB):
            ug[g, :, half * half_w:(half + 1) * half_w] = z[:, g * 128:(g + 1) * 128].astype(BF16)
    for g in range(S5_GB):
        sre[g] = jnp.dot(ug[g], pre_ref[g].astype(BF16), preferred_element_type=F32)
        sim[g] = jnp.dot(ug[g], pim_ref[g].astype(BF16), preferred_element_type=F32)
    fwd = lax.broadcasted_iota(jnp.int32, (1, 2 * S5_STATE), 1) < S5_STATE

    def scan(base, nchunk, nseq, init):
        def step(k, hs):
            rf = pl.ds(base + k, nseq, stride=nchunk)
            rb = pl.ds(base + nchunk - 1 - k, nseq, stride=nchunk)
            out = []
            for g in range(S5_GB):
                h_re, h_im = hs[2 * g], hs[2 * g + 1]
                hfr.at[g][rf, :] = h_re
                hfi.at[g][rf, :] = h_im
                hbr.at[g][rb, :] = h_re
                hbi.at[g][rb, :] = h_im
                s_r = jnp.where(fwd, sre.at[g][rf, :], sre.at[g][rb, :])
                s_i = jnp.where(fwd, sim.at[g][rf, :], sim.at[g][rb, :])
                ar = are_ref[g]
                ai = aim_ref[g]
                out.append(ar * h_re - ai * h_im + s_r)
                out.append(ar * h_im + ai * h_re + s_i)
            return tuple(out)

        return lax.fori_loop(0, nchunk, step, init)

    zero = jnp.zeros((BATCH, 2 * S5_STATE), F32)
    fin = scan(0, S5_CTX_CH, BATCH, (zero,) * (2 * S5_GB))
    lat0 = []
    for g in range(S5_GB):
        fr_ref[g] = fin[2 * g]
        fi_ref[g] = fin[2 * g + 1]
        lat0 += [h0r_ref[g], h0i_ref[g]]
    scan(S5_CTX_ROWS, S5_LAT_CH, DEC_BATCH, tuple(lat0))

    for g in range(S5_GB):
        h_re = jnp.where(fwd, hfr[g], hbr[g])
        h_im = jnp.where(fwd, hfi[g], hbi[g])
        y = jnp.dot(ug[g], m_ref[g].astype(BF16), preferred_element_type=F32)
        ys[g] = y + _bdot_nt(h_re, qre_ref[g]) + _bdot_nt(h_im, qim_ref[g])
    for half in range(2):
        w = jnp.concatenate([ys[g, :, half * half_w:(half + 1) * half_w] for g in range(S5_GB)], axis=1)
        w_hi = w.astype(BF16)
        w_lo = (w - w_hi.astype(F32)).astype(BF16)
        zo = sum(lax.dot_general(piece, perm, (((1,), (1,)), ((), ())), preferred_element_type=F32)
                 for piece in (w_hi, w_lo))
        for t in range(8):
            y_ref[pl.ds(half * 8 + t, S5_ROWS, stride=S5_CHUNK), :] = zo[:, t * 128:(t + 1) * 128]


def _s5_scan(proj, perm, tables, h0_re, h0_im, layer):
    def lspec(*tail):
        n = len(tail)
        return pl.BlockSpec((None, S5_GB) + tail, lambda t: (layer, t) + (0,) * n)

    def gspec(*tail):
        n = len(tail)
        return pl.BlockSpec((S5_GB,) + tail, lambda t: (t,) + (0,) * n)

    st = 2 * S5_STATE
    return pl.pallas_call(
        _s5_kernel,
        grid=(S5_GROUPS // S5_GB,),
        in_specs=[pl.BlockSpec((N_ROWS, 128), lambda t: (0, COL_U // 128 + t)),
                  pl.BlockSpec((S5_PERM, S5_PERM), lambda t: (0, 0)),
                  lspec(S5_ROW, S5_ROW), lspec(S5_ROW, st), lspec(S5_ROW, st), lspec(S5_ROW, st),
                  lspec(S5_ROW, st), lspec(1, st), lspec(1, st), lspec(DEC_BATCH, st), lspec(DEC_BATCH, st)],
        out_specs=[pl.BlockSpec((N_ROWS, 128), lambda t: (0, t)), gspec(BATCH, st), gspec(BATCH, st)],
        out_shape=[jax.ShapeDtypeStruct((N_ROWS, S5_WIDTH), F32),
                   jax.ShapeDtypeStruct((S5_GROUPS, BATCH, st), F32),
                   jax.ShapeDtypeStruct((S5_GROUPS, BATCH, st), F32)],
        scratch_shapes=([pltpu.VMEM((S5_GB, S5_ROWS, S5_ROW), BF16)]
                        + [pltpu.VMEM((S5_GB, S5_ROWS, st), F32) for _ in range(6)]
                        + [pltpu.VMEM((S5_GB, S5_ROWS, S5_ROW), F32)]),
        compiler_params=_cparams("parallel"),
        name="s5_scan",
    )(proj, perm, *tables, h0_re, h0_im)


PRE_ROWS = 128


def _conv_block(x_ref, w_ref, r0, seq, width):
    x = x_ref[pl.ds(r0, PRE_ROWS), :]
    prev = x_ref[pl.ds(jnp.maximum(r0 - 1, 0), 1), :]
    nxt = x_ref[pl.ds(jnp.minimum(r0 + PRE_ROWS, seq - 1), 1), :]
    rid = lax.broadcasted_iota(jnp.int32, (PRE_ROWS, 1), 0)
    pos = (r0 + rid) % width
    xm = jnp.where(rid == 0, prev, pltpu.roll(x, 1, 0))
    xm = jnp.where(pos == 0, 0.0, xm)
    xp = jnp.where(rid == PRE_ROWS - 1, nxt, pltpu.roll(x, PRE_ROWS - 1, 0))
    xp = jnp.where(pos == width - 1, 0.0, xp)
    return xm * w_ref[0:1, :] + x * w_ref[1:2, :] + xp * w_ref[2:3, :]


def _round_robin(problems):
    live = list(problems)
    while live:
        nxt = []
        for p in live:
            try:
                next(p)
                nxt.append(p)
            except StopIteration:
                pass
        live = nxt


def _tri(n, lower):
    r = lax.broadcasted_iota(jnp.int32, (n, n), 0)
    c = lax.broadcasted_iota(jnp.int32, (n, n), 1)
    return (r >= c) if lower else (r <= c)


SSD_PAIR = 2 * SSD_HEADDIM
SSD_PAIRS = SSD_HEADS // 2


def _ssd_kernel(xbc_ref, sm_ref, smt_ref, z_ref, cw_ref, cb_ref, dtb_r_ref, dtb_c_ref, alog_r_ref, alog_c_ref,
                dvec_ref, ng_ref, h0_ref, yn_ref, hf_ref, xs_scr, bc_scr, dac_scr, dar_scr, dtr_scr, h_scr, y_ref,
                *, seq, width):
    nck = seq // SSD_CHUNK

    def pre(bi, carry):
        r0 = pl.multiple_of(bi * PRE_ROWS, PRE_ROWS)
        rows = pl.ds(r0, PRE_ROWS)
        xc = _silu(_conv_block(xbc_ref, cw_ref, r0, seq, width) + cb_ref[...])
        xs = xc[:, :SSD_WIDTH]
        xs_scr[rows, :] = xs
        bc_scr[rows, :] = xc[:, SSD_WIDTH:]
        y_ref[rows, :] = dvec_ref[...] * xs
        dt_c = _softplus(sm_ref[rows, 0:16] + dtb_r_ref[...])
        dac_scr[rows, :] = dt_c * (-jnp.exp(alog_r_ref[...]))
        return carry

    lax.fori_loop(0, seq // PRE_ROWS, pre, 0)
    dt_r = _softplus(smt_ref[0:16, :] + dtb_c_ref[...])
    da_r = dt_r * (-jnp.exp(alog_c_ref[...]))
    for ck in range(nck):
        dtr_scr[ck] = dt_r[:, ck * SSD_CHUNK:(ck + 1) * SSD_CHUNK]
        dar_scr[ck] = da_r[:, ck * SSD_CHUNK:(ck + 1) * SSD_CHUNK]
    h_scr[...] = h0_ref[...]

    tril = _tri(SSD_CHUNK, True)
    triu = _tri(SSD_CHUNK, False)
    tril_b = tril.astype(BF16)
    triu_b = triu.astype(BF16)
    lo_half = lax.broadcasted_iota(jnp.int32, (1, SSD_PAIR), 1) < SSD_HEADDIM

    def chunk_problem(dirs, k):
        for d in dirs:
            c = k if d == 0 else nck - 1 - k
            r0 = pl.multiple_of(c * SSD_CHUNK, SSD_CHUNK)
            rows = pl.ds(r0, SSD_CHUNK)
            mask = tril if d == 0 else triu
            ac = _dot_exact_lhs(tril_b if d == 0 else triu_b, dac_scr[rows, :])
            at = _dot_exact_rhs(dar_scr[c], triu_b if d == 0 else tril_b)
            dt_row = dtr_scr[c]
            end = SSD_CHUNK - 1 if d == 0 else 0
            bcx = bc_scr[rows, :]
            gmat = []
            for g in range(SSD_GROUPS):
                bm = bcx[:, g * SSD_STATE:(g + 1) * SSD_STATE]
                cm = bcx[:, 2 * SSD_STATE + g * SSD_STATE:2 * SSD_STATE + (g + 1) * SSD_STATE]
                gmat.append((bm.T, cm, _bdot_nt(cm, bm)))
            yield
            for pr in range(SSD_PAIRS):
                bmt, cm, gm = gmat[pr // (SSD_PAIRS // SSD_GROUPS)]
                xpair = xs_scr[rows, pr * SSD_PAIR:(pr + 1) * SSD_PAIR]
                sc, bt, es, dec, xh = [], [], [], [], []
                for half in range(2):
                    ln = d * SSD_HEADS + 2 * pr + half
                    colb = jnp.broadcast_to(ac[:, ln:ln + 1], (SSD_CHUNK, SSD_CHUNK))
                    row = at[ln:ln + 1, :]
                    dtr = dt_row[ln:ln + 1, :]
                    seg = jnp.where(mask, jnp.exp(jnp.where(mask, colb - row, 0.0)), 0.0)
                    a_end = row[:, end:end + 1]
                    sc.append(gm * seg * dtr)
                    bt.append(bmt * (jnp.exp(a_end - row) * dtr))
                    es.append(jnp.exp(colb))
                    dec.append(jnp.exp(a_end))
                    xh.append(jnp.where(lo_half if half == 0 else jnp.logical_not(lo_half), xpair, 0.0))
                xst = jnp.concatenate(xh, axis=0)
                hs = h_scr[d, pr]
                y = _bdot(jnp.concatenate(sc, axis=1), xst)
                y = y + _bdot(cm, hs) * jnp.where(lo_half, es[0], es[1])
                y_ref[rows, pr * SSD_PAIR:(pr + 1) * SSD_PAIR] += y
                h_scr[d, pr] = (hs * jnp.where(lo_half, dec[0], dec[1])
                                + _bdot(jnp.concatenate(bt, axis=1), xst))
                yield

    def chunk_step(k, carry):
        _round_robin([chunk_problem((d,), k) for d in range(N_DIR)])
        return carry

    lax.fori_loop(0, nck, chunk_step, 0)
    hf_ref[...] = h_scr[...]

    def post(bi, carry):
        rows = pl.ds(pl.multiple_of(bi * PRE_ROWS, PRE_ROWS), PRE_ROWS)
        y = y_ref[rows, :] * _silu(z_ref[rows, :])
        y = y * lax.rsqrt(jnp.mean(y * y, axis=-1, keepdims=True) + EPS) * ng_ref[...]
        yn_ref[rows, :] = y.astype(BF16)
        return carry

    lax.fori_loop(0, seq // PRE_ROWS, post, 0)


def _ssd(proj, small_t, row0, nseq, seq, width, conv_w, conv_b, dt_bias, a_log, dvec, norm_g, h0):
    blk0 = row0 // seq
    nck = seq // SSD_CHUNK
    kern = functools.partial(_ssd_kernel, seq=seq, width=width)
    full = lambda *shape: pl.BlockSpec(shape, lambda b: (0,) * len(shape))
    dtb_r = dt_bias.reshape(1, 16)
    dtb_c = dt_bias.reshape(16, 1)
    al_r = a_log.reshape(1, 16)
    al_c = a_log.reshape(16, 1)
    st_spec = pl.BlockSpec((None, N_DIR, SSD_PAIRS, SSD_STATE, SSD_PAIR), lambda b: (b, 0, 0, 0, 0))
    return pl.pallas_call(
        kern,
        grid=(nseq,),
        in_specs=[
            pl.BlockSpec((seq, SSD_CONV_DIM), lambda b: (blk0 + b, COL_XBC // SSD_CONV_DIM)),
            pl.BlockSpec((seq, SMALL_W), lambda b: (blk0 + b, COL_SMALL // SMALL_W)),
            pl.BlockSpec((32, seq), lambda b: (0, blk0 + b)),
            pl.BlockSpec((seq, SSD_WIDTH), lambda b: (blk0 + b, COL_Z // SSD_WIDTH)),
            full(3, SSD_CONV_DIM), full(1, SSD_CONV_DIM), full(1, 16), full(16, 1), full(1, 16), full(16, 1),
            full(1, SSD_WIDTH), full(1, SSD_WIDTH), st_spec,
        ],
        out_specs=[pl.BlockSpec((seq, SSD_WIDTH), lambda b: (b, 0)), st_spec],
        out_shape=[jax.ShapeDtypeStruct((nseq * seq, SSD_WIDTH), BF16),
                   jax.ShapeDtypeStruct((nseq, N_DIR, SSD_PAIRS, SSD_STATE, SSD_PAIR), F32)],
        scratch_shapes=[
            pltpu.VMEM((seq, SSD_WIDTH), F32), pltpu.VMEM((seq, 4 * SSD_STATE), F32),
            pltpu.VMEM((seq, 16), F32), pltpu.VMEM((nck, 16, SSD_CHUNK), F32),
            pltpu.VMEM((nck, 16, SSD_CHUNK), F32),
            pltpu.VMEM((N_DIR, SSD_PAIRS, SSD_STATE, SSD_PAIR), F32),
            pltpu.VMEM((seq, SSD_WIDTH), F32),
        ],
        compiler_params=_cparams("parallel"),
        name="ssd_seq%d" % seq,
    )(proj, proj, small_t, proj, conv_w, conv_b.reshape(1, -1), dtb_r, dtb_c, al_r, al_c, dvec,
      norm_g.reshape(1, -1), h0)


def _ssd_state_to_pairs(h):
    lead = h.shape[:-3]
    t = h.reshape(lead + (SSD_PAIRS, 2, SSD_HEADDIM, SSD_STATE))
    t = jnp.moveaxis(t, -1, -3)
    return t.reshape(lead + (SSD_PAIRS, SSD_STATE, SSD_PAIR))


def _ssd_state_from_pairs(hp):
    lead = hp.shape[:-3]
    t = hp.reshape(lead + (SSD_PAIRS, SSD_STATE, 2, SSD_HEADDIM))
    t = jnp.moveaxis(t, -3, -1)
    return t.reshape(lead + (SSD_HEADS, SSD_HEADDIM, SSD_STATE))


DN_ST = DN_HEADS * DN_CHUNK
DN_PAR = 2


def _dn_kernel(qkv_ref, sm_ref, smt_ref, dg_ref, cw_ref, dtb_r_ref, alog_r_ref, dtb_c_ref, alog_c_ref, ng_ref,
               s0_ref, on_ref, sf_ref, q_scr, k_scr, v_scr, b_scr, g_scr, s_scr, u_scr, wq_scr, a_scr, kd_scr,
               gl_scr, grow_scr, o_ref, *, seq, width):
    nck = seq // DN_CHUNK
    g_rows = -jnp.exp(alog_c_ref[...]) * _softplus(smt_ref[24:32, :] + dtb_c_ref[...])
    for ck in range(nck):
        grow_scr[ck] = g_rows[:, ck * DN_CHUNK:(ck + 1) * DN_CHUNK]

    def pre(bi, carry):
        r0 = pl.multiple_of(bi * PRE_ROWS, PRE_ROWS)
        rows = pl.ds(r0, PRE_ROWS)
        xc = _silu(_conv_block(qkv_ref, cw_ref, r0, seq, width))
        for h in range(DN_HEADS):
            q = xc[:, h * DN_DK:(h + 1) * DN_DK]
            k = xc[:, DN_QK + h * DN_DK:DN_QK + (h + 1) * DN_DK]
            q_scr[rows, h * DN_DK:(h + 1) * DN_DK] = (
                q * lax.rsqrt(jnp.sum(q * q, axis=-1, keepdims=True) + EPS) * (DN_DK ** -0.5))
            k_scr[rows, h * DN_DK:(h + 1) * DN_DK] = (
                k * lax.rsqrt(jnp.sum(k * k, axis=-1, keepdims=True) + EPS))
        v_scr[rows, :] = xc[:, 2 * DN_QK:]
        b_scr[rows, :] = _sigmoid(sm_ref[rows, 16:24])
        g_scr[rows, :] = -jnp.exp(alog_r_ref[...]) * _softplus(sm_ref[rows, 24:32] + dtb_r_ref[...])
        o_ref[rows, :] = jnp.zeros((PRE_ROWS, DN_V), F32)
        return carry

    lax.fori_loop(0, seq // PRE_ROWS, pre, 0)
    s_scr[...] = s0_ref[...]

    r = lax.broadcasted_iota(jnp.int32, (DN_ST, DN_ST), 0)
    c = lax.broadcasted_iota(jnp.int32, (DN_ST, DN_ST), 1)
    same = (r // DN_CHUNK) == (c // DN_CHUNK)
    eye = (r == c).astype(F32)
    tril64 = _tri(DN_CHUNK, True).astype(BF16)
    triu64 = _tri(DN_CHUNK, False).astype(BF16)
    tj = lax.broadcasted_iota(jnp.int32, (DN_CHUNK, DN_ST), 0)
    ti = lax.broadcasted_iota(jnp.int32, (DN_CHUNK, DN_ST), 1) % DN_CHUNK
    cum_f = (tj <= ti).astype(BF16)
    cum_b = (tj >= ti).astype(BF16)

    def chunk_problem(d, ci):
        r0 = pl.multiple_of(ci * DN_CHUNK, DN_CHUNK)
        rows = pl.ds(r0, DN_CHUNK)
        incl = jnp.logical_and(same, (r >= c) if d == 0 else (r <= c))
        strict = jnp.logical_and(same, (r > c) if d == 0 else (r < c))
        gc_c = _dot_exact_lhs(tril64 if d == 0 else triu64, g_scr[rows, :])
        gc_t = _dot_exact_rhs(grow_scr[ci], cum_f if d == 0 else cum_b)
        gc_r = jnp.concatenate(
            [jnp.broadcast_to(gc_t[d * DN_HEADS + h:d * DN_HEADS + h + 1, :], (DN_CHUNK, DN_ST))
             for h in range(DN_HEADS)], axis=0)
        beta = b_scr[rows, :]
        end_row = DN_CHUNK - 1 if d == 0 else 0
        k_st, q_st, v_st, bt_st, gc_st, gl_st = [], [], [], [], [], []
        for h in range(DN_HEADS):
            ln = d * DN_HEADS + h
            k_st.append(k_scr[rows, h * DN_DK:(h + 1) * DN_DK])
            q_st.append(q_scr[rows, h * DN_DK:(h + 1) * DN_DK])
            v_st.append(v_scr[rows, h * DN_DV:(h + 1) * DN_DV])
            bt_st.append(beta[:, ln:ln + 1])
            col = gc_c[:, ln:ln + 1]
            gc_st.append(col)
            gl_st.append(col[end_row:end_row + 1, :])
        kst = jnp.concatenate(k_st, axis=0)
        qst = jnp.concatenate(q_st, axis=0)
        vst = jnp.concatenate(v_st, axis=0)
        bst = jnp.concatenate(bt_st, axis=0)
        gst = jnp.concatenate(gc_st, axis=0)
        decay = jnp.where(incl, jnp.exp(jnp.where(incl, gst - gc_r, 0.0)), 0.0)
        kb = kst * bst
        kstb = kst.astype(BF16)
        m = jnp.where(strict, _bdot_nt(kb, kstb) * decay, 0.0)
        attn = jnp.where(incl, _bdot_nt(qst, kstb) * decay, 0.0)
        a_scr[d, ci] = attn.astype(BF16)
        yield
        t = eye - m
        pb = m.astype(BF16)
        p = jnp.dot(pb, pb, preferred_element_type=F32)
        yield
        for lvl in range(5):
            pb = p.astype(BF16)
            t = t + jnp.dot(t.astype(BF16), pb, preferred_element_type=F32)
            if lvl < 4:
                p = jnp.dot(pb, pb, preferred_element_type=F32)
            yield
        tb = t.astype(BF16)
        rhs = jnp.concatenate([vst * bst, kb * jnp.exp(gst)], axis=1)
        x0 = jnp.dot(tb, rhs.astype(BF16), preferred_element_type=F32)
        yield
        res = rhs - x0 - _dot3(m, x0)
        yield
        uw = x0 + jnp.dot(tb, res.astype(BF16), preferred_element_type=F32)
        yield
        qg = qst * jnp.exp(gst)
        u_scr[d, ci] = uw[:, :DN_DV]
        kdec = []
        for h in range(DN_HEADS):
            hs = slice(h * DN_CHUNK, (h + 1) * DN_CHUNK)
            wq_scr[d, ci, h] = jnp.concatenate([uw[hs, DN_DV:], qg[hs, :]], axis=0).astype(BF16)
            gl = gl_st[h]
            kdec.append(k_st[h] * jnp.exp(gl - gc_st[h]))
            gl_scr[d, ci, h:h + 1, :] = jnp.broadcast_to(jnp.exp(gl), (1, DN_DV))
        kd_scr[d, ci] = jnp.concatenate(kdec, axis=0).astype(BF16)

    def chunk_step(kk, carry):
        _round_robin([chunk_problem(d, kk * DN_PAR + j) for j in range(DN_PAR) for d in range(N_DIR)])
        return carry

    lax.fori_loop(0, nck // DN_PAR, chunk_step, 0)

    def state_problem(d, ci):
        rows = pl.ds(pl.multiple_of(ci * DN_CHUNK, DN_CHUNK), DN_CHUNK)
        s_old, vnew, qs_all = [], [], []
        for h in range(DN_HEADS):
            hs = slice(h * DN_CHUNK, (h + 1) * DN_CHUNK)
            s_h = s_scr[d, h]
            ws = jnp.dot(wq_scr[d, ci, h], s_h.astype(BF16), preferred_element_type=F32)
            s_old.append(s_h)
            vnew.append((u_scr[d, ci, hs, :] - ws[:DN_CHUNK]).astype(BF16))
            qs_all.append(ws[DN_CHUNK:])
        yield
        o_st = jnp.concatenate(qs_all, axis=0) + jnp.dot(
            a_scr[d, ci], jnp.concatenate(vnew, axis=0), preferred_element_type=F32)
        for h in range(DN_HEADS):
            hs = slice(h * DN_CHUNK, (h + 1) * DN_CHUNK)
            s_scr[d, h] = s_old[h] * gl_scr[d, ci, h:h + 1, :] + lax.dot_general(
                kd_scr[d, ci, hs, :], vnew[h], (((0,), (0,)), ((), ())), preferred_element_type=F32)
        yield
        for h in range(DN_HEADS):
            hs = slice(h * DN_CHUNK, (h + 1) * DN_CHUNK)
            o_ref[rows, h * DN_DV:(h + 1) * DN_DV] += o_st[hs, :]

    def state_step(kk, carry):
        _round_robin([state_problem(0, kk), state_problem(1, nck - 1 - kk)])
        return carry

    lax.fori_loop(0, nck, state_step, 0)
    sf_ref[...] = s_scr[...]

    def post(bi, carry):
        rows = pl.ds(pl.multiple_of(bi * PRE_ROWS, PRE_ROWS), PRE_ROWS)
        for h in range(DN_HEADS):
            cols = slice(h * DN_DV, (h + 1) * DN_DV)
            oh = o_ref[rows, cols]
            oh = oh * lax.rsqrt(jnp.mean(oh * oh, axis=-1, keepdims=True) + EPS) * ng_ref[...]
            on_ref[rows, cols] = (oh * _silu(dg_ref[rows, cols])).astype(BF16)
        return carry

    lax.fori_loop(0, seq // PRE_ROWS, post, 0)


def _dn(proj, small_t, row0, nseq, seq, width, conv_w, dt_bias, a_log, norm_g, s0):
    blk0 = row0 // seq
    nck = seq // DN_CHUNK
    kern = functools.partial(_dn_kernel, seq=seq, width=width)
    full = lambda *shape: pl.BlockSpec(shape, lambda b: (0,) * len(shape))
    dtb_r = dt_bias.reshape(1, 8)
    al_r = a_log.reshape(1, 8)
    dtb_c = dt_bias.reshape(8, 1)
    al_c = a_log.reshape(8, 1)
    return pl.pallas_call(
        kern,
        grid=(nseq,),
        in_specs=[
            pl.BlockSpec((seq, DN_CONV_DIM), lambda b: (blk0 + b, COL_QKV // DN_CONV_DIM)),
            pl.BlockSpec((seq, SMALL_W), lambda b: (blk0 + b, COL_SMALL // SMALL_W)),
            pl.BlockSpec((32, seq), lambda b: (0, blk0 + b)),
            pl.BlockSpec((seq, DN_V), lambda b: (blk0 + b, COL_DNG // DN_V)),
            full(3, DN_CONV_DIM), full(1, 8), full(1, 8), full(8, 1), full(8, 1), full(1, DN_DV),
            pl.BlockSpec((None, N_DIR, DN_HEADS, DN_DK, DN_DV), lambda b: (b, 0, 0, 0, 0)),
        ],
        out_specs=[
            pl.BlockSpec((seq, DN_V), lambda b: (b, 0)),
            pl.BlockSpec((None, N_DIR, DN_HEADS, DN_DK, DN_DV), lambda b: (b, 0, 0, 0, 0)),
        ],
        out_shape=[jax.ShapeDtypeStruct((nseq * seq, DN_V), BF16),
                   jax.ShapeDtypeStruct((nseq, N_DIR, DN_HEADS, DN_DK, DN_DV), F32)],
        scratch_shapes=[
            pltpu.VMEM((seq, DN_QK), F32), pltpu.VMEM((seq, DN_QK), F32), pltpu.VMEM((seq, DN_V), F32),
            pltpu.VMEM((seq, 8), F32), pltpu.VMEM((seq, 8), F32),
            pltpu.VMEM((N_DIR, DN_HEADS, DN_DK, DN_DV), F32),
            pltpu.VMEM((N_DIR, nck, DN_ST, DN_DV), F32),
            pltpu.VMEM((N_DIR, nck, DN_HEADS, 2 * DN_CHUNK, DN_DK), BF16),
            pltpu.VMEM((N_DIR, nck, DN_ST, DN_ST), BF16),
            pltpu.VMEM((N_DIR, nck, DN_ST, DN_DK), BF16),
            pltpu.VMEM((N_DIR, nck, 8, DN_DV), F32),
            pltpu.VMEM((nck, 8, DN_CHUNK), F32),
            pltpu.VMEM((seq, DN_V), F32),
        ],
        compiler_params=_cparams("parallel"),
        name="dn_seq%d" % seq,
    )(proj, proj, small_t, proj, conv_w, dtb_r, al_r, dtb_c, al_c, norm_g.reshape(1, -1), s0)


MRG_TM = 512
MRG_TD = 512


def _merge_kernel(x_ref, sh_ref, sc_ref, gt_ref, ng_ref, y5_ref, u5_ref, d5_ref, ysc_ref, ysl_ref,
                  odc_ref, odl_ref, wg0_ref, wg1_ref, wg2_ref, glu_ref, swo_ref,
                  dwo_ref, wo_ref, o_ref, h_scr, a_scr, b_scr, c_scr, acc_scr):
    i = pl.program_id(0)
    j = pl.program_id(1)
    grp = _row_group(i, MRG_TM)

    @pl.when(j == 0)
    def _():
        is_ctx = i < N_CTX // MRG_TM
        h = _norm_mod(x_ref[...], ng_ref[...], sc_ref[pl.ds(grp, 1), :], sh_ref[pl.ds(grp, 1), :])
        h_scr[...] = h.astype(BF16)
        a_scr[...] = jax.nn.gelu(y5_ref[...] + d5_ref[...] * u5_ref[...]).astype(BF16)
        b_scr[...] = jnp.where(is_ctx, ysc_ref[...], ysl_ref[...])
        c_scr[...] = jnp.where(is_ctx, odc_ref[...], odl_ref[...])
        acc_scr[...] = jnp.zeros_like(acc_scr)

    h = h_scr[...]
    nt = (((1,), (1,)), ((), ()))
    gate = lambda w_ref: _sigmoid(lax.dot_general(h, w_ref[...], nt, preferred_element_type=F32))
    g5 = a_scr[...]
    br_a = (jnp.dot(g5, glu_ref[0], preferred_element_type=F32)
            * _sigmoid(jnp.dot(g5, glu_ref[1], preferred_element_type=F32)))
    br_b = jnp.dot(b_scr[...], swo_ref[...], preferred_element_type=F32)
    br_c = jnp.dot(c_scr[...], dwo_ref[...], preferred_element_type=F32)
    merged = gate(wg0_ref) * br_a + gate(wg1_ref) * br_b + gate(wg2_ref) * br_c
    acc_scr[...] += jnp.dot(merged.astype(BF16), wo_ref[...], preferred_element_type=F32)

    @pl.when(j == pl.num_programs(1) - 1)
    def _():
        o_ref[...] = x_ref[...] + gt_ref[pl.ds(grp, 1), :] * acc_scr[...]


def _merge(x, mod, norm_g_row, proj, w_packed, y5, s5_d_l, ys_c, ys_l, od_c, od_l, glu_b, swo_b, dwo_b, wo_b,
           layer):
    nctx = N_CTX // MRG_TM
    rowblk = lambda w, col: pl.BlockSpec((MRG_TM, w), lambda i, j: (i, col // w))
    ctxblk = lambda w: pl.BlockSpec((MRG_TM, w), lambda i, j: (jnp.minimum(i, nctx - 1), 0))
    latblk = lambda w: pl.BlockSpec((MRG_TM, w), lambda i, j: (jnp.maximum(i - nctx, 0), 0))
    row1 = lambda w: pl.BlockSpec((1, w), lambda i, j: (0, 0))
    modblk = lambda k: pl.BlockSpec((8, D_MODEL), lambda i, j: (0, k))
    gateblk = lambda k: pl.BlockSpec((None, MRG_TD, D_MODEL),
                                     lambda i, j: (layer, (COL_GATES + k * D_MODEL) // MRG_TD + j, 0))
    return pl.pallas_call(
        _merge_kernel,
        grid=(N_ROWS // MRG_TM, D_MODEL // MRG_TD),
        in_specs=[
            rowblk(D_MODEL, 0), modblk(3), modblk(4), modblk(5), row1(D_MODEL),
            rowblk(S5_WIDTH, 0), rowblk(S5_WIDTH, COL_U), row1(S5_WIDTH),
            ctxblk(SSD_WIDTH), latblk(SSD_WIDTH), ctxblk(DN_V), latblk(DN_V),
            gateblk(0), gateblk(1), gateblk(2),
            pl.BlockSpec((None, 2, S5_WIDTH, MRG_TD), lambda i, j: (layer, 0, 0, j)),
            pl.BlockSpec((None, SSD_WIDTH, MRG_TD), lambda i, j: (layer, 0, j)),
            pl.BlockSpec((None, DN_V, MRG_TD), lambda i, j: (layer, 0, j)),
            pl.BlockSpec((None, MRG_TD, D_MODEL), lambda i, j: (layer, j, 0)),
        ],
        out_specs=rowblk(D_MODEL, 0),
        out_shape=jax.ShapeDtypeStruct((N_ROWS, D_MODEL), F32),
        scratch_shapes=[pltpu.VMEM((MRG_TM, D_MODEL), BF16), pltpu.VMEM((MRG_TM, S5_WIDTH), BF16),
                        pltpu.VMEM((MRG_TM, SSD_WIDTH), BF16), pltpu.VMEM((MRG_TM, DN_V), BF16),
                        pltpu.VMEM((MRG_TM, D_MODEL), F32)],
        compiler_params=_cparams("parallel", "arbitrary"),
        name="merge",
    )(x, mod, mod, mod, norm_g_row, y5, proj, s5_d_l.reshape(1, -1), ys_c, ys_l, od_c, od_l,
      w_packed, w_packed, w_packed, glu_b, swo_b, dwo_b, wo_b)


FIN_TM = 1024


def _final_norm_kernel(x_ref, g_ref, oc_ref, ol_ref):
    i = pl.program_id(0)
    x = x_ref[...]
    y = x * lax.rsqrt(jnp.mean(x * x, axis=-1, keepdims=True) + EPS) * g_ref[...]

    @pl.when(i < N_CTX // FIN_TM)
    def _():
        oc_ref[...] = y

    @pl.when(i >= N_CTX // FIN_TM)
    def _():
        ol_ref[...] = y


def _final_norm(x, g):
    nctx = N_CTX // FIN_TM
    return pl.pallas_call(
        _final_norm_kernel,
        grid=(N_ROWS // FIN_TM,),
        in_specs=[pl.BlockSpec((FIN_TM, D_MODEL), lambda i: (i, 0)), pl.BlockSpec((1, D_MODEL), lambda i: (0, 0))],
        out_specs=[pl.BlockSpec((FIN_TM, D_MODEL), lambda i: (jnp.minimum(i, nctx - 1), 0)),
                   pl.BlockSpec((FIN_TM, D_MODEL), lambda i: (jnp.maximum(i - nctx, 0), 0))],
        out_shape=[jax.ShapeDtypeStruct((N_CTX, D_MODEL), F32), jax.ShapeDtypeStruct((N_LAT, D_MODEL), F32)],
        compiler_params=_cparams("arbitrary"),
        name="final_norm",
    )(x, g.reshape(1, -1))


def kernel(x_prompt, x_sample, state_s5_re, state_s5_im, state_ssd, state_dn, c, c_ctx, ada_w, ada_b, norm_g, ffn_wi, ffn_wo, w_in, s5_lam_re, s5_lam_im, s5_log_dt, s5_b_re, s5_b_im, s5_c_re, s5_c_im, s5_d, s5_glu, ssd_conv_w, ssd_conv_b, ssd_dt_bias, ssd_a_log, ssd_d, ssd_norm_g, ssd_w_out, dn_conv_w, dn_dt_bias, dn_a_log, dn_norm_g, dn_w_out, w_out, final_norm_g):
    x = jnp.concatenate([x_prompt.reshape(N_CTX, D_MODEL), x_sample.reshape(N_LAT, D_MODEL)], axis=0)
    cond8 = jnp.concatenate([c_ctx[None, :], c, jnp.zeros((8 - 1 - DEC_BATCH, D_MODEL), F32)], axis=0)
    mods = _ada_mods(cond8, ada_w, ada_b)

    s5_tables = _s5_prep(s5_lam_re, s5_lam_im, s5_log_dt, s5_b_re, s5_b_im, s5_c_re, s5_c_im)
    s5_perm = _s5_perm()
    w_packed = _repack_w_in(w_in)
    glu_b = s5_glu.astype(BF16)
    swo_b = ssd_w_out.astype(BF16)
    dwo_b = dn_w_out.astype(BF16)
    wo_b = w_out.astype(BF16)
    ssd_d_rows = jnp.repeat(ssd_d, SSD_HEADDIM, axis=1).reshape(DEPTH, 1, SSD_WIDTH)

    def s5_h0(state):
        return state.transpose(1, 3, 0, 2, 4).reshape(DEPTH, S5_GROUPS, DEC_BATCH, 2 * S5_STATE)

    h0_re = s5_h0(state_s5_re)
    h0_im = s5_h0(state_s5_im)
    ssd_h0_lat = _ssd_state_to_pairs(jnp.swapaxes(state_ssd, 0, 1))
    ssd_h0_ctx = jnp.zeros((BATCH, N_DIR, SSD_PAIRS, SSD_STATE, SSD_PAIR), F32)
    dn_s0_lat = jnp.swapaxes(state_dn, 0, 1)
    dn_s0_ctx = jnp.zeros((BATCH, N_DIR, DN_HEADS, DN_DK, DN_DV), F32)

    new_s5_re, new_s5_im, new_ssd, new_dn = [], [], [], []
    for l in range(DEPTH):
        mod = mods[l]
        x = _ffn(x, mod, norm_g[l, 0:1], ffn_wi, ffn_wo, l, 0)
        proj, small_t = _inproj(x, mod, norm_g[l, 1:2], w_packed, l)

        y5, f_re, f_im = _s5_scan(proj, s5_perm, s5_tables, h0_re, h0_im, l)
        new_s5_re.append(f_re.reshape(S5_GROUPS, BATCH, N_DIR, S5_STATE).transpose(1, 2, 0, 3))
        new_s5_im.append(f_im.reshape(S5_GROUPS, BATCH, N_DIR, S5_STATE).transpose(1, 2, 0, 3))

        ys_c, hs_c = _ssd(proj, small_t, 0, BATCH, SEQ, SEQ, ssd_conv_w[l], ssd_conv_b[l], ssd_dt_bias[l],
                          ssd_a_log[l], ssd_d_rows[l], ssd_norm_g[l], ssd_h0_ctx)
        ys_l, _ = _ssd(proj, small_t, N_CTX, DEC_BATCH, DEC_SEQ, GRID_W, ssd_conv_w[l], ssd_conv_b[l],
                       ssd_dt_bias[l], ssd_a_log[l], ssd_d_rows[l], ssd_norm_g[l], ssd_h0_lat[l])
        new_ssd.append(_ssd_state_from_pairs(hs_c))
        od_c, sd_c = _dn(proj, small_t, 0, BATCH, SEQ, SEQ, dn_conv_w[l], dn_dt_bias[l], dn_a_log[l],
                         dn_norm_g[l], dn_s0_ctx)
        od_l, _ = _dn(proj, small_t, N_CTX, DEC_BATCH, DEC_SEQ, GRID_W, dn_conv_w[l], dn_dt_bias[l],
                      dn_a_log[l], dn_norm_g[l], dn_s0_lat[l])
        new_dn.append(sd_c)

        x = _merge(x, mod, norm_g[l, 1:2], proj, w_packed, y5, s5_d[l], ys_c, ys_l, od_c, od_l, glu_b, swo_b,
                   dwo_b, wo_b, l)
        x = _ffn(x, mod, norm_g[l, 2:3], ffn_wi, ffn_wo, l, 1)

    y_ctx, y_lat = _final_norm(x, final_norm_g)
    y_prompt = y_ctx.reshape(BATCH, SEQ, D_MODEL)
    y_sample = y_lat.reshape(DEC_BATCH, DEC_SEQ, D_MODEL)
    return (y_prompt, y_sample, jnp.stack(new_s5_re, axis=1), jnp.stack(new_s5_im, axis=1),
            jnp.stack(new_ssd, axis=1), jnp.stack(new_dn, axis=1))
```

```python
import functools

import jax
import jax.numpy as jnp
import numpy as np
from jax import lax
from jax.experimental import pallas as pl
from jax.experimental.pallas import tpu as pltpu

F32 = jnp.float32
BF16 = jnp.bfloat16

D_MODEL = 1024
BATCH = 16
SEQ = 256
DEPTH = 4
DEC_BATCH = 2
DEC_SEQ = 1024
GRID_W = 64
N_DIR = 2
N_ADA = 9
D_FF = 2816
EPS = 1e-6

S5_WIDTH = 512
S5_GROUP = 16
S5_GROUPS = 32
S5_STATE = 64
S5_CHUNK = 16

SSD_WIDTH = 512
SSD_HEADDIM = 64
SSD_HEADS = 8
SSD_GROUPS = 2
SSD_STATE = 64
SSD_CHUNK = 128
SSD_CONV_DIM = 768

DN_HEADS = 4
DN_DK = 128
DN_DV = 128
DN_QK = 512
DN_V = 512
DN_CHUNK = 64
DN_CONV_DIM = 1536

IN_SEGMENTS = (512, 512, 768, 16, 1536, 8, 8, 512, 3072)
IN_SPLITS = tuple(int(s) for s in np.cumsum(IN_SEGMENTS)[:-1])

N_CTX = BATCH * SEQ
N_LAT = DEC_BATCH * DEC_SEQ
N_ROWS = N_CTX + N_LAT

COL_QKV = 0
COL_U = 1536
COL_Z = 2048
COL_DNG = 2560
COL_XBC = 3072
COL_SMALL = 3840
PROJ_W = 4096
COL_GATES = PROJ_W
PACK_W = PROJ_W + 3 * D_MODEL
SMALL_W = 128

VMEM_LIMIT = 56 * 1024 * 1024


def _cparams(*sem):
    return pltpu.CompilerParams(dimension_semantics=sem, vmem_limit_bytes=VMEM_LIMIT)


def _sigmoid(x):
    return 0.5 * (jnp.tanh(0.5 * x) + 1.0)


def _silu(x):
    return x * _sigmoid(x)


def _softplus(x):
    return jnp.maximum(x, 0.0) + jnp.log(1.0 + jnp.exp(-jnp.abs(x)))


def _bdot(a, b):
    return jnp.dot(a.astype(BF16), b.astype(BF16), preferred_element_type=F32)


def _bdot_nt(a, b):
    return lax.dot_general(a.astype(BF16), b.astype(BF16), (((1,), (1,)), ((), ())),
                           preferred_element_type=F32)


def _bdot_tn(a, b):
    return lax.dot_general(a.astype(BF16), b.astype(BF16), (((0,), (0,)), ((), ())),
                           preferred_element_type=F32)


def _split3(a):
    hi = a.astype(BF16)
    r = a - hi.astype(F32)
    mid = r.astype(BF16)
    lo = (r - mid.astype(F32)).astype(BF16)
    return hi, mid, lo


def _dot3(a, b):
    ah = a.astype(BF16)
    al = (a - ah.astype(F32)).astype(BF16)
    bh = b.astype(BF16)
    bl = (b - bh.astype(F32)).astype(BF16)
    out = jnp.dot(ah, bh, preferred_element_type=F32)
    out = out + jnp.dot(ah, bl, preferred_element_type=F32)
    out = out + jnp.dot(al, bh, preferred_element_type=F32)
    return out


def _dot_exact_lhs(t_bf16, x):
    hi, mid, lo = _split3(x)
    out = jnp.dot(t_bf16, hi, preferred_element_type=F32)
    out = out + jnp.dot(t_bf16, mid, preferred_element_type=F32)
    out = out + jnp.dot(t_bf16, lo, preferred_element_type=F32)
    return out


def _dot_exact_rhs(x, t_bf16):
    hi, mid, lo = _split3(x)
    out = jnp.dot(hi, t_bf16, preferred_element_type=F32)
    out = out + jnp.dot(mid, t_bf16, preferred_element_type=F32)
    out = out + jnp.dot(lo, t_bf16, preferred_element_type=F32)
    return out


def _norm_mod(x, g, sc, sh):
    ms = jnp.mean(x * x, axis=-1, keepdims=True)
    y = x * lax.rsqrt(ms + EPS) * g
    return y * (1.0 + sc) + sh


def _row_group(i, tm):
    nctx = N_CTX // tm
    per = DEC_SEQ // tm
    return jnp.where(i < nctx, 0, 1 + jnp.maximum(i - nctx, 0) // per)


ADA_TN = 1152


def _ada_kernel(c_ref, w_ref, b_ref, o_ref):
    c = c_ref[...]
    o_ref[...] = _bdot(_silu(c), w_ref[...]) + b_ref[...]


def _ada_mods(cond8, ada_w, ada_b):
    nj = (N_ADA * D_MODEL) // ADA_TN
    return pl.pallas_call(
        _ada_kernel,
        grid=(DEPTH, nj),
        in_specs=[
            pl.BlockSpec((8, D_MODEL), lambda l, j: (0, 0)),
            pl.BlockSpec((None, D_MODEL, ADA_TN), lambda l, j: (l, 0, j)),
            pl.BlockSpec((None, 1, ADA_TN), lambda l, j: (l, 0, j)),
        ],
        out_specs=pl.BlockSpec((None, 8, ADA_TN), lambda l, j: (l, 0, j)),
        out_shape=jax.ShapeDtypeStruct((DEPTH, 8, N_ADA * D_MODEL), F32),
        compiler_params=_cparams("parallel", "parallel"),
        name="ada_mods",
    )(cond8, ada_w, ada_b.reshape(DEPTH, 1, N_ADA * D_MODEL))


FFN_TM = 2048
FFN_TF = 256
MOD_ROWS = 1024
FFN_SUB = FFN_TM // MOD_ROWS


def _ffn_kernel(x_ref, sh_ref, sc_ref, gt_ref, g_ref, wa_ref, wb_ref, wo_ref, o_ref, h_scr, acc_scr):
    i = pl.program_id(0)
    j = pl.program_id(1)

    @pl.when(j == 0)
    def _():
        for s in range(FFN_SUB):
            rows = slice(s * MOD_ROWS, (s + 1) * MOD_ROWS)
            grp = _row_group(i * FFN_SUB + s, MOD_ROWS)
            h = _norm_mod(x_ref[rows, :], g_ref[...], sc_ref[pl.ds(grp, 1), :], sh_ref[pl.ds(grp, 1), :])
            h_scr[rows, :] = h.astype(BF16)
        acc_scr[...] = jnp.zeros_like(acc_scr)

    h = h_scr[...]
    a = jnp.dot(h, wa_ref[...].astype(BF16), preferred_element_type=F32)
    b = jnp.dot(h, wb_ref[...].astype(BF16), preferred_element_type=F32)
    u = (_silu(a) * b).astype(BF16)
    acc_scr[...] += jnp.dot(u, wo_ref[...].astype(BF16), preferred_element_type=F32)

    @pl.when(j == pl.num_programs(1) - 1)
    def _():
        for s in range(FFN_SUB):
            rows = slice(s * MOD_ROWS, (s + 1) * MOD_ROWS)
            grp = _row_group(i * FFN_SUB + s, MOD_ROWS)
            o_ref[rows, :] = x_ref[rows, :] + (0.5 * gt_ref[pl.ds(grp, 1), :]) * acc_scr[rows, :]


def _ffn(x, mod, norm_g_row, ffn_wi, ffn_wo, layer, which):
    nf = D_FF // FFN_TF
    base = 0 if which == 0 else 6
    return pl.pallas_call(
        _ffn_kernel,
        grid=(N_ROWS // FFN_TM, nf),
        in_specs=[
            pl.BlockSpec((FFN_TM, D_MODEL), lambda i, j: (i, 0)),
            pl.BlockSpec((8, D_MODEL), lambda i, j: (0, base)),
            pl.BlockSpec((8, D_MODEL), lambda i, j: (0, base + 1)),
            pl.BlockSpec((8, D_MODEL), lambda i, j: (0, base + 2)),
            pl.BlockSpec((1, D_MODEL), lambda i, j: (0, 0)),
            pl.BlockSpec((None, None, D_MODEL, FFN_TF), lambda i, j: (layer, which, 0, j)),
            pl.BlockSpec((None, None, D_MODEL, FFN_TF), lambda i, j: (layer, which, 0, j + nf)),
            pl.BlockSpec((None, None, FFN_TF, D_MODEL), lambda i, j: (layer, which, j, 0)),
        ],
        out_specs=pl.BlockSpec((FFN_TM, D_MODEL), lambda i, j: (i, 0)),
        out_shape=jax.ShapeDtypeStruct((N_ROWS, D_MODEL), F32),
        scratch_shapes=[pltpu.VMEM((FFN_TM, D_MODEL), BF16), pltpu.VMEM((FFN_TM, D_MODEL), F32)],
        compiler_params=_cparams("parallel", "arbitrary"),
        name="ffn",
    )(x, mod, mod, mod, norm_g_row, ffn_wi, ffn_wi, ffn_wo)


INP_TM = 512


def _inproj_kernel(x_ref, sh_ref, sc_ref, g_ref, w_ref, o_ref, ot_ref):
    grp = _row_group(pl.program_id(0), INP_TM)
    h = _norm_mod(x_ref[...], g_ref[...], sc_ref[pl.ds(grp, 1), :], sh_ref[pl.ds(grp, 1), :])
    res = lax.dot_general(h.astype(BF16), w_ref[...], (((1,), (1,)), ((), ())), preferred_element_type=F32)
    o_ref[...] = res
    ot_ref[...] = res[:, COL_SMALL:COL_SMALL + SMALL_W].T


RPK_LANES = 256
RPK_MOVES = ((COL_QKV, 1808, 1536), (COL_U, 0, 512), (COL_Z, 512, 512), (COL_DNG, 3360, 512),
             (COL_XBC, 1024, 768), (COL_SMALL, 1792, 16), (COL_SMALL + 16, 3344, 16), (COL_GATES, 3872, 3072))
RPK_PAD = (COL_SMALL + 32, PROJ_W)


def _repack_kernel(w_ref, o_ref):
    for dst, src, n in RPK_MOVES:
        o_ref[dst:dst + n, :] = w_ref[src:src + n, :].astype(BF16)
    o_ref[RPK_PAD[0]:RPK_PAD[1], :] = jnp.zeros((RPK_PAD[1] - RPK_PAD[0], RPK_LANES), BF16)


def _repack_w_in(w_in):
    w_t = jnp.swapaxes(w_in, 1, 2)
    in_w = w_t.shape[1]
    return pl.pallas_call(
        _repack_kernel,
        grid=(DEPTH, D_MODEL // RPK_LANES),
        in_specs=[pl.BlockSpec((None, in_w, RPK_LANES), lambda l, i: (l, 0, i))],
        out_specs=pl.BlockSpec((None, PACK_W, RPK_LANES), lambda l, i: (l, 0, i)),
        out_shape=jax.ShapeDtypeStruct((DEPTH, PACK_W, D_MODEL), BF16),
        compiler_params=_cparams("parallel", "parallel"),
        name="repack_w_in",
    )(w_t)


def _inproj(x, mod, norm_g_row, w_packed, layer):
    return pl.pallas_call(
        _inproj_kernel,
        grid=(N_ROWS // INP_TM,),
        in_specs=[
            pl.BlockSpec((INP_TM, D_MODEL), lambda i: (i, 0)),
            pl.BlockSpec((8, D_MODEL), lambda i: (0, 3)),
            pl.BlockSpec((8, D_MODEL), lambda i: (0, 4)),
            pl.BlockSpec((1, D_MODEL), lambda i: (0, 0)),
            pl.BlockSpec((None, PROJ_W, D_MODEL), lambda i: (layer, 0, 0)),
        ],
        out_specs=[pl.BlockSpec((INP_TM, PROJ_W), lambda i: (i, 0)),
                   pl.BlockSpec((SMALL_W, INP_TM), lambda i: (0, i))],
        out_shape=[jax.ShapeDtypeStruct((N_ROWS, PROJ_W), F32),
                   jax.ShapeDtypeStruct((SMALL_W, N_ROWS), F32)],
        compiler_params=_cparams("parallel"),
        name="inproj",
    )(x, mod, mod, norm_g_row, w_packed)


S5_ROW = S5_CHUNK * S5_GROUP


def _s5_prep_kernel(lam_re_ref, lam_im_ref, ldt_ref, btr_ref, bti_ref, cr_ref, ci_ref, ctr_ref, cti_ref,
                    m_ref, pre_ref, pim_ref, qre_ref, qim_ref, are_ref, aim_ref):
    tau = lax.broadcasted_iota(jnp.int32, (S5_CHUNK, 1), 0).astype(F32)
    lane = lax.broadcasted_iota(jnp.int32, (S5_GROUP, S5_ROW), 1)
    lane_t = lane[0:1, :] // S5_GROUP
    rep = (lax.broadcasted_iota(jnp.int32, (S5_GROUP, S5_ROW), 0) == lane % S5_GROUP).astype(BF16)
    taps = [None, None]

    def outer(ar, ai, xr, xi):
        rr = ar[:, None, :] * xr[None, :, :] - ai[:, None, :] * xi[None, :, :]
        ii = ar[:, None, :] * xi[None, :, :] + ai[:, None, :] * xr[None, :, :]
        return rr.reshape(S5_ROW, S5_STATE), ii.reshape(S5_ROW, S5_STATE)

    def direction(d):
        lr = lam_re_ref[d:d + 1, :]
        li = lam_im_ref[d:d + 1, :]
        dt = jnp.exp(ldt_ref[d:d + 1, :])
        mag = jnp.exp(lr * dt)
        lb_re = mag * jnp.cos(li * dt)
        lb_im = mag * jnp.sin(li * dt)
        den = lr * lr + li * li
        cr = ((lb_re - 1.0) * lr + lb_im * li) / den
        ci = (lb_im * lr - (lb_re - 1.0) * li) / den
        bt_r = btr_ref[d]
        bt_i = bti_ref[d]
        bbt_r = cr * bt_r - ci * bt_i
        bbt_i = cr * bt_i + ci * bt_r
        c_r = cr_ref[d]
        c_i = ci_ref[d]

        def powtab(t):
            m = jnp.exp(t * (lr * dt))
            ang = t * (li * dt)
            return m * jnp.cos(ang), m * jnp.sin(ang)

        t_in = (S5_CHUNK - 1) - tau if d == 0 else tau
        ar, ai = powtab(t_in)
        ba_r, ba_i = outer(ar, ai, bbt_r, bbt_i)
        lanes = slice(d * S5_STATE, (d + 1) * S5_STATE)
        pre_ref[:, lanes] = ba_r.astype(BF16)
        pim_ref[:, lanes] = ba_i.astype(BF16)
        yield
        kt = _dot3(ba_r, ctr_ref[d]) - _dot3(ba_i, cti_ref[d])
        yield
        taps[d] = _dot_exact_rhs(kt, rep)
        yield
        t_out = tau + 1.0 if d == 0 else S5_CHUNK - tau
        ar, ai = powtab(t_out)
        qr, qi = outer(ar, ai, c_r, c_i)
        qre_ref[:, lanes] = qr.astype(BF16)
        qim_ref[:, lanes] = (-qi).astype(BF16)
        a16r, a16i = powtab(jnp.full((1, 1), float(S5_CHUNK), F32))
        are_ref[:, lanes] = a16r
        aim_ref[:, lanes] = a16i

    _round_robin([direction(d) for d in range(N_DIR)])

    last = S5_ROW - S5_GROUP
    table = jnp.concatenate([taps[0][:last, :], taps[0][last:, :] + taps[1][:S5_GROUP, :], taps[1][S5_GROUP:, :]],
                            axis=0)
    mmat = table[last:last + S5_ROW, :]
    for t in range(1, S5_CHUNK):
        start = (S5_CHUNK - 1 - t) * S5_GROUP
        mmat = jnp.where(lane_t == t, table[start:start + S5_ROW, :], mmat)
    m_ref[...] = mmat.astype(BF16)


def _s5_prep(lam_re, lam_im, log_dt, b_re, b_im, c_re, c_im):
    tg = lambda t: jnp.swapaxes(t, 1, 2)
    lam_re_g = tg(lam_re)
    lam_im_g = tg(lam_im)
    ldt_g = tg(log_dt)[..., None]
    bt_r = jnp.swapaxes(tg(b_re), -1, -2)
    bt_i = jnp.swapaxes(tg(b_im), -1, -2)
    c_r = tg(c_re)
    c_i = tg(c_im)
    ct_r = jnp.swapaxes(c_r, -1, -2)
    ct_i = jnp.swapaxes(c_i, -1, -2)

    def spec(*tail):
        n = len(tail)
        return pl.BlockSpec((None, None) + tail, lambda l, g: (l, g) + (0,) * n)

    st = 2 * S5_STATE
    tab = jax.ShapeDtypeStruct((DEPTH, S5_GROUPS, S5_ROW, st), BF16)
    dec = jax.ShapeDtypeStruct((DEPTH, S5_GROUPS, 1, st), F32)
    return pl.pallas_call(
        _s5_prep_kernel,
        grid=(DEPTH, S5_GROUPS),
        in_specs=[spec(2, 64), spec(2, 64), spec(2, 1), spec(2, 16, 64), spec(2, 16, 64),
                  spec(2, 16, 64), spec(2, 16, 64), spec(2, 64, S5_GROUP), spec(2, 64, S5_GROUP)],
        out_specs=[spec(S5_ROW, S5_ROW), spec(S5_ROW, st), spec(S5_ROW, st), spec(S5_ROW, st), spec(S5_ROW, st),
                   spec(1, st), spec(1, st)],
        out_shape=[jax.ShapeDtypeStruct((DEPTH, S5_GROUPS, S5_ROW, S5_ROW), BF16), tab, tab, tab, tab, dec, dec],
        compiler_params=_cparams("parallel", "parallel"),
        name="s5_prep",
    )(lam_re_g, lam_im_g, ldt_g, bt_r, bt_i, c_r, c_i, ct_r, ct_i)


S5_CTX_CH = SEQ // S5_CHUNK
S5_LAT_CH = DEC_SEQ // S5_CHUNK
S5_CTX_ROWS = S5_CTX_CH * BATCH
S5_LAT_ROWS = S5_LAT_CH * DEC_BATCH
S5_ROWS = S5_CTX_ROWS + S5_LAT_ROWS
S5_GB = 128 // S5_GROUP
S5_PERM = S5_GB * 128


def _s5_perm():
    src = np.arange(S5_PERM)
    s, g, j = src // 128, (src % 128) // S5_GROUP, src % S5_GROUP
    p = np.zeros((S5_PERM, S5_PERM), np.float32)
    p[src, g * 128 + s * S5_GROUP + j] = 1.0
    return jnp.asarray(p, BF16)


def _s5_kernel(u_ref, perm_ref, m_ref, pre_ref, pim_ref, qre_ref, qim_ref, are_ref, aim_ref, h0r_ref, h0i_ref,
               d_ref, y_ref, fr_ref, fi_ref, ug, sre, sim, hfr, hfi, hbr, hbi, ys):
    perm = perm_ref[...]
    half_w = S5_ROW // 2
    for half in range(2):
        x = jnp.concatenate([u_ref[pl.ds(half * 8 + s, S5_ROWS, stride=S5_CHUNK), :] for s in range(8)], axis=1)
        z = jnp.dot(x.astype(BF16), perm, preferred_element_type=F32)
        for g in range(S5_GB):
            ug[g, :, half * half_w:(half + 1) * half_w] = z[:, g * 128:(g + 1) * 128].astype(BF16)
    for g in range(S5_GB):
        sre[g] = jnp.dot(ug[g], pre_ref[g].astype(BF16), preferred_element_type=F32)
        sim[g] = jnp.dot(ug[g], pim_ref[g].astype(BF16), preferred_element_type=F32)
    fwd = lax.broadcasted_iota(jnp.int32, (1, 2 * S5_STATE), 1) < S5_STATE

    def scan(base, nchunk, nseq, init):
        def step(k, hs):
            rf = pl.ds(base + k, nseq, stride=nchunk)
            rb = pl.ds(base + nchunk - 1 - k, nseq, stride=nchunk)
            out = []
            for g in range(S5_GB):
                h_re, h_im = hs[2 * g], hs[2 * g + 1]
                hfr.at[g][rf, :] = h_re
                hfi.at[g][rf, :] = h_im
                hbr.at[g][rb, :] = h_re
                hbi.at[g][rb, :] = h_im
                s_r = jnp.where(fwd, sre.at[g][rf, :], sre.at[g][rb, :])
                s_i = jnp.where(fwd, sim.at[g][rf, :], sim.at[g][rb, :])
                ar = are_ref[g]
                ai = aim_ref[g]
                out.append(ar * h_re - ai * h_im + s_r)
                out.append(ar * h_im + ai * h_re + s_i)
            return tuple(out)

        return lax.fori_loop(0, nchunk, step, init)

    zero = jnp.zeros((BATCH, 2 * S5_STATE), F32)
    fin = scan(0, S5_CTX_CH, BATCH, (zero,) * (2 * S5_GB))
    lat0 = []
    for g in range(S5_GB):
        fr_ref[g] = fin[2 * g]
        fi_ref[g] = fin[2 * g + 1]
        lat0 += [h0r_ref[g], h0i_ref[g]]
    scan(S5_CTX_ROWS, S5_LAT_CH, DEC_BATCH, tuple(lat0))

    for g in range(S5_GB):
        h_re = jnp.where(fwd, hfr[g], hbr[g])
        h_im = jnp.where(fwd, hfi[g], hbi[g])
        y = jnp.dot(ug[g], m_ref[g].astype(BF16), preferred_element_type=F32)
        ys[g] = y + _bdot_nt(h_re, qre_ref[g]) + _bdot_nt(h_im, qim_ref[g])
    for half in range(2):
        w = jnp.concatenate([ys[g, :, half * half_w:(half + 1) * half_w] for g in range(S5_GB)], axis=1)
        w_hi = w.astype(BF16)
        w_lo = (w - w_hi.astype(F32)).astype(BF16)
        zo = sum(lax.dot_general(piece, perm, (((1,), (1,)), ((), ())), preferred_element_type=F32)
                 for piece in (w_hi, w_lo))
        for t in range(8):
            rows = pl.ds(half * 8 + t, S5_ROWS, stride=S5_CHUNK)
            y_ref[rows, :] = jax.nn.gelu(zo[:, t * 128:(t + 1) * 128] + d_ref[...] * u_ref[rows, :])


def _s5_scan(proj, perm, tables, h0_re, h0_im, d_row, layer):
    def lspec(*tail):
        n = len(tail)
        return pl.BlockSpec((None, S5_GB) + tail, lambda t: (layer, t) + (0,) * n)

    def gspec(*tail):
        n = len(tail)
        return pl.BlockSpec((S5_GB,) + tail, lambda t: (t,) + (0,) * n)

    st = 2 * S5_STATE
    return pl.pallas_call(
        _s5_kernel,
        grid=(S5_GROUPS // S5_GB,),
        in_specs=[pl.BlockSpec((N_ROWS, 128), lambda t: (0, COL_U // 128 + t)),
                  pl.BlockSpec((S5_PERM, S5_PERM), lambda t: (0, 0)),
                  lspec(S5_ROW, S5_ROW), lspec(S5_ROW, st), lspec(S5_ROW, st), lspec(S5_ROW, st),
                  lspec(S5_ROW, st), lspec(1, st), lspec(1, st), lspec(DEC_BATCH, st), lspec(DEC_BATCH, st),
                  pl.BlockSpec((1, 128), lambda t: (0, t))],
        out_specs=[pl.BlockSpec((N_ROWS, 128), lambda t: (0, t)), gspec(BATCH, st), gspec(BATCH, st)],
        out_shape=[jax.ShapeDtypeStruct((N_ROWS, S5_WIDTH), F32),
                   jax.ShapeDtypeStruct((S5_GROUPS, BATCH, st), F32),
                   jax.ShapeDtypeStruct((S5_GROUPS, BATCH, st), F32)],
        scratch_shapes=([pltpu.VMEM((S5_GB, S5_ROWS, S5_ROW), BF16)]
                        + [pltpu.VMEM((S5_GB, S5_ROWS, st), F32) for _ in range(6)]
                        + [pltpu.VMEM((S5_GB, S5_ROWS, S5_ROW), F32)]),
        compiler_params=_cparams("parallel"),
        name="s5_scan",
    )(proj, perm, *tables, h0_re, h0_im, d_row.reshape(1, -1))


PRE_ROWS = 128


def _conv_block(x_ref, w_ref, r0, seq, width):
    x = x_ref[pl.ds(r0, PRE_ROWS), :]
    prev = x_ref[pl.ds(jnp.maximum(r0 - 1, 0), 1), :]
    nxt = x_ref[pl.ds(jnp.minimum(r0 + PRE_ROWS, seq - 1), 1), :]
    rid = lax.broadcasted_iota(jnp.int32, (PRE_ROWS, 1), 0)
    pos = (r0 + rid) % width
    xm = jnp.where(rid == 0, prev, pltpu.roll(x, 1, 0))
    xm = jnp.where(pos == 0, 0.0, xm)
    xp = jnp.where(rid == PRE_ROWS - 1, nxt, pltpu.roll(x, PRE_ROWS - 1, 0))
    xp = jnp.where(pos == width - 1, 0.0, xp)
    return xm * w_ref[0:1, :] + x * w_ref[1:2, :] + xp * w_ref[2:3, :]


def _round_robin(problems):
    live = list(problems)
    while live:
        nxt = []
        for p in live:
            try:
                next(p)
                nxt.append(p)
            except StopIteration:
                pass
        live = nxt


def _tri(n, lower):
    r = lax.broadcasted_iota(jnp.int32, (n, n), 0)
    c = lax.broadcasted_iota(jnp.int32, (n, n), 1)
    return (r >= c) if lower else (r <= c)


SSD_PAIR = 2 * SSD_HEADDIM
SSD_PAIRS = SSD_HEADS // 2


def _ssd_kernel(xbc_ref, sm_ref, smt_ref, z_ref, cw_ref, cb_ref, dtb_r_ref, dtb_c_ref, alog_r_ref, alog_c_ref,
                dvec_ref, ng_ref, h0_ref, yn_ref, hf_ref, xs_scr, bc_scr, dac_scr, dar_scr, dtr_scr, h_scr, y_ref,
                *, seq, width):
    nck = seq // SSD_CHUNK

    def pre(bi, carry):
        r0 = pl.multiple_of(bi * PRE_ROWS, PRE_ROWS)
        rows = pl.ds(r0, PRE_ROWS)
        xc = _silu(_conv_block(xbc_ref, cw_ref, r0, seq, width) + cb_ref[...])
        xs = xc[:, :SSD_WIDTH]
        xs_scr[rows, :] = xs
        bc_scr[rows, :] = xc[:, SSD_WIDTH:]
        y_ref[rows, :] = dvec_ref[...] * xs
        dt_c = _softplus(sm_ref[rows, 0:16] + dtb_r_ref[...])
        dac_scr[rows, :] = dt_c * (-jnp.exp(alog_r_ref[...]))
        return carry

    lax.fori_loop(0, seq // PRE_ROWS, pre, 0)
    dt_r = _softplus(smt_ref[0:16, :] + dtb_c_ref[...])
    da_r = dt_r * (-jnp.exp(alog_c_ref[...]))
    for ck in range(nck):
        dtr_scr[ck] = dt_r[:, ck * SSD_CHUNK:(ck + 1) * SSD_CHUNK]
        dar_scr[ck] = da_r[:, ck * SSD_CHUNK:(ck + 1) * SSD_CHUNK]
    h_scr[...] = h0_ref[...]

    tril = _tri(SSD_CHUNK, True)
    triu = _tri(SSD_CHUNK, False)
    tril_b = tril.astype(BF16)
    triu_b = triu.astype(BF16)
    lo_half = lax.broadcasted_iota(jnp.int32, (1, SSD_PAIR), 1) < SSD_HEADDIM

    def chunk_problem(dirs, k):
        for d in dirs:
            c = k if d == 0 else nck - 1 - k
            r0 = pl.multiple_of(c * SSD_CHUNK, SSD_CHUNK)
            rows = pl.ds(r0, SSD_CHUNK)
            mask = tril if d == 0 else triu
            ac = _dot_exact_lhs(tril_b if d == 0 else triu_b, dac_scr[rows, :])
            at = _dot_exact_rhs(dar_scr[c], triu_b if d == 0 else tril_b)
            dt_row = dtr_scr[c]
            end = SSD_CHUNK - 1 if d == 0 else 0
            bcx = bc_scr[rows, :]
            gmat = []
            for g in range(SSD_GROUPS):
                bm = bcx[:, g * SSD_STATE:(g + 1) * SSD_STATE]
                cm = bcx[:, 2 * SSD_STATE + g * SSD_STATE:2 * SSD_STATE + (g + 1) * SSD_STATE]
                gmat.append((bm.T, cm, _bdot_nt(cm, bm)))
            yield
            for pr in range(SSD_PAIRS):
                bmt, cm, gm = gmat[pr // (SSD_PAIRS // SSD_GROUPS)]
                xpair = xs_scr[rows, pr * SSD_PAIR:(pr + 1) * SSD_PAIR]
                sc, bt, es, dec, xh = [], [], [], [], []
                for half in range(2):
                    ln = d * SSD_HEADS + 2 * pr + half
                    colb = jnp.broadcast_to(ac[:, ln:ln + 1], (SSD_CHUNK, SSD_CHUNK))
                    row = at[ln:ln + 1, :]
                    dtr = dt_row[ln:ln + 1, :]
                    seg = jnp.where(mask, jnp.exp(jnp.where(mask, colb - row, 0.0)), 0.0)
                    a_end = row[:, end:end + 1]
                    sc.append(gm * seg * dtr)
                    bt.append(bmt * (jnp.exp(a_end - row) * dtr))
                    es.append(jnp.exp(colb))
                    dec.append(jnp.exp(a_end))
                    xh.append(jnp.where(lo_half if half == 0 else jnp.logical_not(lo_half), xpair, 0.0))
                xst = jnp.concatenate(xh, axis=0)
                hs = h_scr[d, pr]
                y = _bdot(jnp.concatenate(sc, axis=1), xst)
                y = y + _bdot(cm, hs) * jnp.where(lo_half, es[0], es[1])
                y_ref[rows, pr * SSD_PAIR:(pr + 1) * SSD_PAIR] += y
                h_scr[d, pr] = (hs * jnp.where(lo_half, dec[0], dec[1])
                                + _bdot(jnp.concatenate(bt, axis=1), xst))
                yield

    def chunk_step(k, carry):
        _round_robin([chunk_problem((d,), k) for d in range(N_DIR)])
        return carry

    lax.fori_loop(0, nck, chunk_step, 0)
    hf_ref[...] = h_scr[...]

    def post(bi, carry):
        rows = pl.ds(pl.multiple_of(bi * PRE_ROWS, PRE_ROWS), PRE_ROWS)
        y = y_ref[rows, :] * _silu(z_ref[rows, :])
        y = y * lax.rsqrt(jnp.mean(y * y, axis=-1, keepdims=True) + EPS) * ng_ref[...]
        yn_ref[rows, :] = y.astype(BF16)
        return carry

    lax.fori_loop(0, seq // PRE_ROWS, post, 0)


def _ssd(proj, small_t, row0, nseq, seq, width, conv_w, conv_b, dt_bias, a_log, dvec, norm_g, h0):
    blk0 = row0 // seq
    nck = seq // SSD_CHUNK
    kern = functools.partial(_ssd_kernel, seq=seq, width=width)
    full = lambda *shape: pl.BlockSpec(shape, lambda b: (0,) * len(shape))
    dtb_r = dt_bias.reshape(1, 16)
    dtb_c = dt_bias.reshape(16, 1)
    al_r = a_log.reshape(1, 16)
    al_c = a_log.reshape(16, 1)
    st_spec = pl.BlockSpec((None, N_DIR, SSD_PAIRS, SSD_STATE, SSD_PAIR), lambda b: (b, 0, 0, 0, 0))
    return pl.pallas_call(
        kern,
        grid=(nseq,),
        in_specs=[
            pl.BlockSpec((seq, SSD_CONV_DIM), lambda b: (blk0 + b, COL_XBC // SSD_CONV_DIM)),
            pl.BlockSpec((seq, SMALL_W), lambda b: (blk0 + b, COL_SMALL // SMALL_W)),
            pl.BlockSpec((32, seq), lambda b: (0, blk0 + b)),
            pl.BlockSpec((seq, SSD_WIDTH), lambda b: (blk0 + b, COL_Z // SSD_WIDTH)),
            full(3, SSD_CONV_DIM), full(1, SSD_CONV_DIM), full(1, 16), full(16, 1), full(1, 16), full(16, 1),
            full(1, SSD_WIDTH), full(1, SSD_WIDTH), st_spec,
        ],
        out_specs=[pl.BlockSpec((seq, SSD_WIDTH), lambda b: (b, 0)), st_spec],
        out_shape=[jax.ShapeDtypeStruct((nseq * seq, SSD_WIDTH), BF16),
                   jax.ShapeDtypeStruct((nseq, N_DIR, SSD_PAIRS, SSD_STATE, SSD_PAIR), F32)],
        scratch_shapes=[
            pltpu.VMEM((seq, SSD_WIDTH), F32), pltpu.VMEM((seq, 4 * SSD_STATE), F32),
            pltpu.VMEM((seq, 16), F32), pltpu.VMEM((nck, 16, SSD_CHUNK), F32),
            pltpu.VMEM((nck, 16, SSD_CHUNK), F32),
            pltpu.VMEM((N_DIR, SSD_PAIRS, SSD_STATE, SSD_PAIR), F32),
            pltpu.VMEM((seq, SSD_WIDTH), F32),
        ],
        compiler_params=_cparams("parallel"),
        name="ssd_seq%d" % seq,
    )(proj, proj, small_t, proj, conv_w, conv_b.reshape(1, -1), dtb_r, dtb_c, al_r, al_c, dvec,
      norm_g.reshape(1, -1), h0)


def _ssd_state_to_pairs(h):
    lead = h.shape[:-3]
    t = h.reshape(lead + (SSD_PAIRS, 2, SSD_HEADDIM, SSD_STATE))
    t = jnp.moveaxis(t, -1, -3)
    return t.reshape(lead + (SSD_PAIRS, SSD_STATE, SSD_PAIR))


def _ssd_state_from_pairs(hp):
    lead = hp.shape[:-3]
    t = hp.reshape(lead + (SSD_PAIRS, SSD_STATE, 2, SSD_HEADDIM))
    t = jnp.moveaxis(t, -3, -1)
    return t.reshape(lead + (SSD_HEADS, SSD_HEADDIM, SSD_STATE))


DN_ST = DN_HEADS * DN_CHUNK
DN_PAR = 4


def _dn_kernel(qkv_ref, sm_ref, smt_ref, dg_ref, cw_ref, dtb_r_ref, alog_r_ref, dtb_c_ref, alog_c_ref, ng_ref,
               s0_ref, on_ref, sf_ref, q_scr, k_scr, v_scr, b_scr, g_scr, s_scr, u_scr, wq_scr, a_scr, kd_scr,
               gl_scr, grow_scr, o_ref, *, seq, width):
    nck = seq // DN_CHUNK
    g_rows = -jnp.exp(alog_c_ref[...]) * _softplus(smt_ref[24:32, :] + dtb_c_ref[...])
    for ck in range(nck):
        grow_scr[ck] = g_rows[:, ck * DN_CHUNK:(ck + 1) * DN_CHUNK]

    def pre(bi, carry):
        r0 = pl.multiple_of(bi * PRE_ROWS, PRE_ROWS)
        rows = pl.ds(r0, PRE_ROWS)
        xc = _silu(_conv_block(qkv_ref, cw_ref, r0, seq, width))
        for h in range(DN_HEADS):
            q = xc[:, h * DN_DK:(h + 1) * DN_DK]
            k = xc[:, DN_QK + h * DN_DK:DN_QK + (h + 1) * DN_DK]
            q_scr[rows, h * DN_DK:(h + 1) * DN_DK] = (
                q * lax.rsqrt(jnp.sum(q * q, axis=-1, keepdims=True) + EPS) * (DN_DK ** -0.5))
            k_scr[rows, h * DN_DK:(h + 1) * DN_DK] = (
                k * lax.rsqrt(jnp.sum(k * k, axis=-1, keepdims=True) + EPS))
        v_scr[rows, :] = xc[:, 2 * DN_QK:]
        b_scr[rows, :] = _sigmoid(sm_ref[rows, 16:24])
        g_scr[rows, :] = -jnp.exp(alog_r_ref[...]) * _softplus(sm_ref[rows, 24:32] + dtb_r_ref[...])
        o_ref[rows, :] = jnp.zeros((PRE_ROWS, DN_V), F32)
        return carry

    lax.fori_loop(0, seq // PRE_ROWS, pre, 0)
    s_scr[...] = s0_ref[...]

    r = lax.broadcasted_iota(jnp.int32, (DN_ST, DN_ST), 0)
    c = lax.broadcasted_iota(jnp.int32, (DN_ST, DN_ST), 1)
    same = (r // DN_CHUNK) == (c // DN_CHUNK)
    eye = (r == c).astype(F32)
    tril64 = _tri(DN_CHUNK, True).astype(BF16)
    triu64 = _tri(DN_CHUNK, False).astype(BF16)
    tj = lax.broadcasted_iota(jnp.int32, (DN_CHUNK, DN_ST), 0)
    ti = lax.broadcasted_iota(jnp.int32, (DN_CHUNK, DN_ST), 1) % DN_CHUNK
    cum_f = (tj <= ti).astype(BF16)
    cum_b = (tj >= ti).astype(BF16)

    def chunk_problem(d, ci):
        r0 = pl.multiple_of(ci * DN_CHUNK, DN_CHUNK)
        rows = pl.ds(r0, DN_CHUNK)
        incl = jnp.logical_and(same, (r >= c) if d == 0 else (r <= c))
        strict = jnp.logical_and(same, (r > c) if d == 0 else (r < c))
        gc_c = _dot_exact_lhs(tril64 if d == 0 else triu64, g_scr[rows, :])
        gc_t = _dot_exact_rhs(grow_scr[ci], cum_f if d == 0 else cum_b)
        gc_r = jnp.concatenate(
            [jnp.broadcast_to(gc_t[d * DN_HEADS + h:d * DN_HEADS + h + 1, :], (DN_CHUNK, DN_ST))
             for h in range(DN_HEADS)], axis=0)
        beta = b_scr[rows, :]
        end_row = DN_CHUNK - 1 if d == 0 else 0
        k_st, q_st, v_st, bt_st, gc_st, gl_st = [], [], [], [], [], []
        for h in range(DN_HEADS):
            ln = d * DN_HEADS + h
            k_st.append(k_scr[rows, h * DN_DK:(h + 1) * DN_DK])
            q_st.append(q_scr[rows, h * DN_DK:(h + 1) * DN_DK])
            v_st.append(v_scr[rows, h * DN_DV:(h + 1) * DN_DV])
            bt_st.append(beta[:, ln:ln + 1])
            col = gc_c[:, ln:ln + 1]
            gc_st.append(col)
            gl_st.append(col[end_row:end_row + 1, :])
        kst = jnp.concatenate(k_st, axis=0)
        qst = jnp.concatenate(q_st, axis=0)
        vst = jnp.concatenate(v_st, axis=0)
        bst = jnp.concatenate(bt_st, axis=0)
        gst = jnp.concatenate(gc_st, axis=0)
        decay = jnp.where(incl, jnp.exp(jnp.where(incl, gst - gc_r, 0.0)), 0.0)
        kb = kst * bst
        kstb = kst.astype(BF16)
        m = jnp.where(strict, _bdot_nt(kb, kstb) * decay, 0.0)
        attn = jnp.where(incl, _bdot_nt(qst, kstb) * decay, 0.0)
        a_scr[d, ci] = attn.astype(BF16)
        yield
        def wide(bd):
            return sum(bd[h * DN_CHUNK:(h + 1) * DN_CHUNK, :] for h in range(DN_HEADS))

        def block_diag(w):
            return jnp.where(same, jnp.concatenate([w] * DN_HEADS, axis=0), 0.0)

        m_w = wide(m)
        t_w = wide(eye) - m_w
        p_w = jnp.dot(m_w.astype(BF16), m.astype(BF16), preferred_element_type=F32)
        yield
        for lvl in range(5):
            p_bd = block_diag(p_w).astype(BF16)
            if lvl < 4:
                both = jnp.dot(jnp.concatenate([t_w, p_w], axis=0).astype(BF16), p_bd,
                               preferred_element_type=F32)
                t_w = t_w + both[:DN_CHUNK]
                p_w = both[DN_CHUNK:]
            else:
                t_w = t_w + jnp.dot(t_w.astype(BF16), p_bd, preferred_element_type=F32)
            yield
        tb = block_diag(t_w).astype(BF16)
        rhs = jnp.concatenate([vst * bst, kb * jnp.exp(gst)], axis=1)
        x0 = jnp.dot(tb, rhs.astype(BF16), preferred_element_type=F32)
        yield
        res = rhs - x0 - _dot3(m, x0)
        yield
        uw = x0 + jnp.dot(tb, res.astype(BF16), preferred_element_type=F32)
        yield
        qg = qst * jnp.exp(gst)
        u_scr[d, ci] = uw[:, :DN_DV]
        kdec = []
        for h in range(DN_HEADS):
            hs = slice(h * DN_CHUNK, (h + 1) * DN_CHUNK)
            wq_scr[d, ci, h] = jnp.concatenate([uw[hs, DN_DV:], qg[hs, :]], axis=0).astype(BF16)
            gl = gl_st[h]
            kdec.append(k_st[h] * jnp.exp(gl - gc_st[h]))
            gl_scr[d, ci, h:h + 1, :] = jnp.broadcast_to(jnp.exp(gl), (1, DN_DV))
        kd_scr[d, ci] = jnp.concatenate(kdec, axis=0).astype(BF16)

    def chunk_step(kk, carry):
        _round_robin([chunk_problem(d, kk * DN_PAR + j) for j in range(DN_PAR) for d in range(N_DIR)])
        return carry

    lax.fori_loop(0, nck // DN_PAR, chunk_step, 0)

    def state_problem(d, ci):
        rows = pl.ds(pl.multiple_of(ci * DN_CHUNK, DN_CHUNK), DN_CHUNK)
        s_old, vnew, qs_all = [], [], []
        for h in range(DN_HEADS):
            hs = slice(h * DN_CHUNK, (h + 1) * DN_CHUNK)
            s_h = s_scr[d, h]
            ws = jnp.dot(wq_scr[d, ci, h], s_h.astype(BF16), preferred_element_type=F32)
            s_old.append(s_h)
            vnew.append((u_scr[d, ci, hs, :] - ws[:DN_CHUNK]).astype(BF16))
            qs_all.append(ws[DN_CHUNK:])
        yield
        o_st = jnp.concatenate(qs_all, axis=0) + jnp.dot(
            a_scr[d, ci], jnp.concatenate(vnew, axis=0), preferred_element_type=F32)
        for h in range(DN_HEADS):
            hs = slice(h * DN_CHUNK, (h + 1) * DN_CHUNK)
            s_scr[d, h] = s_old[h] * gl_scr[d, ci, h:h + 1, :] + lax.dot_general(
                kd_scr[d, ci, hs, :], vnew[h], (((0,), (0,)), ((), ())), preferred_element_type=F32)
        yield
        for h in range(DN_HEADS):
            hs = slice(h * DN_CHUNK, (h + 1) * DN_CHUNK)
            o_ref[rows, h * DN_DV:(h + 1) * DN_DV] += o_st[hs, :]

    def state_step(kk, carry):
        _round_robin([state_problem(0, kk), state_problem(1, nck - 1 - kk)])
        return carry

    lax.fori_loop(0, nck, state_step, 0)
    sf_ref[...] = s_scr[...]

    def post(bi, carry):
        rows = pl.ds(pl.multiple_of(bi * PRE_ROWS, PRE_ROWS), PRE_ROWS)
        for h in range(DN_HEADS):
            cols = slice(h * DN_DV, (h + 1) * DN_DV)
            oh = o_ref[rows, cols]
            oh = oh * lax.rsqrt(jnp.mean(oh * oh, axis=-1, keepdims=True) + EPS) * ng_ref[...]
            on_ref[rows, cols] = (oh * _silu(dg_ref[rows, cols])).astype(BF16)
        return carry

    lax.fori_loop(0, seq // PRE_ROWS, post, 0)


def _dn(proj, small_t, row0, nseq, seq, width, conv_w, dt_bias, a_log, norm_g, s0):
    blk0 = row0 // seq
    nck = seq // DN_CHUNK
    kern = functools.partial(_dn_kernel, seq=seq, width=width)
    full = lambda *shape: pl.BlockSpec(shape, lambda b: (0,) * len(shape))
    dtb_r = dt_bias.reshape(1, 8)
    al_r = a_log.reshape(1, 8)
    dtb_c = dt_bias.reshape(8, 1)
    al_c = a_log.reshape(8, 1)
    return pl.pallas_call(
        kern,
        grid=(nseq,),
        in_specs=[
            pl.BlockSpec((seq, DN_CONV_DIM), lambda b: (blk0 + b, COL_QKV // DN_CONV_DIM)),
            pl.BlockSpec((seq, SMALL_W), lambda b: (blk0 + b, COL_SMALL // SMALL_W)),
            pl.BlockSpec((32, seq), lambda b: (0, blk0 + b)),
            pl.BlockSpec((seq, DN_V), lambda b: (blk0 + b, COL_DNG // DN_V)),
            full(3, DN_CONV_DIM), full(1, 8), full(1, 8), full(8, 1), full(8, 1), full(1, DN_DV),
            pl.BlockSpec((None, N_DIR, DN_HEADS, DN_DK, DN_DV), lambda b: (b, 0, 0, 0, 0)),
        ],
        out_specs=[
            pl.BlockSpec((seq, DN_V), lambda b: (b, 0)),
            pl.BlockSpec((None, N_DIR, DN_HEADS, DN_DK, DN_DV), lambda b: (b, 0, 0, 0, 0)),
        ],
        out_shape=[jax.ShapeDtypeStruct((nseq * seq, DN_V), BF16),
                   jax.ShapeDtypeStruct((nseq, N_DIR, DN_HEADS, DN_DK, DN_DV), F32)],
        scratch_shapes=[
            pltpu.VMEM((seq, DN_QK), F32), pltpu.VMEM((seq, DN_QK), F32), pltpu.VMEM((seq, DN_V), F32),
            pltpu.VMEM((seq, 8), F32), pltpu.VMEM((seq, 8), F32),
            pltpu.VMEM((N_DIR, DN_HEADS, DN_DK, DN_DV), F32),
            pltpu.VMEM((N_DIR, nck, DN_ST, DN_DV), F32),
            pltpu.VMEM((N_DIR, nck, DN_HEADS, 2 * DN_CHUNK, DN_DK), BF16),
            pltpu.VMEM((N_DIR, nck, DN_ST, DN_ST), BF16),
            pltpu.VMEM((N_DIR, nck, DN_ST, DN_DK), BF16),
            pltpu.VMEM((N_DIR, nck, 8, DN_DV), F32),
            pltpu.VMEM((nck, 8, DN_CHUNK), F32),
            pltpu.VMEM((seq, DN_V), F32),
        ],
        compiler_params=_cparams("parallel"),
        name="dn_seq%d" % seq,
    )(proj, proj, small_t, proj, conv_w, dtb_r, al_r, dtb_c, al_c, norm_g.reshape(1, -1), s0)


MRG_TM = 512
MRG_TD = 512


def _merge_kernel(x_ref, sh_ref, sc_ref, gt_ref, ng_ref, y5_ref, ysc_ref, ysl_ref,
                  odc_ref, odl_ref, wg0_ref, wg1_ref, wg2_ref, glu_ref, swo_ref,
                  dwo_ref, wo_ref, o_ref, h_scr, a_scr, b_scr, c_scr, acc_scr):
    i = pl.program_id(0)
    j = pl.program_id(1)
    grp = _row_group(i, MRG_TM)

    @pl.when(j == 0)
    def _():
        is_ctx = i < N_CTX // MRG_TM
        h = _norm_mod(x_ref[...], ng_ref[...], sc_ref[pl.ds(grp, 1), :], sh_ref[pl.ds(grp, 1), :])
        h_scr[...] = h.astype(BF16)
        a_scr[...] = y5_ref[...].astype(BF16)
        b_scr[...] = jnp.where(is_ctx, ysc_ref[...], ysl_ref[...])
        c_scr[...] = jnp.where(is_ctx, odc_ref[...], odl_ref[...])
        acc_scr[...] = jnp.zeros_like(acc_scr)

    h = h_scr[...]
    nt = (((1,), (1,)), ((), ()))
    gate = lambda w_ref: _sigmoid(lax.dot_general(h, w_ref[...], nt, preferred_element_type=F32))
    g5 = a_scr[...]
    br_a = (jnp.dot(g5, glu_ref[0], preferred_element_type=F32)
            * _sigmoid(jnp.dot(g5, glu_ref[1], preferred_element_type=F32)))
    br_b = jnp.dot(b_scr[...], swo_ref[...], preferred_element_type=F32)
    br_c = jnp.dot(c_scr[...], dwo_ref[...], preferred_element_type=F32)
    merged = gate(wg0_ref) * br_a + gate(wg1_ref) * br_b + gate(wg2_ref) * br_c
    acc_scr[...] += jnp.dot(merged.astype(BF16), wo_ref[...], preferred_element_type=F32)

    @pl.when(j == pl.num_programs(1) - 1)
    def _():
        o_ref[...] = x_ref[...] + gt_ref[pl.ds(grp, 1), :] * acc_scr[...]


def _merge(x, mod, norm_g_row, w_packed, y5, ys_c, ys_l, od_c, od_l, glu_b, swo_b, dwo_b, wo_b, layer):
    nctx = N_CTX // MRG_TM
    rowblk = lambda w, col: pl.BlockSpec((MRG_TM, w), lambda i, j: (i, col // w))
    ctxblk = lambda w: pl.BlockSpec((MRG_TM, w), lambda i, j: (jnp.minimum(i, nctx - 1), 0))
    latblk = lambda w: pl.BlockSpec((MRG_TM, w), lambda i, j: (jnp.maximum(i - nctx, 0), 0))
    row1 = lambda w: pl.BlockSpec((1, w), lambda i, j: (0, 0))
    modblk = lambda k: pl.BlockSpec((8, D_MODEL), lambda i, j: (0, k))
    gateblk = lambda k: pl.BlockSpec((None, MRG_TD, D_MODEL),
                                     lambda i, j: (layer, (COL_GATES + k * D_MODEL) // MRG_TD + j, 0))
    return pl.pallas_call(
        _merge_kernel,
        grid=(N_ROWS // MRG_TM, D_MODEL // MRG_TD),
        in_specs=[
            rowblk(D_MODEL, 0), modblk(3), modblk(4), modblk(5), row1(D_MODEL),
            rowblk(S5_WIDTH, 0),
            ctxblk(SSD_WIDTH), latblk(SSD_WIDTH), ctxblk(DN_V), latblk(DN_V),
            gateblk(0), gateblk(1), gateblk(2),
            pl.BlockSpec((None, 2, S5_WIDTH, MRG_TD), lambda i, j: (layer, 0, 0, j)),
            pl.BlockSpec((None, SSD_WIDTH, MRG_TD), lambda i, j: (layer, 0, j)),
            pl.BlockSpec((None, DN_V, MRG_TD), lambda i, j: (layer, 0, j)),
            pl.BlockSpec((None, MRG_TD, D_MODEL), lambda i, j: (layer, j, 0)),
        ],
        out_specs=rowblk(D_MODEL, 0),
        out_shape=jax.ShapeDtypeStruct((N_ROWS, D_MODEL), F32),
        scratch_shapes=[pltpu.VMEM((MRG_TM, D_MODEL), BF16), pltpu.VMEM((MRG_TM, S5_WIDTH), BF16),
                        pltpu.VMEM((MRG_TM, SSD_WIDTH), BF16), pltpu.VMEM((MRG_TM, DN_V), BF16),
                        pltpu.VMEM((MRG_TM, D_MODEL), F32)],
        compiler_params=_cparams("parallel", "arbitrary"),
        name="merge",
    )(x, mod, mod, mod, norm_g_row, y5, ys_c, ys_l, od_c, od_l,
      w_packed, w_packed, w_packed, glu_b, swo_b, dwo_b, wo_b)


FIN_TM = 1024


def _final_norm_kernel(x_ref, g_ref, oc_ref, ol_ref):
    i = pl.program_id(0)
    x = x_ref[...]
    y = x * lax.rsqrt(jnp.mean(x * x, axis=-1, keepdims=True) + EPS) * g_ref[...]

    @pl.when(i < N_CTX // FIN_TM)
    def _():
        oc_ref[...] = y

    @pl.when(i >= N_CTX // FIN_TM)
    def _():
        ol_ref[...] = y


def _final_norm(x, g):
    nctx = N_CTX // FIN_TM
    return pl.pallas_call(
        _final_norm_kernel,
        grid=(N_ROWS // FIN_TM,),
        in_specs=[pl.BlockSpec((FIN_TM, D_MODEL), lambda i: (i, 0)), pl.BlockSpec((1, D_MODEL), lambda i: (0, 0))],
        out_specs=[pl.BlockSpec((FIN_TM, D_MODEL), lambda i: (jnp.minimum(i, nctx - 1), 0)),
                   pl.BlockSpec((FIN_TM, D_MODEL), lambda i: (jnp.maximum(i - nctx, 0), 0))],
        out_shape=[jax.ShapeDtypeStruct((N_CTX, D_MODEL), F32), jax.ShapeDtypeStruct((N_LAT, D_MODEL), F32)],
        compiler_params=_cparams("arbitrary"),
        name="final_norm",
    )(x, g.reshape(1, -1))


def kernel(x_prompt, x_sample, state_s5_re, state_s5_im, state_ssd, state_dn, c, c_ctx, ada_w, ada_b, norm_g, ffn_wi, ffn_wo, w_in, s5_lam_re, s5_lam_im, s5_log_dt, s5_b_re, s5_b_im, s5_c_re, s5_c_im, s5_d, s5_glu, ssd_conv_w, ssd_conv_b, ssd_dt_bias, ssd_a_log, ssd_d, ssd_norm_g, ssd_w_out, dn_conv_w, dn_dt_bias, dn_a_log, dn_norm_g, dn_w_out, w_out, final_norm_g):
    x = jnp.concatenate([x_prompt.reshape(N_CTX, D_MODEL), x_sample.reshape(N_LAT, D_MODEL)], axis=0)
    cond8 = jnp.concatenate([c_ctx[None, :], c, jnp.zeros((8 - 1 - DEC_BATCH, D_MODEL), F32)], axis=0)
    mods = _ada_mods(cond8, ada_w, ada_b)

    s5_tables = _s5_prep(s5_lam_re, s5_lam_im, s5_log_dt, s5_b_re, s5_b_im, s5_c_re, s5_c_im)
    s5_perm = _s5_perm()
    w_packed = _repack_w_in(w_in)
    glu_b = s5_glu.astype(BF16)
    swo_b = ssd_w_out.astype(BF16)
    dwo_b = dn_w_out.astype(BF16)
    wo_b = w_out.astype(BF16)
    ssd_d_rows = jnp.repeat(ssd_d, SSD_HEADDIM, axis=1).reshape(DEPTH, 1, SSD_WIDTH)

    def s5_h0(state):
        return state.transpose(1, 3, 0, 2, 4).reshape(DEPTH, S5_GROUPS, DEC_BATCH, 2 * S5_STATE)

    h0_re = s5_h0(state_s5_re)
    h0_im = s5_h0(state_s5_im)
    ssd_h0_lat = _ssd_state_to_pairs(jnp.swapaxes(state_ssd, 0, 1))
    ssd_h0_ctx = jnp.zeros((BATCH, N_DIR, SSD_PAIRS, SSD_STATE, SSD_PAIR), F32)
    dn_s0_lat = jnp.swapaxes(state_dn, 0, 1)
    dn_s0_ctx = jnp.zeros((BATCH, N_DIR, DN_HEADS, DN_DK, DN_DV), F32)

    new_s5_re, new_s5_im, new_ssd, new_dn = [], [], [], []
    for l in range(DEPTH):
        mod = mods[l]
        x = _ffn(x, mod, norm_g[l, 0:1], ffn_wi, ffn_wo, l, 0)
        proj, small_t = _inproj(x, mod, norm_g[l, 1:2], w_packed, l)

        y5, f_re, f_im = _s5_scan(proj, s5_perm, s5_tables, h0_re, h0_im, s5_d[l], l)
        new_s5_re.append(f_re.reshape(S5_GROUPS, BATCH, N_DIR, S5_STATE).transpose(1, 2, 0, 3))
        new_s5_im.append(f_im.reshape(S5_GROUPS, BATCH, N_DIR, S5_STATE).transpose(1, 2, 0, 3))

        ys_c, hs_c = _ssd(proj, small_t, 0, BATCH, SEQ, SEQ, ssd_conv_w[l], ssd_conv_b[l], ssd_dt_bias[l],
                          ssd_a_log[l], ssd_d_rows[l], ssd_norm_g[l], ssd_h0_ctx)
        ys_l, _ = _ssd(proj, small_t, N_CTX, DEC_BATCH, DEC_SEQ, GRID_W, ssd_conv_w[l], ssd_conv_b[l],
                       ssd_dt_bias[l], ssd_a_log[l], ssd_d_rows[l], ssd_norm_g[l], ssd_h0_lat[l])
        new_ssd.append(_ssd_state_from_pairs(hs_c))
        od_c, sd_c = _dn(proj, small_t, 0, BATCH, SEQ, SEQ, dn_conv_w[l], dn_dt_bias[l], dn_a_log[l],
                         dn_norm_g[l], dn_s0_ctx)
        od_l, _ = _dn(proj, small_t, N_CTX, DEC_BATCH, DEC_SEQ, GRID_W, dn_conv_w[l], dn_dt_bias[l],
                      dn_a_log[l], dn_norm_g[l], dn_s0_lat[l])
        new_dn.append(sd_c)

        x = _merge(x, mod, norm_g[l, 1:2], w_packed, y5, ys_c, ys_l, od_c, od_l, glu_b, swo_b,
                   dwo_b, wo_b, l)
        x = _ffn(x, mod, norm_g[l, 2:3], ffn_wi, ffn_wo, l, 1)

    y_ctx, y_lat = _final_norm(x, final_norm_g)
    y_prompt = y_ctx.reshape(BATCH, SEQ, D_MODEL)
    y_sample = y_lat.reshape(DEC_BATCH, DEC_SEQ, D_MODEL)
    return (y_prompt, y_sample, jnp.stack(new_s5_re, axis=1), jnp.stack(new_s5_im, axis=1),
            jnp.stack(new_ssd, axis=1), jnp.stack(new_dn, axis=1))
```

```python
import functools

import jax
import jax.numpy as jnp
import numpy as np
from jax import lax
from jax.experimental import pallas as pl
from jax.experimental.pallas import tpu as pltpu

F32 = jnp.float32
BF16 = jnp.bfloat16

D_MODEL = 1024
BATCH = 16
SEQ = 256
DEPTH = 4
DEC_BATCH = 2
DEC_SEQ = 1024
GRID_W = 64
N_DIR = 2
N_ADA = 9
D_FF = 2816
EPS = 1e-6

S5_WIDTH = 512
S5_GROUP = 16
S5_GROUPS = 32
S5_STATE = 64
S5_CHUNK = 16

SSD_WIDTH = 512
SSD_HEADDIM = 64
SSD_HEADS = 8
SSD_GROUPS = 2
SSD_STATE = 64
SSD_CHUNK = 128
SSD_CONV_DIM = 768

DN_HEADS = 4
DN_DK = 128
DN_DV = 128
DN_QK = 512
DN_V = 512
DN_CHUNK = 64
DN_CONV_DIM = 1536

IN_SEGMENTS = (512, 512, 768, 16, 1536, 8, 8, 512, 3072)
IN_SPLITS = tuple(int(s) for s in np.cumsum(IN_SEGMENTS)[:-1])

N_CTX = BATCH * SEQ
N_LAT = DEC_BATCH * DEC_SEQ
N_ROWS = N_CTX + N_LAT

COL_QKV = 0
COL_U = 1536
COL_Z = 2048
COL_DNG = 2560
COL_XBC = 3072
COL_SMALL = 3840
PROJ_W = 4096
COL_GATES = PROJ_W
PACK_W = PROJ_W + 3 * D_MODEL
SMALL_W = 128

VMEM_LIMIT = 56 * 1024 * 1024


def _cparams(*sem):
    return pltpu.CompilerParams(dimension_semantics=sem, vmem_limit_bytes=VMEM_LIMIT)


def _sigmoid(x):
    return 0.5 * (jnp.tanh(0.5 * x) + 1.0)


def _silu(x):
    return x * _sigmoid(x)


def _softplus(x):
    return jnp.maximum(x, 0.0) + jnp.log(1.0 + jnp.exp(-jnp.abs(x)))


def _bdot(a, b):
    return jnp.dot(a.astype(BF16), b.astype(BF16), preferred_element_type=F32)


def _bdot_nt(a, b):
    return lax.dot_general(a.astype(BF16), b.astype(BF16), (((1,), (1,)), ((), ())),
                           preferred_element_type=F32)


def _bdot_tn(a, b):
    return lax.dot_general(a.astype(BF16), b.astype(BF16), (((0,), (0,)), ((), ())),
                           preferred_element_type=F32)


def _split3(a):
    hi = a.astype(BF16)
    r = a - hi.astype(F32)
    mid = r.astype(BF16)
    lo = (r - mid.astype(F32)).astype(BF16)
    return hi, mid, lo


def _dot3(a, b):
    ah = a.astype(BF16)
    al = (a - ah.astype(F32)).astype(BF16)
    bh = b.astype(BF16)
    bl = (b - bh.astype(F32)).astype(BF16)
    out = jnp.dot(ah, bh, preferred_element_type=F32)
    out = out + jnp.dot(ah, bl, preferred_element_type=F32)
    out = out + jnp.dot(al, bh, preferred_element_type=F32)
    return out


def _dot_exact_lhs(t_bf16, x):
    hi, mid, lo = _split3(x)
    out = jnp.dot(t_bf16, hi, preferred_element_type=F32)
    out = out + jnp.dot(t_bf16, mid, preferred_element_type=F32)
    out = out + jnp.dot(t_bf16, lo, preferred_element_type=F32)
    return out


def _dot_exact_rhs(x, t_bf16):
    hi, mid, lo = _split3(x)
    out = jnp.dot(hi, t_bf16, preferred_element_type=F32)
    out = out + jnp.dot(mid, t_bf16, preferred_element_type=F32)
    out = out + jnp.dot(lo, t_bf16, preferred_element_type=F32)
    return out


def _norm_mod(x, g, sc, sh):
    ms = jnp.mean(x * x, axis=-1, keepdims=True)
    y = x * lax.rsqrt(ms + EPS) * g
    return y * (1.0 + sc) + sh


def _row_group(i, tm):
    nctx = N_CTX // tm
    per = DEC_SEQ // tm
    return jnp.where(i < nctx, 0, 1 + jnp.maximum(i - nctx, 0) // per)


ADA_TN = 2304


def _ada_kernel(c_ref, w_ref, b_ref, o_ref):
    c = c_ref[...]
    o_ref[...] = _bdot(_silu(c), w_ref[...]) + b_ref[...]


def _ada_mods(cond8, ada_w, ada_b):
    nj = (N_ADA * D_MODEL) // ADA_TN
    return pl.pallas_call(
        _ada_kernel,
        grid=(DEPTH, nj),
        in_specs=[
            pl.BlockSpec((8, D_MODEL), lambda l, j: (0, 0)),
            pl.BlockSpec((None, D_MODEL, ADA_TN), lambda l, j: (l, 0, j)),
            pl.BlockSpec((None, 1, ADA_TN), lambda l, j: (l, 0, j)),
        ],
        out_specs=pl.BlockSpec((None, 8, ADA_TN), lambda l, j: (l, 0, j)),
        out_shape=jax.ShapeDtypeStruct((DEPTH, 8, N_ADA * D_MODEL), F32),
        compiler_params=_cparams("parallel", "parallel"),
        name="ada_mods",
    )(cond8, ada_w, ada_b.reshape(DEPTH, 1, N_ADA * D_MODEL))


FFN_TM = 2048
FFN_TF = 256
MOD_ROWS = 1024
FFN_SUB = FFN_TM // MOD_ROWS


def _ffn_kernel(x_ref, sh_ref, sc_ref, gt_ref, g_ref, wa_ref, wb_ref, wo_ref, o_ref, h_scr, acc_scr):
    i = pl.program_id(0)
    j = pl.program_id(1)

    @pl.when(j == 0)
    def _():
        for s in range(FFN_SUB):
            rows = slice(s * MOD_ROWS, (s + 1) * MOD_ROWS)
            grp = _row_group(i * FFN_SUB + s, MOD_ROWS)
            h = _norm_mod(x_ref[rows, :], g_ref[...], sc_ref[pl.ds(grp, 1), :], sh_ref[pl.ds(grp, 1), :])
            h_scr[rows, :] = h.astype(BF16)
        acc_scr[...] = jnp.zeros_like(acc_scr)

    h = h_scr[...]
    a = jnp.dot(h, wa_ref[...].astype(BF16), preferred_element_type=F32)
    b = jnp.dot(h, wb_ref[...].astype(BF16), preferred_element_type=F32)
    u = (_silu(a) * b).astype(BF16)
    acc_scr[...] += jnp.dot(u, wo_ref[...].astype(BF16), preferred_element_type=F32)

    @pl.when(j == pl.num_programs(1) - 1)
    def _():
        for s in range(FFN_SUB):
            rows = slice(s * MOD_ROWS, (s + 1) * MOD_ROWS)
            grp = _row_group(i * FFN_SUB + s, MOD_ROWS)
            o_ref[rows, :] = x_ref[rows, :] + (0.5 * gt_ref[pl.ds(grp, 1), :]) * acc_scr[rows, :]


def _ffn(x, mod, norm_g_row, ffn_wi, ffn_wo, layer, which):
    nf = D_FF // FFN_TF
    base = 0 if which == 0 else 6
    return pl.pallas_call(
        _ffn_kernel,
        grid=(N_ROWS // FFN_TM, nf),
        in_specs=[
            pl.BlockSpec((FFN_TM, D_MODEL), lambda i, j: (i, 0)),
            pl.BlockSpec((8, D_MODEL), lambda i, j: (0, base)),
            pl.BlockSpec((8, D_MODEL), lambda i, j: (0, base + 1)),
            pl.BlockSpec((8, D_MODEL), lambda i, j: (0, base + 2)),
            pl.BlockSpec((1, D_MODEL), lambda i, j: (0, 0)),
            pl.BlockSpec((None, None, D_MODEL, FFN_TF), lambda i, j: (layer, which, 0, j)),
            pl.BlockSpec((None, None, D_MODEL, FFN_TF), lambda i, j: (layer, which, 0, j + nf)),
            pl.BlockSpec((None, None, FFN_TF, D_MODEL), lambda i, j: (layer, which, j, 0)),
        ],
        out_specs=pl.BlockSpec((FFN_TM, D_MODEL), lambda i, j: (i, 0)),
        out_shape=jax.ShapeDtypeStruct((N_ROWS, D_MODEL), F32),
        scratch_shapes=[pltpu.VMEM((FFN_TM, D_MODEL), BF16), pltpu.VMEM((FFN_TM, D_MODEL), F32)],
        compiler_params=_cparams("parallel", "arbitrary"),
        name="ffn",
    )(x, mod, mod, mod, norm_g_row, ffn_wi, ffn_wi, ffn_wo)


INP_TM = 512


def _inproj_kernel(x_ref, sh_ref, sc_ref, g_ref, w_ref, o_ref, ot_ref):
    grp = _row_group(pl.program_id(0), INP_TM)
    h = _norm_mod(x_ref[...], g_ref[...], sc_ref[pl.ds(grp, 1), :], sh_ref[pl.ds(grp, 1), :])
    res = lax.dot_general(h.astype(BF16), w_ref[...], (((1,), (1,)), ((), ())), preferred_element_type=F32)
    o_ref[...] = res
    ot_ref[...] = res[:, COL_SMALL:COL_SMALL + SMALL_W].T


RPK_LANES = 256
RPK_MOVES = ((COL_QKV, 1808, 1536), (COL_U, 0, 512), (COL_Z, 512, 512), (COL_DNG, 3360, 512),
             (COL_XBC, 1024, 768), (COL_SMALL, 1792, 16), (COL_SMALL + 16, 3344, 16), (COL_GATES, 3872, 3072))
RPK_PAD = (COL_SMALL + 32, PROJ_W)


def _repack_kernel(w_ref, o_ref):
    for dst, src, n in RPK_MOVES:
        o_ref[dst:dst + n, :] = w_ref[src:src + n, :].astype(BF16)
    o_ref[RPK_PAD[0]:RPK_PAD[1], :] = jnp.zeros((RPK_PAD[1] - RPK_PAD[0], RPK_LANES), BF16)


def _repack_w_in(w_in):
    w_t = jnp.swapaxes(w_in, 1, 2)
    in_w = w_t.shape[1]
    return pl.pallas_call(
        _repack_kernel,
        grid=(DEPTH, D_MODEL // RPK_LANES),
        in_specs=[pl.BlockSpec((None, in_w, RPK_LANES), lambda l, i: (l, 0, i))],
        out_specs=pl.BlockSpec((None, PACK_W, RPK_LANES), lambda l, i: (l, 0, i)),
        out_shape=jax.ShapeDtypeStruct((DEPTH, PACK_W, D_MODEL), BF16),
        compiler_params=_cparams("parallel", "parallel"),
        name="repack_w_in",
    )(w_t)


def _inproj(x, mod, norm_g_row, w_packed, layer):
    return pl.pallas_call(
        _inproj_kernel,
        grid=(N_ROWS // INP_TM,),
        in_specs=[
            pl.BlockSpec((INP_TM, D_MODEL), lambda i: (i, 0)),
            pl.BlockSpec((8, D_MODEL), lambda i: (0, 3)),
            pl.BlockSpec((8, D_MODEL), lambda i: (0, 4)),
            pl.BlockSpec((1, D_MODEL), lambda i: (0, 0)),
            pl.BlockSpec((None, PROJ_W, D_MODEL), lambda i: (layer, 0, 0)),
        ],
        out_specs=[pl.BlockSpec((INP_TM, PROJ_W), lambda i: (i, 0)),
                   pl.BlockSpec((SMALL_W, INP_TM), lambda i: (0, i))],
        out_shape=[jax.ShapeDtypeStruct((N_ROWS, PROJ_W), F32),
                   jax.ShapeDtypeStruct((SMALL_W, N_ROWS), F32)],
        compiler_params=_cparams("parallel"),
        name="inproj",
    )(x, mod, mod, norm_g_row, w_packed)


S5_ROW = S5_CHUNK * S5_GROUP


S5_PREP_G = 2


def _s5_prep_kernel(*refs):
    staged = [_s5_prep_group(*(r.at[g] for r in refs)) for g in range(S5_PREP_G)]
    _round_robin([p for problems, _ in staged for p in problems])
    for _, finish in staged:
        finish()


def _s5_prep_group(lam_re_ref, lam_im_ref, ldt_ref, btr_ref, bti_ref, cr_ref, ci_ref, ctr_ref, cti_ref,
                   m_ref, pre_ref, pim_ref, qre_ref, qim_ref, are_ref, aim_ref):
    tau = lax.broadcasted_iota(jnp.int32, (S5_CHUNK, 1), 0).astype(F32)
    lane = lax.broadcasted_iota(jnp.int32, (S5_GROUP, S5_ROW), 1)
    lane_t = lane[0:1, :] // S5_GROUP
    rep = (lax.broadcasted_iota(jnp.int32, (S5_GROUP, S5_ROW), 0) == lane % S5_GROUP).astype(BF16)
    taps = [None, None]

    def outer(ar, ai, xr, xi):
        rr = ar[:, None, :] * xr[None, :, :] - ai[:, None, :] * xi[None, :, :]
        ii = ar[:, None, :] * xi[None, :, :] + ai[:, None, :] * xr[None, :, :]
        return rr.reshape(S5_ROW, S5_STATE), ii.reshape(S5_ROW, S5_STATE)

    def direction(d):
        lr = lam_re_ref[d:d + 1, :]
        li = lam_im_ref[d:d + 1, :]
        dt = jnp.exp(ldt_ref[d:d + 1, :])
        mag = jnp.exp(lr * dt)
        lb_re = mag * jnp.cos(li * dt)
        lb_im = mag * jnp.sin(li * dt)
        den = lr * lr + li * li
        cr = ((lb_re - 1.0) * lr + lb_im * li) / den
        ci = (lb_im * lr - (lb_re - 1.0) * li) / den
        bt_r = btr_ref[d]
        bt_i = bti_ref[d]
        bbt_r = cr * bt_r - ci * bt_i
        bbt_i = cr * bt_i + ci * bt_r
        c_r = cr_ref[d]
        c_i = ci_ref[d]

        def powtab(t):
            m = jnp.exp(t * (lr * dt))
            ang = t * (li * dt)
            return m * jnp.cos(ang), m * jnp.sin(ang)

        t_in = (S5_CHUNK - 1) - tau if d == 0 else tau
        ar, ai = powtab(t_in)
        ba_r, ba_i = outer(ar, ai, bbt_r, bbt_i)
        lanes = slice(d * S5_STATE, (d + 1) * S5_STATE)
        pre_ref[:, lanes] = ba_r.astype(BF16)
        pim_ref[:, lanes] = ba_i.astype(BF16)
        yield
        kt = _dot3(ba_r, ctr_ref[d]) - _dot3(ba_i, cti_ref[d])
        yield
        taps[d] = _dot_exact_rhs(kt, rep)
        yield
        t_out = tau + 1.0 if d == 0 else S5_CHUNK - tau
        ar, ai = powtab(t_out)
        qr, qi = outer(ar, ai, c_r, c_i)
        qre_ref[:, lanes] = qr.astype(BF16)
        qim_ref[:, lanes] = (-qi).astype(BF16)
        a16r, a16i = powtab(jnp.full((1, 1), float(S5_CHUNK), F32))
        are_ref[:, lanes] = a16r
        aim_ref[:, lanes] = a16i

    def finish():
        last = S5_ROW - S5_GROUP
        table = jnp.concatenate(
            [taps[0][:last, :], taps[0][last:, :] + taps[1][:S5_GROUP, :], taps[1][S5_GROUP:, :]], axis=0)
        mmat = table[last:last + S5_ROW, :]
        for t in range(1, S5_CHUNK):
            start = (S5_CHUNK - 1 - t) * S5_GROUP
            mmat = jnp.where(lane_t == t, table[start:start + S5_ROW, :], mmat)
        m_ref[...] = mmat.astype(BF16)

    return [direction(d) for d in range(N_DIR)], finish


def _s5_prep(lam_re, lam_im, log_dt, b_re, b_im, c_re, c_im):
    tg = lambda t: jnp.swapaxes(t, 1, 2)
    lam_re_g = tg(lam_re)
    lam_im_g = tg(lam_im)
    ldt_g = tg(log_dt)[..., None]
    bt_r = jnp.swapaxes(tg(b_re), -1, -2)
    bt_i = jnp.swapaxes(tg(b_im), -1, -2)
    c_r = tg(c_re)
    c_i = tg(c_im)
    ct_r = jnp.swapaxes(c_r, -1, -2)
    ct_i = jnp.swapaxes(c_i, -1, -2)

    def spec(*tail):
        n = len(tail)
        return pl.BlockSpec((None, S5_PREP_G) + tail, lambda l, g: (l, g) + (0,) * n)

    st = 2 * S5_STATE
    tab = jax.ShapeDtypeStruct((DEPTH, S5_GROUPS, S5_ROW, st), BF16)
    dec = jax.ShapeDtypeStruct((DEPTH, S5_GROUPS, 1, st), F32)
    return pl.pallas_call(
        _s5_prep_kernel,
        grid=(DEPTH, S5_GROUPS // S5_PREP_G),
        in_specs=[spec(2, 64), spec(2, 64), spec(2, 1), spec(2, 16, 64), spec(2, 16, 64),
                  spec(2, 16, 64), spec(2, 16, 64), spec(2, 64, S5_GROUP), spec(2, 64, S5_GROUP)],
        out_specs=[spec(S5_ROW, S5_ROW), spec(S5_ROW, st), spec(S5_ROW, st), spec(S5_ROW, st), spec(S5_ROW, st),
                   spec(1, st), spec(1, st)],
        out_shape=[jax.ShapeDtypeStruct((DEPTH, S5_GROUPS, S5_ROW, S5_ROW), BF16), tab, tab, tab, tab, dec, dec],
        compiler_params=_cparams("parallel", "parallel"),
        name="s5_prep",
    )(lam_re_g, lam_im_g, ldt_g, bt_r, bt_i, c_r, c_i, ct_r, ct_i)


S5_CTX_CH = SEQ // S5_CHUNK
S5_LAT_CH = DEC_SEQ // S5_CHUNK
S5_CTX_ROWS = S5_CTX_CH * BATCH
S5_LAT_ROWS = S5_LAT_CH * DEC_BATCH
S5_ROWS = S5_CTX_ROWS + S5_LAT_ROWS
S5_GB = 128 // S5_GROUP
S5_PERM = S5_GB * 128


def _s5_perm():
    src = np.arange(S5_PERM)
    s, g, j = src // 128, (src % 128) // S5_GROUP, src % S5_GROUP
    p = np.zeros((S5_PERM, S5_PERM), np.float32)
    p[src, g * 128 + s * S5_GROUP + j] = 1.0
    return jnp.asarray(p, BF16)


def _s5_kernel(u_ref, perm_ref, m_ref, pre_ref, pim_ref, qre_ref, qim_ref, are_ref, aim_ref, h0r_ref, h0i_ref,
               d_ref, y_ref, fr_ref, fi_ref, ug, sre, sim, hfr, hfi, hbr, hbi, ys):
    perm = perm_ref[...]
    half_w = S5_ROW // 2
    for half in range(2):
        x = jnp.concatenate([u_ref[pl.ds(half * 8 + s, S5_ROWS, stride=S5_CHUNK), :] for s in range(8)], axis=1)
        z = jnp.dot(x.astype(BF16), perm, preferred_element_type=F32)
        for g in range(S5_GB):
            ug[g, :, half * half_w:(half + 1) * half_w] = z[:, g * 128:(g + 1) * 128].astype(BF16)
    for g in range(S5_GB):
        sre[g] = jnp.dot(ug[g], pre_ref[g].astype(BF16), preferred_element_type=F32)
        sim[g] = jnp.dot(ug[g], pim_ref[g].astype(BF16), preferred_element_type=F32)
    fwd = lax.broadcasted_iota(jnp.int32, (1, 2 * S5_STATE), 1) < S5_STATE

    def scan(base, nchunk, nseq, init):
        def step(k, hs):
            rf = pl.ds(base + k, nseq, stride=nchunk)
            rb = pl.ds(base + nchunk - 1 - k, nseq, stride=nchunk)
            out = []
            for g in range(S5_GB):
                h_re, h_im = hs[2 * g], hs[2 * g + 1]
                hfr.at[g][rf, :] = h_re
                hfi.at[g][rf, :] = h_im
                hbr.at[g][rb, :] = h_re
                hbi.at[g][rb, :] = h_im
                s_r = jnp.where(fwd, sre.at[g][rf, :], sre.at[g][rb, :])
                s_i = jnp.where(fwd, sim.at[g][rf, :], sim.at[g][rb, :])
                ar = are_ref[g]
                ai = aim_ref[g]
                out.append(ar * h_re - ai * h_im + s_r)
                out.append(ar * h_im + ai * h_re + s_i)
            return tuple(out)

        return lax.fori_loop(0, nchunk, step, init)

    zero = jnp.zeros((BATCH, 2 * S5_STATE), F32)
    fin = scan(0, S5_CTX_CH, BATCH, (zero,) * (2 * S5_GB))
    lat0 = []
    for g in range(S5_GB):
        fr_ref[g] = fin[2 * g]
        fi_ref[g] = fin[2 * g + 1]
        lat0 += [h0r_ref[g], h0i_ref[g]]
    scan(S5_CTX_ROWS, S5_LAT_CH, DEC_BATCH, tuple(lat0))

    for g in range(S5_GB):
        h_re = jnp.where(fwd, hfr[g], hbr[g])
        h_im = jnp.where(fwd, hfi[g], hbi[g])
        y = jnp.dot(ug[g], m_ref[g].astype(BF16), preferred_element_type=F32)
        ys[g] = y + _bdot_nt(h_re, qre_ref[g]) + _bdot_nt(h_im, qim_ref[g])
    for half in range(2):
        w = jnp.concatenate([ys[g, :, half * half_w:(half + 1) * half_w] for g in range(S5_GB)], axis=1)
        w_hi = w.astype(BF16)
        w_lo = (w - w_hi.astype(F32)).astype(BF16)
        zo = sum(lax.dot_general(piece, perm, (((1,), (1,)), ((), ())), preferred_element_type=F32)
                 for piece in (w_hi, w_lo))
        for t in range(8):
            rows = pl.ds(half * 8 + t, S5_ROWS, stride=S5_CHUNK)
            y_ref[rows, :] = jax.nn.gelu(zo[:, t * 128:(t + 1) * 128] + d_ref[...] * u_ref[rows, :])


def _s5_scan(proj, perm, tables, h0_re, h0_im, d_row, layer):
    def lspec(*tail):
        n = len(tail)
        return pl.BlockSpec((None, S5_GB) + tail, lambda t: (layer, t) + (0,) * n)

    def gspec(*tail):
        n = len(tail)
        return pl.BlockSpec((S5_GB,) + tail, lambda t: (t,) + (0,) * n)

    st = 2 * S5_STATE
    return pl.pallas_call(
        _s5_kernel,
        grid=(S5_GROUPS // S5_GB,),
        in_specs=[pl.BlockSpec((N_ROWS, 128), lambda t: (0, COL_U // 128 + t)),
                  pl.BlockSpec((S5_PERM, S5_PERM), lambda t: (0, 0)),
                  lspec(S5_ROW, S5_ROW), lspec(S5_ROW, st), lspec(S5_ROW, st), lspec(S5_ROW, st),
                  lspec(S5_ROW, st), lspec(1, st), lspec(1, st), lspec(DEC_BATCH, st), lspec(DEC_BATCH, st),
                  pl.BlockSpec((1, 128), lambda t: (0, t))],
        out_specs=[pl.BlockSpec((N_ROWS, 128), lambda t: (0, t)), gspec(BATCH, st), gspec(BATCH, st)],
        out_shape=[jax.ShapeDtypeStruct((N_ROWS, S5_WIDTH), F32),
                   jax.ShapeDtypeStruct((S5_GROUPS, BATCH, st), F32),
                   jax.ShapeDtypeStruct((S5_GROUPS, BATCH, st), F32)],
        scratch_shapes=([pltpu.VMEM((S5_GB, S5_ROWS, S5_ROW), BF16)]
                        + [pltpu.VMEM((S5_GB, S5_ROWS, st), F32) for _ in range(6)]
                        + [pltpu.VMEM((S5_GB, S5_ROWS, S5_ROW), F32)]),
        compiler_params=_cparams("parallel"),
        name="s5_scan",
    )(proj, perm, *tables, h0_re, h0_im, d_row.reshape(1, -1))


PRE_ROWS = 128


def _conv_block(x_ref, w_ref, r0, seq, width):
    x = x_ref[pl.ds(r0, PRE_ROWS), :]
    prev = x_ref[pl.ds(jnp.maximum(r0 - 1, 0), 1), :]
    nxt = x_ref[pl.ds(jnp.minimum(r0 + PRE_ROWS, seq - 1), 1), :]
    rid = lax.broadcasted_iota(jnp.int32, (PRE_ROWS, 1), 0)
    pos = (r0 + rid) % width
    xm = jnp.where(rid == 0, prev, pltpu.roll(x, 1, 0))
    xm = jnp.where(pos == 0, 0.0, xm)
    xp = jnp.where(rid == PRE_ROWS - 1, nxt, pltpu.roll(x, PRE_ROWS - 1, 0))
    xp = jnp.where(pos == width - 1, 0.0, xp)
    return xm * w_ref[0:1, :] + x * w_ref[1:2, :] + xp * w_ref[2:3, :]


def _round_robin(problems):
    live = list(problems)
    while live:
        nxt = []
        for p in live:
            try:
                next(p)
                nxt.append(p)
            except StopIteration:
                pass
        live = nxt


def _tri(n, lower):
    r = lax.broadcasted_iota(jnp.int32, (n, n), 0)
    c = lax.broadcasted_iota(jnp.int32, (n, n), 1)
    return (r >= c) if lower else (r <= c)


SSD_PAIR = 2 * SSD_HEADDIM
SSD_PAIRS = SSD_HEADS // 2


def _ssd_kernel(xbc_ref, sm_ref, smt_ref, z_ref, cw_ref, cb_ref, dtb_r_ref, dtb_c_ref, alog_r_ref, alog_c_ref,
                dvec_ref, ng_ref, h0_ref, yn_ref, hf_ref, xs_scr, bc_scr, dac_scr, dar_scr, dtr_scr, h_scr, y_ref,
                *, seq, width):
    nck = seq // SSD_CHUNK

    def pre(bi, carry):
        r0 = pl.multiple_of(bi * PRE_ROWS, PRE_ROWS)
        rows = pl.ds(r0, PRE_ROWS)
        xc = _silu(_conv_block(xbc_ref, cw_ref, r0, seq, width) + cb_ref[...])
        xs = xc[:, :SSD_WIDTH]
        xs_scr[rows, :] = xs
        bc_scr[rows, :] = xc[:, SSD_WIDTH:]
        y_ref[rows, :] = dvec_ref[...] * xs
        dt_c = _softplus(sm_ref[rows, 0:16] + dtb_r_ref[...])
        dac_scr[rows, :] = dt_c * (-jnp.exp(alog_r_ref[...]))
        return carry

    lax.fori_loop(0, seq // PRE_ROWS, pre, 0)
    dt_r = _softplus(smt_ref[0:16, :] + dtb_c_ref[...])
    da_r = dt_r * (-jnp.exp(alog_c_ref[...]))
    for ck in range(nck):
        dtr_scr[ck] = dt_r[:, ck * SSD_CHUNK:(ck + 1) * SSD_CHUNK]
        dar_scr[ck] = da_r[:, ck * SSD_CHUNK:(ck + 1) * SSD_CHUNK]
    h_scr[...] = h0_ref[...]

    tril = _tri(SSD_CHUNK, True)
    triu = _tri(SSD_CHUNK, False)
    tril_b = tril.astype(BF16)
    triu_b = triu.astype(BF16)
    lo_half = lax.broadcasted_iota(jnp.int32, (1, SSD_PAIR), 1) < SSD_HEADDIM

    def chunk_problem(dirs, k):
        for d in dirs:
            c = k if d == 0 else nck - 1 - k
            r0 = pl.multiple_of(c * SSD_CHUNK, SSD_CHUNK)
            rows = pl.ds(r0, SSD_CHUNK)
            mask = tril if d == 0 else triu
            ac = _dot_exact_lhs(tril_b if d == 0 else triu_b, dac_scr[rows, :])
            at = _dot_exact_rhs(dar_scr[c], triu_b if d == 0 else tril_b)
            dt_row = dtr_scr[c]
            end = SSD_CHUNK - 1 if d == 0 else 0
            bcx = bc_scr[rows, :]
            gmat = []
            for g in range(SSD_GROUPS):
                bm = bcx[:, g * SSD_STATE:(g + 1) * SSD_STATE]
                cm = bcx[:, 2 * SSD_STATE + g * SSD_STATE:2 * SSD_STATE + (g + 1) * SSD_STATE]
                gmat.append((bm.T, cm, _bdot_nt(cm, bm)))
            yield
            for pr in range(SSD_PAIRS):
                bmt, cm, gm = gmat[pr // (SSD_PAIRS // SSD_GROUPS)]
                xpair = xs_scr[rows, pr * SSD_PAIR:(pr + 1) * SSD_PAIR]
                sc, bt, es, dec, xh = [], [], [], [], []
                for half in range(2):
                    ln = d * SSD_HEADS + 2 * pr + half
                    colb = jnp.broadcast_to(ac[:, ln:ln + 1], (SSD_CHUNK, SSD_CHUNK))
                    row = at[ln:ln + 1, :]
                    dtr = dt_row[ln:ln + 1, :]
                    seg = jnp.where(mask, jnp.exp(jnp.where(mask, colb - row, 0.0)), 0.0)
                    a_end = row[:, end:end + 1]
                    sc.append(gm * seg * dtr)
                    bt.append(bmt * (jnp.exp(a_end - row) * dtr))
                    es.append(jnp.exp(colb))
                    dec.append(jnp.exp(a_end))
                    xh.append(jnp.where(lo_half if half == 0 else jnp.logical_not(lo_half), xpair, 0.0))
                xst = jnp.concatenate(xh, axis=0)
                hs = h_scr[d, pr]
                y = _bdot(jnp.concatenate(sc, axis=1), xst)
                y = y + _bdot(cm, hs) * jnp.where(lo_half, es[0], es[1])
                y_ref[rows, pr * SSD_PAIR:(pr + 1) * SSD_PAIR] += y
                h_scr[d, pr] = (hs * jnp.where(lo_half, dec[0], dec[1])
                                + _bdot(jnp.concatenate(bt, axis=1), xst))
                yield

    def chunk_step(k, carry):
        _round_robin([chunk_problem((d,), k) for d in range(N_DIR)])
        return carry

    lax.fori_loop(0, nck, chunk_step, 0)
    hf_ref[...] = h_scr[...]

    def post(bi, carry):
        rows = pl.ds(pl.multiple_of(bi * PRE_ROWS, PRE_ROWS), PRE_ROWS)
        y = y_ref[rows, :] * _silu(z_ref[rows, :])
        y = y * lax.rsqrt(jnp.mean(y * y, axis=-1, keepdims=True) + EPS) * ng_ref[...]
        yn_ref[rows, :] = y.astype(BF16)
        return carry

    lax.fori_loop(0, seq // PRE_ROWS, post, 0)


def _ssd(proj, small_t, row0, nseq, seq, width, conv_w, conv_b, dt_bias, a_log, dvec, norm_g, h0):
    blk0 = row0 // seq
    nck = seq // SSD_CHUNK
    kern = functools.partial(_ssd_kernel, seq=seq, width=width)
    full = lambda *shape: pl.BlockSpec(shape, lambda b: (0,) * len(shape))
    dtb_r = dt_bias.reshape(1, 16)
    dtb_c = dt_bias.reshape(16, 1)
    al_r = a_log.reshape(1, 16)
    al_c = a_log.reshape(16, 1)
    st_spec = pl.BlockSpec((None, N_DIR, SSD_PAIRS, SSD_STATE, SSD_PAIR), lambda b: (b, 0, 0, 0, 0))
    return pl.pallas_call(
        kern,
        grid=(nseq,),
        in_specs=[
            pl.BlockSpec((seq, SSD_CONV_DIM), lambda b: (blk0 + b, COL_XBC // SSD_CONV_DIM)),
            pl.BlockSpec((seq, SMALL_W), lambda b: (blk0 + b, COL_SMALL // SMALL_W)),
            pl.BlockSpec((32, seq), lambda b: (0, blk0 + b)),
            pl.BlockSpec((seq, SSD_WIDTH), lambda b: (blk0 + b, COL_Z // SSD_WIDTH)),
            full(3, SSD_CONV_DIM), full(1, SSD_CONV_DIM), full(1, 16), full(16, 1), full(1, 16), full(16, 1),
            full(1, SSD_WIDTH), full(1, SSD_WIDTH), st_spec,
        ],
        out_specs=[pl.BlockSpec((seq, SSD_WIDTH), lambda b: (b, 0)), st_spec],
        out_shape=[jax.ShapeDtypeStruct((nseq * seq, SSD_WIDTH), BF16),
                   jax.ShapeDtypeStruct((nseq, N_DIR, SSD_PAIRS, SSD_STATE, SSD_PAIR), F32)],
        scratch_shapes=[
            pltpu.VMEM((seq, SSD_WIDTH), F32), pltpu.VMEM((seq, 4 * SSD_STATE), F32),
            pltpu.VMEM((seq, 16), F32), pltpu.VMEM((nck, 16, SSD_CHUNK), F32),
            pltpu.VMEM((nck, 16, SSD_CHUNK), F32),
            pltpu.VMEM((N_DIR, SSD_PAIRS, SSD_STATE, SSD_PAIR), F32),
            pltpu.VMEM((seq, SSD_WIDTH), F32),
        ],
        compiler_params=_cparams("parallel"),
        name="ssd_seq%d" % seq,
    )(proj, proj, small_t, proj, conv_w, conv_b.reshape(1, -1), dtb_r, dtb_c, al_r, al_c, dvec,
      norm_g.reshape(1, -1), h0)


def _ssd_state_to_pairs(h):
    lead = h.shape[:-3]
    t = h.reshape(lead + (SSD_PAIRS, 2, SSD_HEADDIM, SSD_STATE))
    t = jnp.moveaxis(t, -1, -3)
    return t.reshape(lead + (SSD_PAIRS, SSD_STATE, SSD_PAIR))


def _ssd_state_from_pairs(hp):
    lead = hp.shape[:-3]
    t = hp.reshape(lead + (SSD_PAIRS, SSD_STATE, 2, SSD_HEADDIM))
    t = jnp.moveaxis(t, -3, -1)
    return t.reshape(lead + (SSD_HEADS, SSD_HEADDIM, SSD_STATE))


DN_ST = DN_HEADS * DN_CHUNK
DN_PAR = 4


def _dn_kernel(qkv_ref, sm_ref, smt_ref, dg_ref, cw_ref, dtb_r_ref, alog_r_ref, dtb_c_ref, alog_c_ref, ng_ref,
               s0_ref, on_ref, sf_ref, q_scr, k_scr, v_scr, b_scr, g_scr, s_scr, u_scr, wq_scr, a_scr, kd_scr,
               gl_scr, grow_scr, o_ref, *, seq, width):
    nck = seq // DN_CHUNK
    g_rows = -jnp.exp(alog_c_ref[...]) * _softplus(smt_ref[24:32, :] + dtb_c_ref[...])
    for ck in range(nck):
        grow_scr[ck] = g_rows[:, ck * DN_CHUNK:(ck + 1) * DN_CHUNK]

    def pre(bi, carry):
        r0 = pl.multiple_of(bi * PRE_ROWS, PRE_ROWS)
        rows = pl.ds(r0, PRE_ROWS)
        xc = _silu(_conv_block(qkv_ref, cw_ref, r0, seq, width))
        for h in range(DN_HEADS):
            q = xc[:, h * DN_DK:(h + 1) * DN_DK]
            k = xc[:, DN_QK + h * DN_DK:DN_QK + (h + 1) * DN_DK]
            q_scr[rows, h * DN_DK:(h + 1) * DN_DK] = (
                q * lax.rsqrt(jnp.sum(q * q, axis=-1, keepdims=True) + EPS) * (DN_DK ** -0.5))
            k_scr[rows, h * DN_DK:(h + 1) * DN_DK] = (
                k * lax.rsqrt(jnp.sum(k * k, axis=-1, keepdims=True) + EPS))
        v_scr[rows, :] = xc[:, 2 * DN_QK:]
        b_scr[rows, :] = _sigmoid(sm_ref[rows, 16:24])
        g_scr[rows, :] = -jnp.exp(alog_r_ref[...]) * _softplus(sm_ref[rows, 24:32] + dtb_r_ref[...])
        o_ref[rows, :] = jnp.zeros((PRE_ROWS, DN_V), F32)
        return carry

    lax.fori_loop(0, seq // PRE_ROWS, pre, 0)
    s_scr[...] = s0_ref[...]

    r = lax.broadcasted_iota(jnp.int32, (DN_ST, DN_ST), 0)
    c = lax.broadcasted_iota(jnp.int32, (DN_ST, DN_ST), 1)
    same = (r // DN_CHUNK) == (c // DN_CHUNK)
    eye = (r == c).astype(F32)
    tril64 = _tri(DN_CHUNK, True).astype(BF16)
    triu64 = _tri(DN_CHUNK, False).astype(BF16)
    tj = lax.broadcasted_iota(jnp.int32, (DN_CHUNK, DN_ST), 0)
    ti = lax.broadcasted_iota(jnp.int32, (DN_CHUNK, DN_ST), 1) % DN_CHUNK
    cum_f = (tj <= ti).astype(BF16)
    cum_b = (tj >= ti).astype(BF16)

    def chunk_problem(d, ci):
        r0 = pl.multiple_of(ci * DN_CHUNK, DN_CHUNK)
        rows = pl.ds(r0, DN_CHUNK)
        incl = jnp.logical_and(same, (r >= c) if d == 0 else (r <= c))
        strict = jnp.logical_and(same, (r > c) if d == 0 else (r < c))
        gc_c = _dot_exact_lhs(tril64 if d == 0 else triu64, g_scr[rows, :])
        gc_t = _dot_exact_rhs(grow_scr[ci], cum_f if d == 0 else cum_b)
        gc_r = jnp.concatenate(
            [jnp.broadcast_to(gc_t[d * DN_HEADS + h:d * DN_HEADS + h + 1, :], (DN_CHUNK, DN_ST))
             for h in range(DN_HEADS)], axis=0)
        beta = b_scr[rows, :]
        end_row = DN_CHUNK - 1 if d == 0 else 0
        k_st, q_st, v_st, bt_st, gc_st, gl_st = [], [], [], [], [], []
        for h in range(DN_HEADS):
            ln = d * DN_HEADS + h
            k_st.append(k_scr[rows, h * DN_DK:(h + 1) * DN_DK])
            q_st.append(q_scr[rows, h * DN_DK:(h + 1) * DN_DK])
            v_st.append(v_scr[rows, h * DN_DV:(h + 1) * DN_DV])
            bt_st.append(beta[:, ln:ln + 1])
            col = gc_c[:, ln:ln + 1]
            gc_st.append(col)
            gl_st.append(col[end_row:end_row + 1, :])
        kst = jnp.concatenate(k_st, axis=0)
        qst = jnp.concatenate(q_st, axis=0)
        vst = jnp.concatenate(v_st, axis=0)
        bst = jnp.concatenate(bt_st, axis=0)
        gst = jnp.concatenate(gc_st, axis=0)
        decay = jnp.where(incl, jnp.exp(jnp.where(incl, gst - gc_r, 0.0)), 0.0)
        kb = kst * bst
        kstb = kst.astype(BF16)
        m = jnp.where(strict, _bdot_nt(kb, kstb) * decay, 0.0)
        attn = jnp.where(incl, _bdot_nt(qst, kstb) * decay, 0.0)
        a_scr[d, ci] = attn.astype(BF16)
        yield
        def wide(bd):
            return sum(bd[h * DN_CHUNK:(h + 1) * DN_CHUNK, :] for h in range(DN_HEADS))

        def block_diag(w):
            return jnp.where(same, jnp.concatenate([w] * DN_HEADS, axis=0), 0.0)

        m_w = wide(m)
        t_w = wide(eye) - m_w
        p_w = jnp.dot(m_w.astype(BF16), m.astype(BF16), preferred_element_type=F32)
        yield
        for lvl in range(5):
            p_bd = block_diag(p_w).astype(BF16)
            if lvl < 4:
                both = jnp.dot(jnp.concatenate([t_w, p_w], axis=0).astype(BF16), p_bd,
                               preferred_element_type=F32)
                t_w = t_w + both[:DN_CHUNK]
                p_w = both[DN_CHUNK:]
            else:
                t_w = t_w + jnp.dot(t_w.astype(BF16), p_bd, preferred_element_type=F32)
            yield
        tb = block_diag(t_w).astype(BF16)
        rhs = jnp.concatenate([vst * bst, kb * jnp.exp(gst)], axis=1)
        x0 = jnp.dot(tb, rhs.astype(BF16), preferred_element_type=F32)
        yield
        res = rhs - x0 - _dot3(m, x0)
        yield
        uw = x0 + jnp.dot(tb, res.astype(BF16), preferred_element_type=F32)
        yield
        qg = qst * jnp.exp(gst)
        u_scr[d, ci] = uw[:, :DN_DV]
        kdec = []
        for h in range(DN_HEADS):
            hs = slice(h * DN_CHUNK, (h + 1) * DN_CHUNK)
            wq_scr[d, ci, h] = jnp.concatenate([uw[hs, DN_DV:], qg[hs, :]], axis=0).astype(BF16)
            gl = gl_st[h]
            kdec.append(k_st[h] * jnp.exp(gl - gc_st[h]))
            gl_scr[d, ci, h:h + 1, :] = jnp.broadcast_to(jnp.exp(gl), (1, DN_DV))
        kd_scr[d, ci] = jnp.concatenate(kdec, axis=0).astype(BF16)

    def chunk_step(kk, carry):
        _round_robin([chunk_problem(d, kk * DN_PAR + j) for j in range(DN_PAR) for d in range(N_DIR)])
        return carry

    lax.fori_loop(0, nck // DN_PAR, chunk_step, 0)

    def state_problem(d, ci):
        rows = pl.ds(pl.multiple_of(ci * DN_CHUNK, DN_CHUNK), DN_CHUNK)
        s_old, vnew, qs_all = [], [], []
        for h in range(DN_HEADS):
            hs = slice(h * DN_CHUNK, (h + 1) * DN_CHUNK)
            s_h = s_scr[d, h]
            ws = jnp.dot(wq_scr[d, ci, h], s_h.astype(BF16), preferred_element_type=F32)
            s_old.append(s_h)
            vnew.append((u_scr[d, ci, hs, :] - ws[:DN_CHUNK]).astype(BF16))
            qs_all.append(ws[DN_CHUNK:])
        yield
        o_st = jnp.concatenate(qs_all, axis=0) + jnp.dot(
            a_scr[d, ci], jnp.concatenate(vnew, axis=0), preferred_element_type=F32)
        for h in range(DN_HEADS):
            hs = slice(h * DN_CHUNK, (h + 1) * DN_CHUNK)
            s_scr[d, h] = s_old[h] * gl_scr[d, ci, h:h + 1, :] + lax.dot_general(
                kd_scr[d, ci, hs, :], vnew[h], (((0,), (0,)), ((), ())), preferred_element_type=F32)
        yield
        for h in range(DN_HEADS):
            hs = slice(h * DN_CHUNK, (h + 1) * DN_CHUNK)
            o_ref[rows, h * DN_DV:(h + 1) * DN_DV] += o_st[hs, :]

    def state_step(kk, carry):
        _round_robin([state_problem(0, kk), state_problem(1, nck - 1 - kk)])
        return carry

    lax.fori_loop(0, nck, state_step, 0)
    sf_ref[...] = s_scr[...]

    def post(bi, carry):
        rows = pl.ds(pl.multiple_of(bi * PRE_ROWS, PRE_ROWS), PRE_ROWS)
        for h in range(DN_HEADS):
            cols = slice(h * DN_DV, (h + 1) * DN_DV)
            oh = o_ref[rows, cols]
            oh = oh * lax.rsqrt(jnp.mean(oh * oh, axis=-1, keepdims=True) + EPS) * ng_ref[...]
            on_ref[rows, cols] = (oh * _silu(dg_ref[rows, cols])).astype(BF16)
        return carry

    lax.fori_loop(0, seq // PRE_ROWS, post, 0)


def _dn(proj, small_t, row0, nseq, seq, width, conv_w, dt_bias, a_log, norm_g, s0):
    blk0 = row0 // seq
    nck = seq // DN_CHUNK
    kern = functools.partial(_dn_kernel, seq=seq, width=width)
    full = lambda *shape: pl.BlockSpec(shape, lambda b: (0,) * len(shape))
    dtb_r = dt_bias.reshape(1, 8)
    al_r = a_log.reshape(1, 8)
    dtb_c = dt_bias.reshape(8, 1)
    al_c = a_log.reshape(8, 1)
    return pl.pallas_call(
        kern,
        grid=(nseq,),
        in_specs=[
            pl.BlockSpec((seq, DN_CONV_DIM), lambda b: (blk0 + b, COL_QKV // DN_CONV_DIM)),
            pl.BlockSpec((seq, SMALL_W), lambda b: (blk0 + b, COL_SMALL // SMALL_W)),
            pl.BlockSpec((32, seq), lambda b: (0, blk0 + b)),
            pl.BlockSpec((seq, DN_V), lambda b: (blk0 + b, COL_DNG // DN_V)),
            full(3, DN_CONV_DIM), full(1, 8), full(1, 8), full(8, 1), full(8, 1), full(1, DN_DV),
            pl.BlockSpec((None, N_DIR, DN_HEADS, DN_DK, DN_DV), lambda b: (b, 0, 0, 0, 0)),
        ],
        out_specs=[
            pl.BlockSpec((seq, DN_V), lambda b: (b, 0)),
            pl.BlockSpec((None, N_DIR, DN_HEADS, DN_DK, DN_DV), lambda b: (b, 0, 0, 0, 0)),
        ],
        out_shape=[jax.ShapeDtypeStruct((nseq * seq, DN_V), BF16),
                   jax.ShapeDtypeStruct((nseq, N_DIR, DN_HEADS, DN_DK, DN_DV), F32)],
        scratch_shapes=[
            pltpu.VMEM((seq, DN_QK), F32), pltpu.VMEM((seq, DN_QK), F32), pltpu.VMEM((seq, DN_V), F32),
            pltpu.VMEM((seq, 8), F32), pltpu.VMEM((seq, 8), F32),
            pltpu.VMEM((N_DIR, DN_HEADS, DN_DK, DN_DV), F32),
            pltpu.VMEM((N_DIR, nck, DN_ST, DN_DV), F32),
            pltpu.VMEM((N_DIR, nck, DN_HEADS, 2 * DN_CHUNK, DN_DK), BF16),
            pltpu.VMEM((N_DIR, nck, DN_ST, DN_ST), BF16),
            pltpu.VMEM((N_DIR, nck, DN_ST, DN_DK), BF16),
            pltpu.VMEM((N_DIR, nck, 8, DN_DV), F32),
            pltpu.VMEM((nck, 8, DN_CHUNK), F32),
            pltpu.VMEM((seq, DN_V), F32),
        ],
        compiler_params=_cparams("parallel"),
        name="dn_seq%d" % seq,
    )(proj, proj, small_t, proj, conv_w, dtb_r, al_r, dtb_c, al_c, norm_g.reshape(1, -1), s0)


MRG_TM = 512
MRG_TD = 512


def _merge_kernel(x_ref, sh_ref, sc_ref, gt_ref, ng_ref, y5_ref, ysc_ref, ysl_ref,
                  odc_ref, odl_ref, wg0_ref, wg1_ref, wg2_ref, glu_ref, swo_ref,
                  dwo_ref, wo_ref, o_ref, h_scr, a_scr, b_scr, c_scr, acc_scr):
    i = pl.program_id(0)
    j = pl.program_id(1)
    grp = _row_group(i, MRG_TM)

    @pl.when(j == 0)
    def _():
        is_ctx = i < N_CTX // MRG_TM
        h = _norm_mod(x_ref[...], ng_ref[...], sc_ref[pl.ds(grp, 1), :], sh_ref[pl.ds(grp, 1), :])
        h_scr[...] = h.astype(BF16)
        a_scr[...] = y5_ref[...].astype(BF16)
        b_scr[...] = jnp.where(is_ctx, ysc_ref[...], ysl_ref[...])
        c_scr[...] = jnp.where(is_ctx, odc_ref[...], odl_ref[...])
        acc_scr[...] = jnp.zeros_like(acc_scr)

    h = h_scr[...]
    nt = (((1,), (1,)), ((), ()))
    gate = lambda w_ref: _sigmoid(lax.dot_general(h, w_ref[...], nt, preferred_element_type=F32))
    g5 = a_scr[...]
    br_a = (jnp.dot(g5, glu_ref[0], preferred_element_type=F32)
            * _sigmoid(jnp.dot(g5, glu_ref[1], preferred_element_type=F32)))
    br_b = jnp.dot(b_scr[...], swo_ref[...], preferred_element_type=F32)
    br_c = jnp.dot(c_scr[...], dwo_ref[...], preferred_element_type=F32)
    merged = gate(wg0_ref) * br_a + gate(wg1_ref) * br_b + gate(wg2_ref) * br_c
    acc_scr[...] += jnp.dot(merged.astype(BF16), wo_ref[...], preferred_element_type=F32)

    @pl.when(j == pl.num_programs(1) - 1)
    def _():
        o_ref[...] = x_ref[...] + gt_ref[pl.ds(grp, 1), :] * acc_scr[...]


def _merge(x, mod, norm_g_row, w_packed, y5, ys_c, ys_l, od_c, od_l, glu_b, swo_b, dwo_b, wo_b, layer):
    nctx = N_CTX // MRG_TM
    rowblk = lambda w, col: pl.BlockSpec((MRG_TM, w), lambda i, j: (i, col // w))
    ctxblk = lambda w: pl.BlockSpec((MRG_TM, w), lambda i, j: (jnp.minimum(i, nctx - 1), 0))
    latblk = lambda w: pl.BlockSpec((MRG_TM, w), lambda i, j: (jnp.maximum(i - nctx, 0), 0))
    row1 = lambda w: pl.BlockSpec((1, w), lambda i, j: (0, 0))
    modblk = lambda k: pl.BlockSpec((8, D_MODEL), lambda i, j: (0, k))
    gateblk = lambda k: pl.BlockSpec((None, MRG_TD, D_MODEL),
                                     lambda i, j: (layer, (COL_GATES + k * D_MODEL) // MRG_TD + j, 0))
    return pl.pallas_call(
        _merge_kernel,
        grid=(N_ROWS // MRG_TM, D_MODEL // MRG_TD),
        in_specs=[
            rowblk(D_MODEL, 0), modblk(3), modblk(4), modblk(5), row1(D_MODEL),
            rowblk(S5_WIDTH, 0),
            ctxblk(SSD_WIDTH), latblk(SSD_WIDTH), ctxblk(DN_V), latblk(DN_V),
            gateblk(0), gateblk(1), gateblk(2),
            pl.BlockSpec((None, 2, S5_WIDTH, MRG_TD), lambda i, j: (layer, 0, 0, j)),
            pl.BlockSpec((None, SSD_WIDTH, MRG_TD), lambda i, j: (layer, 0, j)),
            pl.BlockSpec((None, DN_V, MRG_TD), lambda i, j: (layer, 0, j)),
            pl.BlockSpec((None, MRG_TD, D_MODEL), lambda i, j: (layer, j, 0)),
        ],
        out_specs=rowblk(D_MODEL, 0),
        out_shape=jax.ShapeDtypeStruct((N_ROWS, D_MODEL), F32),
        scratch_shapes=[pltpu.VMEM((MRG_TM, D_MODEL), BF16), pltpu.VMEM((MRG_TM, S5_WIDTH), BF16),
                        pltpu.VMEM((MRG_TM, SSD_WIDTH), BF16), pltpu.VMEM((MRG_TM, DN_V), BF16),
                        pltpu.VMEM((MRG_TM, D_MODEL), F32)],
        compiler_params=_cparams("parallel", "arbitrary"),
        name="merge",
    )(x, mod, mod, mod, norm_g_row, y5, ys_c, ys_l, od_c, od_l,
      w_packed, w_packed, w_packed, glu_b, swo_b, dwo_b, wo_b)


FIN_TM = 1024


def _final_norm_kernel(x_ref, g_ref, oc_ref, ol_ref):
    i = pl.program_id(0)
    x = x_ref[...]
    y = x * lax.rsqrt(jnp.mean(x * x, axis=-1, keepdims=True) + EPS) * g_ref[...]

    @pl.when(i < N_CTX // FIN_TM)
    def _():
        oc_ref[...] = y

    @pl.when(i >= N_CTX // FIN_TM)
    def _():
        ol_ref[...] = y


def _final_norm(x, g):
    nctx = N_CTX // FIN_TM
    return pl.pallas_call(
        _final_norm_kernel,
        grid=(N_ROWS // FIN_TM,),
        in_specs=[pl.BlockSpec((FIN_TM, D_MODEL), lambda i: (i, 0)), pl.BlockSpec((1, D_MODEL), lambda i: (0, 0))],
        out_specs=[pl.BlockSpec((FIN_TM, D_MODEL), lambda i: (jnp.minimum(i, nctx - 1), 0)),
                   pl.BlockSpec((FIN_TM, D_MODEL), lambda i: (jnp.maximum(i - nctx, 0), 0))],
        out_shape=[jax.ShapeDtypeStruct((N_CTX, D_MODEL), F32), jax.ShapeDtypeStruct((N_LAT, D_MODEL), F32)],
        compiler_params=_cparams("arbitrary"),
        name="final_norm",
    )(x, g.reshape(1, -1))


def kernel(x_prompt, x_sample, state_s5_re, state_s5_im, state_ssd, state_dn, c, c_ctx, ada_w, ada_b, norm_g, ffn_wi, ffn_wo, w_in, s5_lam_re, s5_lam_im, s5_log_dt, s5_b_re, s5_b_im, s5_c_re, s5_c_im, s5_d, s5_glu, ssd_conv_w, ssd_conv_b, ssd_dt_bias, ssd_a_log, ssd_d, ssd_norm_g, ssd_w_out, dn_conv_w, dn_dt_bias, dn_a_log, dn_norm_g, dn_w_out, w_out, final_norm_g):
    x = jnp.concatenate([x_prompt.reshape(N_CTX, D_MODEL), x_sample.reshape(N_LAT, D_MODEL)], axis=0)
    cond8 = jnp.concatenate([c_ctx[None, :], c, jnp.zeros((8 - 1 - DEC_BATCH, D_MODEL), F32)], axis=0)
    mods = _ada_mods(cond8, ada_w, ada_b)

    s5_tables = _s5_prep(s5_lam_re, s5_lam_im, s5_log_dt, s5_b_re, s5_b_im, s5_c_re, s5_c_im)
    s5_perm = _s5_perm()
    w_packed = _repack_w_in(w_in)
    glu_b = s5_glu.astype(BF16)
    swo_b = ssd_w_out.astype(BF16)
    dwo_b = dn_w_out.astype(BF16)
    wo_b = w_out.astype(BF16)
    ssd_d_rows = jnp.repeat(ssd_d, SSD_HEADDIM, axis=1).reshape(DEPTH, 1, SSD_WIDTH)

    def s5_h0(state):
        return state.transpose(1, 3, 0, 2, 4).reshape(DEPTH, S5_GROUPS, DEC_BATCH, 2 * S5_STATE)

    h0_re = s5_h0(state_s5_re)
    h0_im = s5_h0(state_s5_im)
    ssd_h0_lat = _ssd_state_to_pairs(jnp.swapaxes(state_ssd, 0, 1))
    ssd_h0_ctx = jnp.zeros((BATCH, N_DIR, SSD_PAIRS, SSD_STATE, SSD_PAIR), F32)
    dn_s0_lat = jnp.swapaxes(state_dn, 0, 1)
    dn_s0_ctx = jnp.zeros((BATCH, N_DIR, DN_HEADS, DN_DK, DN_DV), F32)

    new_s5_re, new_s5_im, new_ssd, new_dn = [], [], [], []
    for l in range(DEPTH):
        mod = mods[l]
        x = _ffn(x, mod, norm_g[l, 0:1], ffn_wi, ffn_wo, l, 0)
        proj, small_t = _inproj(x, mod, norm_g[l, 1:2], w_packed, l)

        y5, f_re, f_im = _s5_scan(proj, s5_perm, s5_tables, h0_re, h0_im, s5_d[l], l)
        new_s5_re.append(f_re.reshape(S5_GROUPS, BATCH, N_DIR, S5_STATE).transpose(1, 2, 0, 3))
        new_s5_im.append(f_im.reshape(S5_GROUPS, BATCH, N_DIR, S5_STATE).transpose(1, 2, 0, 3))

        ys_c, hs_c = _ssd(proj, small_t, 0, BATCH, SEQ, SEQ, ssd_conv_w[l], ssd_conv_b[l], ssd_dt_bias[l],
                          ssd_a_log[l], ssd_d_rows[l], ssd_norm_g[l], ssd_h0_ctx)
        ys_l, _ = _ssd(proj, small_t, N_CTX, DEC_BATCH, DEC_SEQ, GRID_W, ssd_conv_w[l], ssd_conv_b[l],
                       ssd_dt_bias[l], ssd_a_log[l], ssd_d_rows[l], ssd_norm_g[l], ssd_h0_lat[l])
        new_ssd.append(_ssd_state_from_pairs(hs_c))
        od_c, sd_c = _dn(proj, small_t, 0, BATCH, SEQ, SEQ, dn_conv_w[l], dn_dt_bias[l], dn_a_log[l],
                         dn_norm_g[l], dn_s0_ctx)
        od_l, _ = _dn(proj, small_t, N_CTX, DEC_BATCH, DEC_SEQ, GRID_W, dn_conv_w[l], dn_dt_bias[l],
                      dn_a_log[l], dn_norm_g[l], dn_s0_lat[l])
        new_dn.append(sd_c)

        x = _merge(x, mod, norm_g[l, 1:2], w_packed, y5, ys_c, ys_l, od_c, od_l, glu_b, swo_b,
                   dwo_b, wo_b, l)
        x = _ffn(x, mod, norm_g[l, 2:3], ffn_wi, ffn_wo, l, 1)

    y_ctx, y_lat = _final_norm(x, final_norm_g)
    y_prompt = y_ctx.reshape(BATCH, SEQ, D_MODEL)
    y_sample = y_lat.reshape(DEC_BATCH, DEC_SEQ, D_MODEL)
    return (y_prompt, y_sample, jnp.stack(new_s5_re, axis=1), jnp.stack(new_s5_im, axis=1),
            jnp.stack(new_ssd, axis=1), jnp.stack(new_dn, axis=1))
```

```python
import functools

import jax
import jax.numpy as jnp
import numpy as np
from jax import lax
from jax.experimental import pallas as pl
from jax.experimental.pallas import tpu as pltpu

F32 = jnp.float32
BF16 = jnp.bfloat16

D_MODEL = 1024
BATCH = 16
SEQ = 256
DEPTH = 4
DEC_BATCH = 2
DEC_SEQ = 1024
GRID_W = 64
N_DIR = 2
N_ADA = 9
D_FF = 2816
EPS = 1e-6

S5_WIDTH = 512
S5_GROUP = 16
S5_GROUPS = 32
S5_STATE = 64
S5_CHUNK = 16

SSD_WIDTH = 512
SSD_HEADDIM = 64
SSD_HEADS = 8
SSD_GROUPS = 2
SSD_STATE = 64
SSD_CHUNK = 128
SSD_CONV_DIM = 768

DN_HEADS = 4
DN_DK = 128
DN_DV = 128
DN_QK = 512
DN_V = 512
DN_CHUNK = 64
DN_CONV_DIM = 1536

IN_SEGMENTS = (512, 512, 768, 16, 1536, 8, 8, 512, 3072)
IN_SPLITS = tuple(int(s) for s in np.cumsum(IN_SEGMENTS)[:-1])

N_CTX = BATCH * SEQ
N_LAT = DEC_BATCH * DEC_SEQ
N_ROWS = N_CTX + N_LAT

COL_QKV = 0
COL_U = 1536
COL_Z = 2048
COL_DNG = 2560
COL_XBC = 3072
COL_SMALL = 3840
PROJ_W = 4096
COL_GATES = PROJ_W
PACK_W = PROJ_W + 3 * D_MODEL
SMALL_W = 128

VMEM_LIMIT = 56 * 1024 * 1024


def _cparams(*sem):
    return pltpu.CompilerParams(dimension_semantics=sem, vmem_limit_bytes=VMEM_LIMIT)


def _sigmoid(x):
    return 0.5 * (jnp.tanh(0.5 * x) + 1.0)


def _silu(x):
    return x * _sigmoid(x)


def _softplus(x):
    return jnp.maximum(x, 0.0) + jnp.log(1.0 + jnp.exp(-jnp.abs(x)))


def _bdot(a, b):
    return jnp.dot(a.astype(BF16), b.astype(BF16), preferred_element_type=F32)


def _bdot_nt(a, b):
    return lax.dot_general(a.astype(BF16), b.astype(BF16), (((1,), (1,)), ((), ())),
                           preferred_element_type=F32)


def _bdot_tn(a, b):
    return lax.dot_general(a.astype(BF16), b.astype(BF16), (((0,), (0,)), ((), ())),
                           preferred_element_type=F32)


def _split3(a):
    hi = a.astype(BF16)
    r = a - hi.astype(F32)
    mid = r.astype(BF16)
    lo = (r - mid.astype(F32)).astype(BF16)
    return hi, mid, lo


def _dot3(a, b):
    ah = a.astype(BF16)
    al = (a - ah.astype(F32)).astype(BF16)
    bh = b.astype(BF16)
    bl = (b - bh.astype(F32)).astype(BF16)
    out = jnp.dot(ah, bh, preferred_element_type=F32)
    out = out + jnp.dot(ah, bl, preferred_element_type=F32)
    out = out + jnp.dot(al, bh, preferred_element_type=F32)
    return out


def _dot_exact_lhs(t_bf16, x):
    hi, mid, lo = _split3(x)
    out = jnp.dot(t_bf16, hi, preferred_element_type=F32)
    out = out + jnp.dot(t_bf16, mid, preferred_element_type=F32)
    out = out + jnp.dot(t_bf16, lo, preferred_element_type=F32)
    return out


def _dot_exact_rhs(x, t_bf16):
    hi, mid, lo = _split3(x)
    out = jnp.dot(hi, t_bf16, preferred_element_type=F32)
    out = out + jnp.dot(mid, t_bf16, preferred_element_type=F32)
    out = out + jnp.dot(lo, t_bf16, preferred_element_type=F32)
    return out


def _norm_mod(x, g, sc, sh):
    ms = jnp.mean(x * x, axis=-1, keepdims=True)
    y = x * lax.rsqrt(ms + EPS) * g
    return y * (1.0 + sc) + sh


def _row_group(i, tm):
    nctx = N_CTX // tm
    per = DEC_SEQ // tm
    return jnp.where(i < nctx, 0, 1 + jnp.maximum(i - nctx, 0) // per)


ADA_TN = 2304


def _ada_kernel(c_ref, w_ref, b_ref, o_ref):
    c = c_ref[...]
    o_ref[...] = _bdot(_silu(c), w_ref[...]) + b_ref[...]


def _ada_mods(cond8, ada_w, ada_b):
    nj = (N_ADA * D_MODEL) // ADA_TN
    return pl.pallas_call(
        _ada_kernel,
        grid=(DEPTH, nj),
        in_specs=[
            pl.BlockSpec((8, D_MODEL), lambda l, j: (0, 0)),
            pl.BlockSpec((None, D_MODEL, ADA_TN), lambda l, j: (l, 0, j)),
            pl.BlockSpec((None, 1, ADA_TN), lambda l, j: (l, 0, j)),
        ],
        out_specs=pl.BlockSpec((None, 8, ADA_TN), lambda l, j: (l, 0, j)),
        out_shape=jax.ShapeDtypeStruct((DEPTH, 8, N_ADA * D_MODEL), F32),
        compiler_params=_cparams("parallel", "parallel"),
        name="ada_mods",
    )(cond8, ada_w, ada_b.reshape(DEPTH, 1, N_ADA * D_MODEL))


FFN_TM = 2048
FFN_TF = 256
MOD_ROWS = 1024
FFN_SUB = FFN_TM // MOD_ROWS


def _ffn_kernel(x_ref, sh_ref, sc_ref, gt_ref, g_ref, wa_ref, wb_ref, wo_ref, o_ref, h_scr, acc_scr):
    i = pl.program_id(0)
    j = pl.program_id(1)

    @pl.when(j == 0)
    def _():
        for s in range(FFN_SUB):
            rows = slice(s * MOD_ROWS, (s + 1) * MOD_ROWS)
            grp = _row_group(i * FFN_SUB + s, MOD_ROWS)
            h = _norm_mod(x_ref[rows, :], g_ref[...], sc_ref[pl.ds(grp, 1), :], sh_ref[pl.ds(grp, 1), :])
            h_scr[rows, :] = h.astype(BF16)
        acc_scr[...] = jnp.zeros_like(acc_scr)

    h = h_scr[...]
    a = jnp.dot(h, wa_ref[...].astype(BF16), preferred_element_type=F32)
    b = jnp.dot(h, wb_ref[...].astype(BF16), preferred_element_type=F32)
    u = (_silu(a) * b).astype(BF16)
    acc_scr[...] += jnp.dot(u, wo_ref[...].astype(BF16), preferred_element_type=F32)

    @pl.when(j == pl.num_programs(1) - 1)
    def _():
        for s in range(FFN_SUB):
            rows = slice(s * MOD_ROWS, (s + 1) * MOD_ROWS)
            grp = _row_group(i * FFN_SUB + s, MOD_ROWS)
            o_ref[rows, :] = x_ref[rows, :] + (0.5 * gt_ref[pl.ds(grp, 1), :]) * acc_scr[rows, :]


def _ffn(x, mod, norm_g_row, ffn_wi, ffn_wo, layer, which):
    nf = D_FF // FFN_TF
    base = 0 if which == 0 else 6
    return pl.pallas_call(
        _ffn_kernel,
        grid=(N_ROWS // FFN_TM, nf),
        in_specs=[
            pl.BlockSpec((FFN_TM, D_MODEL), lambda i, j: (i, 0)),
            pl.BlockSpec((8, D_MODEL), lambda i, j: (0, base)),
            pl.BlockSpec((8, D_MODEL), lambda i, j: (0, base + 1)),
            pl.BlockSpec((8, D_MODEL), lambda i, j: (0, base + 2)),
            pl.BlockSpec((1, D_MODEL), lambda i, j: (0, 0)),
            pl.BlockSpec((None, None, D_MODEL, FFN_TF), lambda i, j: (layer, which, 0, j)),
            pl.BlockSpec((None, None, D_MODEL, FFN_TF), lambda i, j: (layer, which, 0, j + nf)),
            pl.BlockSpec((None, None, FFN_TF, D_MODEL), lambda i, j: (layer, which, j, 0)),
        ],
        out_specs=pl.BlockSpec((FFN_TM, D_MODEL), lambda i, j: (i, 0)),
        out_shape=jax.ShapeDtypeStruct((N_ROWS, D_MODEL), F32),
        scratch_shapes=[pltpu.VMEM((FFN_TM, D_MODEL), BF16), pltpu.VMEM((FFN_TM, D_MODEL), F32)],
        compiler_params=_cparams("parallel", "arbitrary"),
        name="ffn",
    )(x, mod, mod, mod, norm_g_row, ffn_wi, ffn_wi, ffn_wo)


INP_TM = 512


def _inproj_kernel(x_ref, sh_ref, sc_ref, g_ref, w_ref, o_ref, ot_ref):
    grp = _row_group(pl.program_id(0), INP_TM)
    h = _norm_mod(x_ref[...], g_ref[...], sc_ref[pl.ds(grp, 1), :], sh_ref[pl.ds(grp, 1), :])
    res = lax.dot_general(h.astype(BF16), w_ref[...], (((1,), (1,)), ((), ())), preferred_element_type=F32)
    o_ref[...] = res
    ot_ref[...] = res[:, COL_SMALL:COL_SMALL + SMALL_W].T


RPK_LANES = 256
RPK_MOVES = ((COL_QKV, 1808, 1536), (COL_U, 0, 512), (COL_Z, 512, 512), (COL_DNG, 3360, 512),
             (COL_XBC, 1024, 768), (COL_SMALL, 1792, 16), (COL_SMALL + 16, 3344, 16), (COL_GATES, 3872, 3072))
RPK_PAD = (COL_SMALL + 32, PROJ_W)


def _repack_kernel(w_ref, o_ref):
    for dst, src, n in RPK_MOVES:
        o_ref[dst:dst + n, :] = w_ref[src:src + n, :].astype(BF16)
    o_ref[RPK_PAD[0]:RPK_PAD[1], :] = jnp.zeros((RPK_PAD[1] - RPK_PAD[0], RPK_LANES), BF16)


def _repack_w_in(w_in):
    w_t = jnp.swapaxes(w_in, 1, 2)
    in_w = w_t.shape[1]
    return pl.pallas_call(
        _repack_kernel,
        grid=(DEPTH, D_MODEL // RPK_LANES),
        in_specs=[pl.BlockSpec((None, in_w, RPK_LANES), lambda l, i: (l, 0, i))],
        out_specs=pl.BlockSpec((None, PACK_W, RPK_LANES), lambda l, i: (l, 0, i)),
        out_shape=jax.ShapeDtypeStruct((DEPTH, PACK_W, D_MODEL), BF16),
        compiler_params=_cparams("parallel", "parallel"),
        name="repack_w_in",
    )(w_t)


def _inproj(x, mod, norm_g_row, w_packed, layer):
    return pl.pallas_call(
        _inproj_kernel,
        grid=(N_ROWS // INP_TM,),
        in_specs=[
            pl.BlockSpec((INP_TM, D_MODEL), lambda i: (i, 0)),
            pl.BlockSpec((8, D_MODEL), lambda i: (0, 3)),
            pl.BlockSpec((8, D_MODEL), lambda i: (0, 4)),
            pl.BlockSpec((1, D_MODEL), lambda i: (0, 0)),
            pl.BlockSpec((None, PROJ_W, D_MODEL), lambda i: (layer, 0, 0)),
        ],
        out_specs=[pl.BlockSpec((INP_TM, PROJ_W), lambda i: (i, 0)),
                   pl.BlockSpec((SMALL_W, INP_TM), lambda i: (0, i))],
        out_shape=[jax.ShapeDtypeStruct((N_ROWS, PROJ_W), F32),
                   jax.ShapeDtypeStruct((SMALL_W, N_ROWS), F32)],
        compiler_params=_cparams("parallel"),
        name="inproj",
    )(x, mod, mod, norm_g_row, w_packed)


S5_ROW = S5_CHUNK * S5_GROUP


S5_PREP_G = 2


def _s5_prep_kernel(*refs):
    staged = [_s5_prep_group(*(r.at[g] for r in refs)) for g in range(S5_PREP_G)]
    _round_robin([p for problems, _ in staged for p in problems])
    for _, finish in staged:
        finish()


def _s5_prep_group(lam_re_ref, lam_im_ref, ldt_ref, btr_ref, bti_ref, cr_ref, ci_ref, ctr_ref, cti_ref,
                   m_ref, pre_ref, pim_ref, qre_ref, qim_ref, are_ref, aim_ref):
    tau = lax.broadcasted_iota(jnp.int32, (S5_CHUNK, 1), 0).astype(F32)
    lane = lax.broadcasted_iota(jnp.int32, (S5_GROUP, S5_ROW), 1)
    lane_t = lane[0:1, :] // S5_GROUP
    rep = (lax.broadcasted_iota(jnp.int32, (S5_GROUP, S5_ROW), 0) == lane % S5_GROUP).astype(BF16)
    taps = [None, None]

    def outer(ar, ai, xr, xi):
        rr = ar[:, None, :] * xr[None, :, :] - ai[:, None, :] * xi[None, :, :]
        ii = ar[:, None, :] * xi[None, :, :] + ai[:, None, :] * xr[None, :, :]
        return rr.reshape(S5_ROW, S5_STATE), ii.reshape(S5_ROW, S5_STATE)

    def direction(d):
        lr = lam_re_ref[d:d + 1, :]
        li = lam_im_ref[d:d + 1, :]
        dt = jnp.exp(ldt_ref[d:d + 1, :])
        mag = jnp.exp(lr * dt)
        lb_re = mag * jnp.cos(li * dt)
        lb_im = mag * jnp.sin(li * dt)
        den = lr * lr + li * li
        cr = ((lb_re - 1.0) * lr + lb_im * li) / den
        ci = (lb_im * lr - (lb_re - 1.0) * li) / den
        bt_r = btr_ref[d]
        bt_i = bti_ref[d]
        bbt_r = cr * bt_r - ci * bt_i
        bbt_i = cr * bt_i + ci * bt_r
        c_r = cr_ref[d]
        c_i = ci_ref[d]

        def powtab(t):
            m = jnp.exp(t * (lr * dt))
            ang = t * (li * dt)
            return m * jnp.cos(ang), m * jnp.sin(ang)

        t_in = (S5_CHUNK - 1) - tau if d == 0 else tau
        ar, ai = powtab(t_in)
        ba_r, ba_i = outer(ar, ai, bbt_r, bbt_i)
        lanes = slice(d * S5_STATE, (d + 1) * S5_STATE)
        pre_ref[:, lanes] = ba_r.astype(BF16)
        pim_ref[:, lanes] = ba_i.astype(BF16)
        yield
        kt = _dot3(ba_r, ctr_ref[d]) - _dot3(ba_i, cti_ref[d])
        yield
        taps[d] = _dot_exact_rhs(kt, rep)
        yield
        t_out = tau + 1.0 if d == 0 else S5_CHUNK - tau
        ar, ai = powtab(t_out)
        qr, qi = outer(ar, ai, c_r, c_i)
        qre_ref[:, lanes] = qr.astype(BF16)
        qim_ref[:, lanes] = (-qi).astype(BF16)
        a16r, a16i = powtab(jnp.full((1, 1), float(S5_CHUNK), F32))
        are_ref[:, lanes] = a16r
        aim_ref[:, lanes] = a16i

    def finish():
        last = S5_ROW - S5_GROUP
        table = jnp.concatenate(
            [taps[0][:last, :], taps[0][last:, :] + taps[1][:S5_GROUP, :], taps[1][S5_GROUP:, :]], axis=0)
        mmat = table[last:last + S5_ROW, :]
        for t in range(1, S5_CHUNK):
            start = (S5_CHUNK - 1 - t) * S5_GROUP
            mmat = jnp.where(lane_t == t, table[start:start + S5_ROW, :], mmat)
        m_ref[...] = mmat.astype(BF16)

    return [direction(d) for d in range(N_DIR)], finish


def _s5_prep(lam_re, lam_im, log_dt, b_re, b_im, c_re, c_im):
    tg = lambda t: jnp.swapaxes(t, 1, 2)
    lam_re_g = tg(lam_re)
    lam_im_g = tg(lam_im)
    ldt_g = tg(log_dt)[..., None]
    bt_r = jnp.swapaxes(tg(b_re), -1, -2)
    bt_i = jnp.swapaxes(tg(b_im), -1, -2)
    c_r = tg(c_re)
    c_i = tg(c_im)
    ct_r = jnp.swapaxes(c_r, -1, -2)
    ct_i = jnp.swapaxes(c_i, -1, -2)

    def spec(*tail):
        n = len(tail)
        return pl.BlockSpec((None, S5_PREP_G) + tail, lambda l, g: (l, g) + (0,) * n)

    st = 2 * S5_STATE
    tab = jax.ShapeDtypeStruct((DEPTH, S5_GROUPS, S5_ROW, st), BF16)
    dec = jax.ShapeDtypeStruct((DEPTH, S5_GROUPS, 1, st), F32)
    return pl.pallas_call(
        _s5_prep_kernel,
        grid=(DEPTH, S5_GROUPS // S5_PREP_G),
        in_specs=[spec(2, 64), spec(2, 64), spec(2, 1), spec(2, 16, 64), spec(2, 16, 64),
                  spec(2, 16, 64), spec(2, 16, 64), spec(2, 64, S5_GROUP), spec(2, 64, S5_GROUP)],
        out_specs=[spec(S5_ROW, S5_ROW), spec(S5_ROW, st), spec(S5_ROW, st), spec(S5_ROW, st), spec(S5_ROW, st),
                   spec(1, st), spec(1, st)],
        out_shape=[jax.ShapeDtypeStruct((DEPTH, S5_GROUPS, S5_ROW, S5_ROW), BF16), tab, tab, tab, tab, dec, dec],
        compiler_params=_cparams("parallel", "parallel"),
        name="s5_prep",
    )(lam_re_g, lam_im_g, ldt_g, bt_r, bt_i, c_r, c_i, ct_r, ct_i)


S5_CTX_CH = SEQ // S5_CHUNK
S5_LAT_CH = DEC_SEQ // S5_CHUNK
S5_CTX_ROWS = S5_CTX_CH * BATCH
S5_LAT_ROWS = S5_LAT_CH * DEC_BATCH
S5_ROWS = S5_CTX_ROWS + S5_LAT_ROWS
S5_GB = 128 // S5_GROUP
S5_PERM = S5_GB * 128


def _s5_perm():
    src = np.arange(S5_PERM)
    s, g, j = src // 128, (src % 128) // S5_GROUP, src % S5_GROUP
    p = np.zeros((S5_PERM, S5_PERM), np.float32)
    p[src, g * 128 + s * S5_GROUP + j] = 1.0
    return jnp.asarray(p, BF16)


def _s5_kernel(u_ref, perm_ref, m_ref, pre_ref, pim_ref, qre_ref, qim_ref, are_ref, aim_ref, h0r_ref, h0i_ref,
               d_ref, y_ref, fr_ref, fi_ref, ug, sre, sim, hfr, hfi, hbr, hbi, ys):
    perm = perm_ref[...]
    half_w = S5_ROW // 2
    for half in range(2):
        x = jnp.concatenate([u_ref[pl.ds(half * 8 + s, S5_ROWS, stride=S5_CHUNK), :] for s in range(8)], axis=1)
        z = jnp.dot(x.astype(BF16), perm, preferred_element_type=F32)
        for g in range(S5_GB):
            ug[g, :, half * half_w:(half + 1) * half_w] = z[:, g * 128:(g + 1) * 128].astype(BF16)
    for g in range(S5_GB):
        sre[g] = jnp.dot(ug[g], pre_ref[g].astype(BF16), preferred_element_type=F32)
        sim[g] = jnp.dot(ug[g], pim_ref[g].astype(BF16), preferred_element_type=F32)
    fwd = lax.broadcasted_iota(jnp.int32, (1, 2 * S5_STATE), 1) < S5_STATE

    def scan(base, nchunk, nseq, init):
        def step(k, hs):
            rf = pl.ds(base + k, nseq, stride=nchunk)
            rb = pl.ds(base + nchunk - 1 - k, nseq, stride=nchunk)
            out = []
            for g in range(S5_GB):
                h_re, h_im = hs[2 * g], hs[2 * g + 1]
                hfr.at[g][rf, :] = h_re
                hfi.at[g][rf, :] = h_im
                hbr.at[g][rb, :] = h_re
                hbi.at[g][rb, :] = h_im
                s_r = jnp.where(fwd, sre.at[g][rf, :], sre.at[g][rb, :])
                s_i = jnp.where(fwd, sim.at[g][rf, :], sim.at[g][rb, :])
                ar = are_ref[g]
                ai = aim_ref[g]
                out.append(ar * h_re - ai * h_im + s_r)
                out.append(ar * h_im + ai * h_re + s_i)
            return tuple(out)

        return lax.fori_loop(0, nchunk, step, init, unroll=4)

    zero = jnp.zeros((BATCH, 2 * S5_STATE), F32)
    fin = scan(0, S5_CTX_CH, BATCH, (zero,) * (2 * S5_GB))
    lat0 = []
    for g in range(S5_GB):
        fr_ref[g] = fin[2 * g]
        fi_ref[g] = fin[2 * g + 1]
        lat0 += [h0r_ref[g], h0i_ref[g]]
    scan(S5_CTX_ROWS, S5_LAT_CH, DEC_BATCH, tuple(lat0))

    for g in range(S5_GB):
        h_re = jnp.where(fwd, hfr[g], hbr[g])
        h_im = jnp.where(fwd, hfi[g], hbi[g])
        y = jnp.dot(ug[g], m_ref[g].astype(BF16), preferred_element_type=F32)
        ys[g] = y + _bdot_nt(h_re, qre_ref[g]) + _bdot_nt(h_im, qim_ref[g])
    for half in range(2):
        w = jnp.concatenate([ys[g, :, half * half_w:(half + 1) * half_w] for g in range(S5_GB)], axis=1)
        w_hi = w.astype(BF16)
        w_lo = (w - w_hi.astype(F32)).astype(BF16)
        zo = sum(lax.dot_general(piece, perm, (((1,), (1,)), ((), ())), preferred_element_type=F32)
                 for piece in (w_hi, w_lo))
        for t in range(8):
            rows = pl.ds(half * 8 + t, S5_ROWS, stride=S5_CHUNK)
            y_ref[rows, :] = jax.nn.gelu(zo[:, t * 128:(t + 1) * 128] + d_ref[...] * u_ref[rows, :])


def _s5_scan(proj, perm, tables, h0_re, h0_im, d_row, layer):
    def lspec(*tail):
        n = len(tail)
        return pl.BlockSpec((None, S5_GB) + tail, lambda t: (layer, t) + (0,) * n)

    def gspec(*tail):
        n = len(tail)
        return pl.BlockSpec((S5_GB,) + tail, lambda t: (t,) + (0,) * n)

    st = 2 * S5_STATE
    return pl.pallas_call(
        _s5_kernel,
        grid=(S5_GROUPS // S5_GB,),
        in_specs=[pl.BlockSpec((N_ROWS, 128), lambda t: (0, COL_U // 128 + t)),
                  pl.BlockSpec((S5_PERM, S5_PERM), lambda t: (0, 0)),
                  lspec(S5_ROW, S5_ROW), lspec(S5_ROW, st), lspec(S5_ROW, st), lspec(S5_ROW, st),
                  lspec(S5_ROW, st), lspec(1, st), lspec(1, st), lspec(DEC_BATCH, st), lspec(DEC_BATCH, st),
                  pl.BlockSpec((1, 128), lambda t: (0, t))],
        out_specs=[pl.BlockSpec((N_ROWS, 128), lambda t: (0, t)), gspec(BATCH, st), gspec(BATCH, st)],
        out_shape=[jax.ShapeDtypeStruct((N_ROWS, S5_WIDTH), F32),
                   jax.ShapeDtypeStruct((S5_GROUPS, BATCH, st), F32),
                   jax.ShapeDtypeStruct((S5_GROUPS, BATCH, st), F32)],
        scratch_shapes=([pltpu.VMEM((S5_GB, S5_ROWS, S5_ROW), BF16)]
                        + [pltpu.VMEM((S5_GB, S5_ROWS, st), F32) for _ in range(6)]
                        + [pltpu.VMEM((S5_GB, S5_ROWS, S5_ROW), F32)]),
        compiler_params=_cparams("parallel"),
        name="s5_scan",
    )(proj, perm, *tables, h0_re, h0_im, d_row.reshape(1, -1))


PRE_ROWS = 128


def _conv_block(x_ref, w_ref, r0, seq, width):
    x = x_ref[pl.ds(r0, PRE_ROWS), :]
    prev = x_ref[pl.ds(jnp.maximum(r0 - 1, 0), 1), :]
    nxt = x_ref[pl.ds(jnp.minimum(r0 + PRE_ROWS, seq - 1), 1), :]
    rid = lax.broadcasted_iota(jnp.int32, (PRE_ROWS, 1), 0)
    pos = (r0 + rid) % width
    xm = jnp.where(rid == 0, prev, pltpu.roll(x, 1, 0))
    xm = jnp.where(pos == 0, 0.0, xm)
    xp = jnp.where(rid == PRE_ROWS - 1, nxt, pltpu.roll(x, PRE_ROWS - 1, 0))
    xp = jnp.where(pos == width - 1, 0.0, xp)
    return xm * w_ref[0:1, :] + x * w_ref[1:2, :] + xp * w_ref[2:3, :]


def _round_robin(problems):
    live = list(problems)
    while live:
        nxt = []
        for p in live:
            try:
                next(p)
                nxt.append(p)
            except StopIteration:
                pass
        live = nxt


def _tri(n, lower):
    r = lax.broadcasted_iota(jnp.int32, (n, n), 0)
    c = lax.broadcasted_iota(jnp.int32, (n, n), 1)
    return (r >= c) if lower else (r <= c)


SSD_PAIR = 2 * SSD_HEADDIM
SSD_PAIRS = SSD_HEADS // 2


def _ssd_kernel(xbc_ref, sm_ref, smt_ref, z_ref, cw_ref, cb_ref, dtb_r_ref, dtb_c_ref, alog_r_ref, alog_c_ref,
                dvec_ref, ng_ref, h0_ref, yn_ref, hf_ref, xs_scr, bc_scr, dac_scr, dar_scr, dtr_scr, h_scr, y_ref,
                *, seq, width):
    nck = seq // SSD_CHUNK

    def pre(bi, carry):
        r0 = pl.multiple_of(bi * PRE_ROWS, PRE_ROWS)
        rows = pl.ds(r0, PRE_ROWS)
        xc = _silu(_conv_block(xbc_ref, cw_ref, r0, seq, width) + cb_ref[...])
        xs = xc[:, :SSD_WIDTH]
        xs_scr[rows, :] = xs
        bc_scr[rows, :] = xc[:, SSD_WIDTH:]
        y_ref[rows, :] = dvec_ref[...] * xs
        dt_c = _softplus(sm_ref[rows, 0:16] + dtb_r_ref[...])
        dac_scr[rows, :] = dt_c * (-jnp.exp(alog_r_ref[...]))
        return carry

    lax.fori_loop(0, seq // PRE_ROWS, pre, 0)
    dt_r = _softplus(smt_ref[0:16, :] + dtb_c_ref[...])
    da_r = dt_r * (-jnp.exp(alog_c_ref[...]))
    for ck in range(nck):
        dtr_scr[ck] = dt_r[:, ck * SSD_CHUNK:(ck + 1) * SSD_CHUNK]
        dar_scr[ck] = da_r[:, ck * SSD_CHUNK:(ck + 1) * SSD_CHUNK]
    h_scr[...] = h0_ref[...]

    tril = _tri(SSD_CHUNK, True)
    triu = _tri(SSD_CHUNK, False)
    tril_b = tril.astype(BF16)
    triu_b = triu.astype(BF16)
    lo_half = lax.broadcasted_iota(jnp.int32, (1, SSD_PAIR), 1) < SSD_HEADDIM

    def chunk_problem(dirs, k):
        for d in dirs:
            c = k if d == 0 else nck - 1 - k
            r0 = pl.multiple_of(c * SSD_CHUNK, SSD_CHUNK)
            rows = pl.ds(r0, SSD_CHUNK)
            mask = tril if d == 0 else triu
            ac = _dot_exact_lhs(tril_b if d == 0 else triu_b, dac_scr[rows, :])
            at = _dot_exact_rhs(dar_scr[c], triu_b if d == 0 else tril_b)
            dt_row = dtr_scr[c]
            end = SSD_CHUNK - 1 if d == 0 else 0
            bcx = bc_scr[rows, :]
            gmat = []
            for g in range(SSD_GROUPS):
                bm = bcx[:, g * SSD_STATE:(g + 1) * SSD_STATE]
                cm = bcx[:, 2 * SSD_STATE + g * SSD_STATE:2 * SSD_STATE + (g + 1) * SSD_STATE]
                gmat.append((bm.T, cm, _bdot_nt(cm, bm)))
            yield
            for pr in range(SSD_PAIRS):
                bmt, cm, gm = gmat[pr // (SSD_PAIRS // SSD_GROUPS)]
                xpair = xs_scr[rows, pr * SSD_PAIR:(pr + 1) * SSD_PAIR]
                sc, bt, es, dec, xh = [], [], [], [], []
                for half in range(2):
                    ln = d * SSD_HEADS + 2 * pr + half
                    colb = jnp.broadcast_to(ac[:, ln:ln + 1], (SSD_CHUNK, SSD_CHUNK))
                    row = at[ln:ln + 1, :]
                    dtr = dt_row[ln:ln + 1, :]
                    seg = jnp.where(mask, jnp.exp(jnp.where(mask, colb - row, 0.0)), 0.0)
                    a_end = row[:, end:end + 1]
                    sc.append(gm * seg * dtr)
                    bt.append(bmt * (jnp.exp(a_end - row) * dtr))
                    es.append(jnp.exp(colb))
                    dec.append(jnp.exp(a_end))
                    xh.append(jnp.where(lo_half if half == 0 else jnp.logical_not(lo_half), xpair, 0.0))
                xst = jnp.concatenate(xh, axis=0)
                hs = h_scr[d, pr]
                y = _bdot(jnp.concatenate(sc, axis=1), xst)
                y = y + _bdot(cm, hs) * jnp.where(lo_half, es[0], es[1])
                y_ref[rows, pr * SSD_PAIR:(pr + 1) * SSD_PAIR] += y
                h_scr[d, pr] = (hs * jnp.where(lo_half, dec[0], dec[1])
                                + _bdot(jnp.concatenate(bt, axis=1), xst))
                yield

    def chunk_step(k, carry):
        _round_robin([chunk_problem((d,), k) for d in range(N_DIR)])
        return carry

    lax.fori_loop(0, nck, chunk_step, 0)
    hf_ref[...] = h_scr[...]

    def post(bi, carry):
        rows = pl.ds(pl.multiple_of(bi * PRE_ROWS, PRE_ROWS), PRE_ROWS)
        y = y_ref[rows, :] * _silu(z_ref[rows, :])
        y = y * lax.rsqrt(jnp.mean(y * y, axis=-1, keepdims=True) + EPS) * ng_ref[...]
        yn_ref[rows, :] = y.astype(BF16)
        return carry

    lax.fori_loop(0, seq // PRE_ROWS, post, 0)


def _ssd(proj, small_t, row0, nseq, seq, width, conv_w, conv_b, dt_bias, a_log, dvec, norm_g, h0):
    blk0 = row0 // seq
    nck = seq // SSD_CHUNK
    kern = functools.partial(_ssd_kernel, seq=seq, width=width)
    full = lambda *shape: pl.BlockSpec(shape, lambda b: (0,) * len(shape))
    dtb_r = dt_bias.reshape(1, 16)
    dtb_c = dt_bias.reshape(16, 1)
    al_r = a_log.reshape(1, 16)
    al_c = a_log.reshape(16, 1)
    st_spec = pl.BlockSpec((None, N_DIR, SSD_PAIRS, SSD_STATE, SSD_PAIR), lambda b: (b, 0, 0, 0, 0))
    return pl.pallas_call(
        kern,
        grid=(nseq,),
        in_specs=[
            pl.BlockSpec((seq, SSD_CONV_DIM), lambda b: (blk0 + b, COL_XBC // SSD_CONV_DIM)),
            pl.BlockSpec((seq, SMALL_W), lambda b: (blk0 + b, COL_SMALL // SMALL_W)),
            pl.BlockSpec((32, seq), lambda b: (0, blk0 + b)),
            pl.BlockSpec((seq, SSD_WIDTH), lambda b: (blk0 + b, COL_Z // SSD_WIDTH)),
            full(3, SSD_CONV_DIM), full(1, SSD_CONV_DIM), full(1, 16), full(16, 1), full(1, 16), full(16, 1),
            full(1, SSD_WIDTH), full(1, SSD_WIDTH), st_spec,
        ],
        out_specs=[pl.BlockSpec((seq, SSD_WIDTH), lambda b: (b, 0)), st_spec],
        out_shape=[jax.ShapeDtypeStruct((nseq * seq, SSD_WIDTH), BF16),
                   jax.ShapeDtypeStruct((nseq, N_DIR, SSD_PAIRS, SSD_STATE, SSD_PAIR), F32)],
        scratch_shapes=[
            pltpu.VMEM((seq, SSD_WIDTH), F32), pltpu.VMEM((seq, 4 * SSD_STATE), F32),
            pltpu.VMEM((seq, 16), F32), pltpu.VMEM((nck, 16, SSD_CHUNK), F32),
            pltpu.VMEM((nck, 16, SSD_CHUNK), F32),
            pltpu.VMEM((N_DIR, SSD_PAIRS, SSD_STATE, SSD_PAIR), F32),
            pltpu.VMEM((seq, SSD_WIDTH), F32),
        ],
        compiler_params=_cparams("parallel"),
        name="ssd_seq%d" % seq,
    )(proj, proj, small_t, proj, conv_w, conv_b.reshape(1, -1), dtb_r, dtb_c, al_r, al_c, dvec,
      norm_g.reshape(1, -1), h0)


def _ssd_state_to_pairs(h):
    lead = h.shape[:-3]
    t = h.reshape(lead + (SSD_PAIRS, 2, SSD_HEADDIM, SSD_STATE))
    t = jnp.moveaxis(t, -1, -3)
    return t.reshape(lead + (SSD_PAIRS, SSD_STATE, SSD_PAIR))


def _ssd_state_from_pairs(hp):
    lead = hp.shape[:-3]
    t = hp.reshape(lead + (SSD_PAIRS, SSD_STATE, 2, SSD_HEADDIM))
    t = jnp.moveaxis(t, -3, -1)
    return t.reshape(lead + (SSD_HEADS, SSD_HEADDIM, SSD_STATE))


DN_ST = DN_HEADS * DN_CHUNK
DN_PAR = 4


def _dn_kernel(qkv_ref, sm_ref, smt_ref, dg_ref, cw_ref, dtb_r_ref, alog_r_ref, dtb_c_ref, alog_c_ref, ng_ref,
               s0_ref, on_ref, sf_ref, q_scr, k_scr, v_scr, b_scr, g_scr, s_scr, u_scr, wq_scr, a_scr, kd_scr,
               gl_scr, grow_scr, o_ref, *, seq, width):
    nck = seq // DN_CHUNK
    g_rows = -jnp.exp(alog_c_ref[...]) * _softplus(smt_ref[24:32, :] + dtb_c_ref[...])
    for ck in range(nck):
        grow_scr[ck] = g_rows[:, ck * DN_CHUNK:(ck + 1) * DN_CHUNK]

    def pre(bi, carry):
        r0 = pl.multiple_of(bi * PRE_ROWS, PRE_ROWS)
        rows = pl.ds(r0, PRE_ROWS)
        xc = _silu(_conv_block(qkv_ref, cw_ref, r0, seq, width))
        for h in range(DN_HEADS):
            q = xc[:, h * DN_DK:(h + 1) * DN_DK]
            k = xc[:, DN_QK + h * DN_DK:DN_QK + (h + 1) * DN_DK]
            q_scr[rows, h * DN_DK:(h + 1) * DN_DK] = (
                q * lax.rsqrt(jnp.sum(q * q, axis=-1, keepdims=True) + EPS) * (DN_DK ** -0.5))
            k_scr[rows, h * DN_DK:(h + 1) * DN_DK] = (
                k * lax.rsqrt(jnp.sum(k * k, axis=-1, keepdims=True) + EPS))
        v_scr[rows, :] = xc[:, 2 * DN_QK:]
        b_scr[rows, :] = _sigmoid(sm_ref[rows, 16:24])
        g_scr[rows, :] = -jnp.exp(alog_r_ref[...]) * _softplus(sm_ref[rows, 24:32] + dtb_r_ref[...])
        o_ref[rows, :] = jnp.zeros((PRE_ROWS, DN_V), F32)
        return carry

    lax.fori_loop(0, seq // PRE_ROWS, pre, 0)
    s_scr[...] = s0_ref[...]

    r = lax.broadcasted_iota(jnp.int32, (DN_ST, DN_ST), 0)
    c = lax.broadcasted_iota(jnp.int32, (DN_ST, DN_ST), 1)
    same = (r // DN_CHUNK) == (c // DN_CHUNK)
    eye = (r == c).astype(F32)
    tril64 = _tri(DN_CHUNK, True).astype(BF16)
    triu64 = _tri(DN_CHUNK, False).astype(BF16)
    tj = lax.broadcasted_iota(jnp.int32, (DN_CHUNK, DN_ST), 0)
    ti = lax.broadcasted_iota(jnp.int32, (DN_CHUNK, DN_ST), 1) % DN_CHUNK
    cum_f = (tj <= ti).astype(BF16)
    cum_b = (tj >= ti).astype(BF16)

    def chunk_problem(d, ci):
        r0 = pl.multiple_of(ci * DN_CHUNK, DN_CHUNK)
        rows = pl.ds(r0, DN_CHUNK)
        incl = jnp.logical_and(same, (r >= c) if d == 0 else (r <= c))
        strict = jnp.logical_and(same, (r > c) if d == 0 else (r < c))
        gc_c = _dot_exact_lhs(tril64 if d == 0 else triu64, g_scr[rows, :])
        gc_t = _dot_exact_rhs(grow_scr[ci], cum_f if d == 0 else cum_b)
        gc_r = jnp.concatenate(
            [jnp.broadcast_to(gc_t[d * DN_HEADS + h:d * DN_HEADS + h + 1, :], (DN_CHUNK, DN_ST))
             for h in range(DN_HEADS)], axis=0)
        beta = b_scr[rows, :]
        end_row = DN_CHUNK - 1 if d == 0 else 0
        k_st, q_st, v_st, bt_st, gc_st, gl_st = [], [], [], [], [], []
        for h in range(DN_HEADS):
            ln = d * DN_HEADS + h
            k_st.append(k_scr[rows, h * DN_DK:(h + 1) * DN_DK])
            q_st.append(q_scr[rows, h * DN_DK:(h + 1) * DN_DK])
            v_st.append(v_scr[rows, h * DN_DV:(h + 1) * DN_DV])
            bt_st.append(beta[:, ln:ln + 1])
            col = gc_c[:, ln:ln + 1]
            gc_st.append(col)
            gl_st.append(col[end_row:end_row + 1, :])
        kst = jnp.concatenate(k_st, axis=0)
        qst = jnp.concatenate(q_st, axis=0)
        vst = jnp.concatenate(v_st, axis=0)
        bst = jnp.concatenate(bt_st, axis=0)
        gst = jnp.concatenate(gc_st, axis=0)
        decay = jnp.where(incl, jnp.exp(jnp.where(incl, gst - gc_r, 0.0)), 0.0)
        kb = kst * bst
        kstb = kst.astype(BF16)
        m = jnp.where(strict, _bdot_nt(kb, kstb) * decay, 0.0)
        attn = jnp.where(incl, _bdot_nt(qst, kstb) * decay, 0.0)
        a_scr[d, ci] = attn.astype(BF16)
        yield
        def wide(bd):
            return sum(bd[h * DN_CHUNK:(h + 1) * DN_CHUNK, :] for h in range(DN_HEADS))

        def block_diag(w):
            return jnp.where(same, jnp.concatenate([w] * DN_HEADS, axis=0), 0.0)

        m_w = wide(m)
        t_w = wide(eye) - m_w
        p_w = jnp.dot(m_w.astype(BF16), m.astype(BF16), preferred_element_type=F32)
        yield
        for lvl in range(5):
            p_bd = block_diag(p_w).astype(BF16)
            if lvl < 4:
                both = jnp.dot(jnp.concatenate([t_w, p_w], axis=0).astype(BF16), p_bd,
                               preferred_element_type=F32)
                t_w = t_w + both[:DN_CHUNK]
                p_w = both[DN_CHUNK:]
            else:
                t_w = t_w + jnp.dot(t_w.astype(BF16), p_bd, preferred_element_type=F32)
            yield
        tb = block_diag(t_w).astype(BF16)
        rhs = jnp.concatenate([vst * bst, kb * jnp.exp(gst)], axis=1)
        x0 = jnp.dot(tb, rhs.astype(BF16), preferred_element_type=F32)
        yield
        res = rhs - x0 - _dot3(m, x0)
        yield
        uw = x0 + jnp.dot(tb, res.astype(BF16), preferred_element_type=F32)
        yield
        qg = qst * jnp.exp(gst)
        u_scr[d, ci] = uw[:, :DN_DV]
        kdec = []
        for h in range(DN_HEADS):
            hs = slice(h * DN_CHUNK, (h + 1) * DN_CHUNK)
            wq_scr[d, ci, h] = jnp.concatenate([uw[hs, DN_DV:], qg[hs, :]], axis=0).astype(BF16)
            gl = gl_st[h]
            kdec.append(k_st[h] * jnp.exp(gl - gc_st[h]))
            gl_scr[d, ci, h:h + 1, :] = jnp.broadcast_to(jnp.exp(gl), (1, DN_DV))
        kd_scr[d, ci] = jnp.concatenate(kdec, axis=0).astype(BF16)

    def chunk_step(kk, carry):
        _round_robin([chunk_problem(d, kk * DN_PAR + j) for j in range(DN_PAR) for d in range(N_DIR)])
        return carry

    lax.fori_loop(0, nck // DN_PAR, chunk_step, 0)

    def state_problem(d, ci):
        rows = pl.ds(pl.multiple_of(ci * DN_CHUNK, DN_CHUNK), DN_CHUNK)
        s_old, vnew, qs_all = [], [], []
        for h in range(DN_HEADS):
            hs = slice(h * DN_CHUNK, (h + 1) * DN_CHUNK)
            s_h = s_scr[d, h]
            ws = jnp.dot(wq_scr[d, ci, h], s_h.astype(BF16), preferred_element_type=F32)
            s_old.append(s_h)
            vnew.append((u_scr[d, ci, hs, :] - ws[:DN_CHUNK]).astype(BF16))
            qs_all.append(ws[DN_CHUNK:])
        yield
        o_st = jnp.concatenate(qs_all, axis=0) + jnp.dot(
            a_scr[d, ci], jnp.concatenate(vnew, axis=0), preferred_element_type=F32)
        for h in range(DN_HEADS):
            hs = slice(h * DN_CHUNK, (h + 1) * DN_CHUNK)
            s_scr[d, h] = s_old[h] * gl_scr[d, ci, h:h + 1, :] + lax.dot_general(
                kd_scr[d, ci, hs, :], vnew[h], (((0,), (0,)), ((), ())), preferred_element_type=F32)
        yield
        for h in range(DN_HEADS):
            hs = slice(h * DN_CHUNK, (h + 1) * DN_CHUNK)
            o_ref[rows, h * DN_DV:(h + 1) * DN_DV] += o_st[hs, :]

    def state_step(kk, carry):
        _round_robin([state_problem(0, kk), state_problem(1, nck - 1 - kk)])
        return carry

    lax.fori_loop(0, nck, state_step, 0, unroll=4)
    sf_ref[...] = s_scr[...]

    def post(bi, carry):
        rows = pl.ds(pl.multiple_of(bi * PRE_ROWS, PRE_ROWS), PRE_ROWS)
        for h in range(DN_HEADS):
            cols = slice(h * DN_DV, (h + 1) * DN_DV)
            oh = o_ref[rows, cols]
            oh = oh * lax.rsqrt(jnp.mean(oh * oh, axis=-1, keepdims=True) + EPS) * ng_ref[...]
            on_ref[rows, cols] = (oh * _silu(dg_ref[rows, cols])).astype(BF16)
        return carry

    lax.fori_loop(0, seq // PRE_ROWS, post, 0)


def _dn(proj, small_t, row0, nseq, seq, width, conv_w, dt_bias, a_log, norm_g, s0):
    blk0 = row0 // seq
    nck = seq // DN_CHUNK
    kern = functools.partial(_dn_kernel, seq=seq, width=width)
    full = lambda *shape: pl.BlockSpec(shape, lambda b: (0,) * len(shape))
    dtb_r = dt_bias.reshape(1, 8)
    al_r = a_log.reshape(1, 8)
    dtb_c = dt_bias.reshape(8, 1)
    al_c = a_log.reshape(8, 1)
    return pl.pallas_call(
        kern,
        grid=(nseq,),
        in_specs=[
            pl.BlockSpec((seq, DN_CONV_DIM), lambda b: (blk0 + b, COL_QKV // DN_CONV_DIM)),
            pl.BlockSpec((seq, SMALL_W), lambda b: (blk0 + b, COL_SMALL // SMALL_W)),
            pl.BlockSpec((32, seq), lambda b: (0, blk0 + b)),
            pl.BlockSpec((seq, DN_V), lambda b: (blk0 + b, COL_DNG // DN_V)),
            full(3, DN_CONV_DIM), full(1, 8), full(1, 8), full(8, 1), full(8, 1), full(1, DN_DV),
            pl.BlockSpec((None, N_DIR, DN_HEADS, DN_DK, DN_DV), lambda b: (b, 0, 0, 0, 0)),
        ],
        out_specs=[
            pl.BlockSpec((seq, DN_V), lambda b: (b, 0)),
            pl.BlockSpec((None, N_DIR, DN_HEADS, DN_DK, DN_DV), lambda b: (b, 0, 0, 0, 0)),
        ],
        out_shape=[jax.ShapeDtypeStruct((nseq * seq, DN_V), BF16),
                   jax.ShapeDtypeStruct((nseq, N_DIR, DN_HEADS, DN_DK, DN_DV), F32)],
        scratch_shapes=[
            pltpu.VMEM((seq, DN_QK), F32), pltpu.VMEM((seq, DN_QK), F32), pltpu.VMEM((seq, DN_V), F32),
            pltpu.VMEM((seq, 8), F32), pltpu.VMEM((seq, 8), F32),
            pltpu.VMEM((N_DIR, DN_HEADS, DN_DK, DN_DV), F32),
            pltpu.VMEM((N_DIR, nck, DN_ST, DN_DV), F32),
            pltpu.VMEM((N_DIR, nck, DN_HEADS, 2 * DN_CHUNK, DN_DK), BF16),
            pltpu.VMEM((N_DIR, nck, DN_ST, DN_ST), BF16),
            pltpu.VMEM((N_DIR, nck, DN_ST, DN_DK), BF16),
            pltpu.VMEM((N_DIR, nck, 8, DN_DV), F32),
            pltpu.VMEM((nck, 8, DN_CHUNK), F32),
            pltpu.VMEM((seq, DN_V), F32),
        ],
        compiler_params=_cparams("parallel"),
        name="dn_seq%d" % seq,
    )(proj, proj, small_t, proj, conv_w, dtb_r, al_r, dtb_c, al_c, norm_g.reshape(1, -1), s0)


MRG_TM = 512
MRG_TD = 512


def _merge_kernel(x_ref, sh_ref, sc_ref, gt_ref, ng_ref, y5_ref, ysc_ref, ysl_ref,
                  odc_ref, odl_ref, wg0_ref, wg1_ref, wg2_ref, glu_ref, swo_ref,
                  dwo_ref, wo_ref, o_ref, h_scr, a_scr, b_scr, c_scr, acc_scr):
    i = pl.program_id(0)
    j = pl.program_id(1)
    grp = _row_group(i, MRG_TM)

    @pl.when(j == 0)
    def _():
        is_ctx = i < N_CTX // MRG_TM
        h = _norm_mod(x_ref[...], ng_ref[...], sc_ref[pl.ds(grp, 1), :], sh_ref[pl.ds(grp, 1), :])
        h_scr[...] = h.astype(BF16)
        a_scr[...] = y5_ref[...].astype(BF16)
        b_scr[...] = jnp.where(is_ctx, ysc_ref[...], ysl_ref[...])
        c_scr[...] = jnp.where(is_ctx, odc_ref[...], odl_ref[...])
        acc_scr[...] = jnp.zeros_like(acc_scr)

    h = h_scr[...]
    nt = (((1,), (1,)), ((), ()))
    gate = lambda w_ref: _sigmoid(lax.dot_general(h, w_ref[...], nt, preferred_element_type=F32))
    g5 = a_scr[...]
    br_a = (jnp.dot(g5, glu_ref[0], preferred_element_type=F32)
            * _sigmoid(jnp.dot(g5, glu_ref[1], preferred_element_type=F32)))
    br_b = jnp.dot(b_scr[...], swo_ref[...], preferred_element_type=F32)
    br_c = jnp.dot(c_scr[...], dwo_ref[...], preferred_element_type=F32)
    merged = gate(wg0_ref) * br_a + gate(wg1_ref) * br_b + gate(wg2_ref) * br_c
    acc_scr[...] += jnp.dot(merged.astype(BF16), wo_ref[...], preferred_element_type=F32)

    @pl.when(j == pl.num_programs(1) - 1)
    def _():
        o_ref[...] = x_ref[...] + gt_ref[pl.ds(grp, 1), :] * acc_scr[...]


def _merge(x, mod, norm_g_row, w_packed, y5, ys_c, ys_l, od_c, od_l, glu_b, swo_b, dwo_b, wo_b, layer):
    nctx = N_CTX // MRG_TM
    rowblk = lambda w, col: pl.BlockSpec((MRG_TM, w), lambda i, j: (i, col // w))
    ctxblk = lambda w: pl.BlockSpec((MRG_TM, w), lambda i, j: (jnp.minimum(i, nctx - 1), 0))
    latblk = lambda w: pl.BlockSpec((MRG_TM, w), lambda i, j: (jnp.maximum(i - nctx, 0), 0))
    row1 = lambda w: pl.BlockSpec((1, w), lambda i, j: (0, 0))
    modblk = lambda k: pl.BlockSpec((8, D_MODEL), lambda i, j: (0, k))
    gateblk = lambda k: pl.BlockSpec((None, MRG_TD, D_MODEL),
                                     lambda i, j: (layer, (COL_GATES + k * D_MODEL) // MRG_TD + j, 0))
    return pl.pallas_call(
        _merge_kernel,
        grid=(N_ROWS // MRG_TM, D_MODEL // MRG_TD),
        in_specs=[
            rowblk(D_MODEL, 0), modblk(3), modblk(4), modblk(5), row1(D_MODEL),
            rowblk(S5_WIDTH, 0),
            ctxblk(SSD_WIDTH), latblk(SSD_WIDTH), ctxblk(DN_V), latblk(DN_V),
            gateblk(0), gateblk(1), gateblk(2),
            pl.BlockSpec((None, 2, S5_WIDTH, MRG_TD), lambda i, j: (layer, 0, 0, j)),
            pl.BlockSpec((None, SSD_WIDTH, MRG_TD), lambda i, j: (layer, 0, j)),
            pl.BlockSpec((None, DN_V, MRG_TD), lambda i, j: (layer, 0, j)),
            pl.BlockSpec((None, MRG_TD, D_MODEL), lambda i, j: (layer, j, 0)),
        ],
        out_specs=rowblk(D_MODEL, 0),
        out_shape=jax.ShapeDtypeStruct((N_ROWS, D_MODEL), F32),
        scratch_shapes=[pltpu.VMEM((MRG_TM, D_MODEL), BF16), pltpu.VMEM((MRG_TM, S5_WIDTH), BF16),
                        pltpu.VMEM((MRG_TM, SSD_WIDTH), BF16), pltpu.VMEM((MRG_TM, DN_V), BF16),
                        pltpu.VMEM((MRG_TM, D_MODEL), F32)],
        compiler_params=_cparams("parallel", "arbitrary"),
        name="merge",
    )(x, mod, mod, mod, norm_g_row, y5, ys_c, ys_l, od_c, od_l,
      w_packed, w_packed, w_packed, glu_b, swo_b, dwo_b, wo_b)


FIN_TM = 1024


def _final_norm_kernel(x_ref, g_ref, oc_ref, ol_ref):
    i = pl.program_id(0)
    x = x_ref[...]
    y = x * lax.rsqrt(jnp.mean(x * x, axis=-1, keepdims=True) + EPS) * g_ref[...]

    @pl.when(i < N_CTX // FIN_TM)
    def _():
        oc_ref[...] = y

    @pl.when(i >= N_CTX // FIN_TM)
    def _():
        ol_ref[...] = y


def _final_norm(x, g):
    nctx = N_CTX // FIN_TM
    return pl.pallas_call(
        _final_norm_kernel,
        grid=(N_ROWS // FIN_TM,),
        in_specs=[pl.BlockSpec((FIN_TM, D_MODEL), lambda i: (i, 0)), pl.BlockSpec((1, D_MODEL), lambda i: (0, 0))],
        out_specs=[pl.BlockSpec((FIN_TM, D_MODEL), lambda i: (jnp.minimum(i, nctx - 1), 0)),
                   pl.BlockSpec((FIN_TM, D_MODEL), lambda i: (jnp.maximum(i - nctx, 0), 0))],
        out_shape=[jax.ShapeDtypeStruct((N_CTX, D_MODEL), F32), jax.ShapeDtypeStruct((N_LAT, D_MODEL), F32)],
        compiler_params=_cparams("arbitrary"),
        name="final_norm",
    )(x, g.reshape(1, -1))


def kernel(x_prompt, x_sample, state_s5_re, state_s5_im, state_ssd, state_dn, c, c_ctx, ada_w, ada_b, norm_g, ffn_wi, ffn_wo, w_in, s5_lam_re, s5_lam_im, s5_log_dt, s5_b_re, s5_b_im, s5_c_re, s5_c_im, s5_d, s5_glu, ssd_conv_w, ssd_conv_b, ssd_dt_bias, ssd_a_log, ssd_d, ssd_norm_g, ssd_w_out, dn_conv_w, dn_dt_bias, dn_a_log, dn_norm_g, dn_w_out, w_out, final_norm_g):
    x = jnp.concatenate([x_prompt.reshape(N_CTX, D_MODEL), x_sample.reshape(N_LAT, D_MODEL)], axis=0)
    cond8 = jnp.concatenate([c_ctx[None, :], c, jnp.zeros((8 - 1 - DEC_BATCH, D_MODEL), F32)], axis=0)
    mods = _ada_mods(cond8, ada_w, ada_b)

    s5_tables = _s5_prep(s5_lam_re, s5_lam_im, s5_log_dt, s5_b_re, s5_b_im, s5_c_re, s5_c_im)
    s5_perm = _s5_perm()
    w_packed = _repack_w_in(w_in)
    glu_b = s5_glu.astype(BF16)
    swo_b = ssd_w_out.astype(BF16)
    dwo_b = dn_w_out.astype(BF16)
    wo_b = w_out.astype(BF16)
    ssd_d_rows = jnp.repeat(ssd_d, SSD_HEADDIM, axis=1).reshape(DEPTH, 1, SSD_WIDTH)

    def s5_h0(state):
        return state.transpose(1, 3, 0, 2, 4).reshape(DEPTH, S5_GROUPS, DEC_BATCH, 2 * S5_STATE)

    h0_re = s5_h0(state_s5_re)
    h0_im = s5_h0(state_s5_im)
    ssd_h0_lat = _ssd_state_to_pairs(jnp.swapaxes(state_ssd, 0, 1))
    ssd_h0_ctx = jnp.zeros((BATCH, N_DIR, SSD_PAIRS, SSD_STATE, SSD_PAIR), F32)
    dn_s0_lat = jnp.swapaxes(state_dn, 0, 1)
    dn_s0_ctx = jnp.zeros((BATCH, N_DIR, DN_HEADS, DN_DK, DN_DV), F32)

    new_s5_re, new_s5_im, new_ssd, new_dn = [], [], [], []
    for l in range(DEPTH):
        mod = mods[l]
        x = _ffn(x, mod, norm_g[l, 0:1], ffn_wi, ffn_wo, l, 0)
        proj, small_t = _inproj(x, mod, norm_g[l, 1:2], w_packed, l)

        y5, f_re, f_im = _s5_scan(proj, s5_perm, s5_tables, h0_re, h0_im, s5_d[l], l)
        new_s5_re.append(f_re.reshape(S5_GROUPS, BATCH, N_DIR, S5_STATE).transpose(1, 2, 0, 3))
        new_s5_im.append(f_im.reshape(S5_GROUPS, BATCH, N_DIR, S5_STATE).transpose(1, 2, 0, 3))

        ys_c, hs_c = _ssd(proj, small_t, 0, BATCH, SEQ, SEQ, ssd_conv_w[l], ssd_conv_b[l], ssd_dt_bias[l],
                          ssd_a_log[l], ssd_d_rows[l], ssd_norm_g[l], ssd_h0_ctx)
        ys_l, _ = _ssd(proj, small_t, N_CTX, DEC_BATCH, DEC_SEQ, GRID_W, ssd_conv_w[l], ssd_conv_b[l],
                       ssd_dt_bias[l], ssd_a_log[l], ssd_d_rows[l], ssd_norm_g[l], ssd_h0_lat[l])
        new_ssd.append(_ssd_state_from_pairs(hs_c))
        od_c, sd_c = _dn(proj, small_t, 0, BATCH, SEQ, SEQ, dn_conv_w[l], dn_dt_bias[l], dn_a_log[l],
                         dn_norm_g[l], dn_s0_ctx)
        od_l, _ = _dn(proj, small_t, N_CTX, DEC_BATCH, DEC_SEQ, GRID_W, dn_conv_w[l], dn_dt_bias[l],
                      dn_a_log[l], dn_norm_g[l], dn_s0_lat[l])
        new_dn.append(sd_c)

        x = _merge(x, mod, norm_g[l, 1:2], w_packed, y5, ys_c, ys_l, od_c, od_l, glu_b, swo_b,
                   dwo_b, wo_b, l)
        x = _ffn(x, mod, norm_g[l, 2:3], ffn_wi, ffn_wo, l, 1)

    y_ctx, y_lat = _final_norm(x, final_norm_g)
    y_prompt = y_ctx.reshape(BATCH, SEQ, D_MODEL)
    y_sample = y_lat.reshape(DEC_BATCH, DEC_SEQ, D_MODEL)
    return (y_prompt, y_sample, jnp.stack(new_s5_re, axis=1), jnp.stack(new_s5_im, axis=1),
            jnp.stack(new_ssd, axis=1), jnp.stack(new_dn, axis=1))
```

```python
import functools

import jax
import jax.numpy as jnp
import numpy as np
from jax import lax
from jax.experimental import pallas as pl
from jax.experimental.pallas import tpu as pltpu

F32 = jnp.float32
BF16 = jnp.bfloat16

D_MODEL = 1024
BATCH = 16
SEQ = 256
DEPTH = 4
DEC_BATCH = 2
DEC_SEQ = 1024
GRID_W = 64
N_DIR = 2
N_ADA = 9
D_FF = 2816
EPS = 1e-6

S5_WIDTH = 512
S5_GROUP = 16
S5_GROUPS = 32
S5_STATE = 64
S5_CHUNK = 16

SSD_WIDTH = 512
SSD_HEADDIM = 64
SSD_HEADS = 8
SSD_GROUPS = 2
SSD_STATE = 64
SSD_CHUNK = 128
SSD_CONV_DIM = 768

DN_HEADS = 4
DN_DK = 128
DN_DV = 128
DN_QK = 512
DN_V = 512
DN_CHUNK = 64
DN_CONV_DIM = 1536

IN_SEGMENTS = (512, 512, 768, 16, 1536, 8, 8, 512, 3072)
IN_SPLITS = tuple(int(s) for s in np.cumsum(IN_SEGMENTS)[:-1])

N_CTX = BATCH * SEQ
N_LAT = DEC_BATCH * DEC_SEQ
N_ROWS = N_CTX + N_LAT

COL_QKV = 0
COL_U = 1536
COL_Z = 2048
COL_DNG = 2560
COL_XBC = 3072
COL_SMALL = 3840
PROJ_W = 4096
COL_GATES = PROJ_W
PACK_W = PROJ_W + 3 * D_MODEL
SMALL_W = 128

VMEM_LIMIT = 56 * 1024 * 1024


def _cparams(*sem):
    return pltpu.CompilerParams(dimension_semantics=sem, vmem_limit_bytes=VMEM_LIMIT)


def _sigmoid(x):
    return 0.5 * (jnp.tanh(0.5 * x) + 1.0)


def _silu(x):
    return x * _sigmoid(x)


def _softplus(x):
    return jnp.maximum(x, 0.0) + jnp.log(1.0 + jnp.exp(-jnp.abs(x)))


def _bdot(a, b):
    return jnp.dot(a.astype(BF16), b.astype(BF16), preferred_element_type=F32)


def _bdot_nt(a, b):
    return lax.dot_general(a.astype(BF16), b.astype(BF16), (((1,), (1,)), ((), ())),
                           preferred_element_type=F32)


def _bdot_tn(a, b):
    return lax.dot_general(a.astype(BF16), b.astype(BF16), (((0,), (0,)), ((), ())),
                           preferred_element_type=F32)


def _split3(a):
    hi = a.astype(BF16)
    r = a - hi.astype(F32)
    mid = r.astype(BF16)
    lo = (r - mid.astype(F32)).astype(BF16)
    return hi, mid, lo


def _dot3(a, b):
    ah = a.astype(BF16)
    al = (a - ah.astype(F32)).astype(BF16)
    bh = b.astype(BF16)
    bl = (b - bh.astype(F32)).astype(BF16)
    out = jnp.dot(ah, bh, preferred_element_type=F32)
    out = out + jnp.dot(ah, bl, preferred_element_type=F32)
    out = out + jnp.dot(al, bh, preferred_element_type=F32)
    return out


def _dot_exact_lhs(t_bf16, x):
    hi, mid, lo = _split3(x)
    out = jnp.dot(t_bf16, hi, preferred_element_type=F32)
    out = out + jnp.dot(t_bf16, mid, preferred_element_type=F32)
    out = out + jnp.dot(t_bf16, lo, preferred_element_type=F32)
    return out


def _dot_exact_rhs(x, t_bf16):
    hi, mid, lo = _split3(x)
    out = jnp.dot(hi, t_bf16, preferred_element_type=F32)
    out = out + jnp.dot(mid, t_bf16, preferred_element_type=F32)
    out = out + jnp.dot(lo, t_bf16, preferred_element_type=F32)
    return out


def _norm_mod(x, g, sc, sh):
    ms = jnp.mean(x * x, axis=-1, keepdims=True)
    y = x * lax.rsqrt(ms + EPS) * g
    return y * (1.0 + sc) + sh


def _row_group(i, tm):
    nctx = N_CTX // tm
    per = DEC_SEQ // tm
    return jnp.where(i < nctx, 0, 1 + jnp.maximum(i - nctx, 0) // per)


ADA_TN = 2304


def _ada_kernel(c_ref, w_ref, b_ref, o_ref):
    c = c_ref[...]
    o_ref[...] = _bdot(_silu(c), w_ref[...]) + b_ref[...]


def _ada_mods(cond8, ada_w, ada_b):
    nj = (N_ADA * D_MODEL) // ADA_TN
    return pl.pallas_call(
        _ada_kernel,
        grid=(DEPTH, nj),
        in_specs=[
            pl.BlockSpec((8, D_MODEL), lambda l, j: (0, 0)),
            pl.BlockSpec((None, D_MODEL, ADA_TN), lambda l, j: (l, 0, j)),
            pl.BlockSpec((None, 1, ADA_TN), lambda l, j: (l, 0, j)),
        ],
        out_specs=pl.BlockSpec((None, 8, ADA_TN), lambda l, j: (l, 0, j)),
        out_shape=jax.ShapeDtypeStruct((DEPTH, 8, N_ADA * D_MODEL), F32),
        compiler_params=_cparams("parallel", "parallel"),
        name="ada_mods",
    )(cond8, ada_w, ada_b.reshape(DEPTH, 1, N_ADA * D_MODEL))


FFN_TM = 2048
FFN_TF = 256
MOD_ROWS = 1024
FFN_SUB = FFN_TM // MOD_ROWS


def _ffn_kernel(x_ref, sh_ref, sc_ref, gt_ref, g_ref, wa_ref, wb_ref, wo_ref, o_ref, h_scr, acc_scr):
    i = pl.program_id(0)
    j = pl.program_id(1)

    @pl.when(j == 0)
    def _():
        for s in range(FFN_SUB):
            rows = slice(s * MOD_ROWS, (s + 1) * MOD_ROWS)
            grp = _row_group(i * FFN_SUB + s, MOD_ROWS)
            h = _norm_mod(x_ref[rows, :], g_ref[...], sc_ref[pl.ds(grp, 1), :], sh_ref[pl.ds(grp, 1), :])
            h_scr[rows, :] = h.astype(BF16)
        acc_scr[...] = jnp.zeros_like(acc_scr)

    h = h_scr[...]
    a = jnp.dot(h, wa_ref[...].astype(BF16), preferred_element_type=F32)
    b = jnp.dot(h, wb_ref[...].astype(BF16), preferred_element_type=F32)
    u = (_silu(a) * b).astype(BF16)
    acc_scr[...] += jnp.dot(u, wo_ref[...].astype(BF16), preferred_element_type=F32)

    @pl.when(j == pl.num_programs(1) - 1)
    def _():
        for s in range(FFN_SUB):
            rows = slice(s * MOD_ROWS, (s + 1) * MOD_ROWS)
            grp = _row_group(i * FFN_SUB + s, MOD_ROWS)
            o_ref[rows, :] = x_ref[rows, :] + (0.5 * gt_ref[pl.ds(grp, 1), :]) * acc_scr[rows, :]


def _ffn(x, mod, norm_g_row, ffn_wi, ffn_wo, layer, which):
    nf = D_FF // FFN_TF
    base = 0 if which == 0 else 6
    return pl.pallas_call(
        _ffn_kernel,
        grid=(N_ROWS // FFN_TM, nf),
        in_specs=[
            pl.BlockSpec((FFN_TM, D_MODEL), lambda i, j: (i, 0)),
            pl.BlockSpec((8, D_MODEL), lambda i, j: (0, base)),
            pl.BlockSpec((8, D_MODEL), lambda i, j: (0, base + 1)),
            pl.BlockSpec((8, D_MODEL), lambda i, j: (0, base + 2)),
            pl.BlockSpec((1, D_MODEL), lambda i, j: (0, 0)),
            pl.BlockSpec((None, None, D_MODEL, FFN_TF), lambda i, j: (layer, which, 0, j)),
            pl.BlockSpec((None, None, D_MODEL, FFN_TF), lambda i, j: (layer, which, 0, j + nf)),
            pl.BlockSpec((None, None, FFN_TF, D_MODEL), lambda i, j: (layer, which, j, 0)),
        ],
        out_specs=pl.BlockSpec((FFN_TM, D_MODEL), lambda i, j: (i, 0)),
        out_shape=jax.ShapeDtypeStruct((N_ROWS, D_MODEL), F32),
        scratch_shapes=[pltpu.VMEM((FFN_TM, D_MODEL), BF16), pltpu.VMEM((FFN_TM, D_MODEL), F32)],
        compiler_params=_cparams("parallel", "arbitrary"),
        name="ffn",
    )(x, mod, mod, mod, norm_g_row, ffn_wi, ffn_wi, ffn_wo)


INP_TM = 512


def _inproj_kernel(x_ref, sh_ref, sc_ref, g_ref, w_ref, o_ref, ot_ref):
    grp = _row_group(pl.program_id(0), INP_TM)
    h = _norm_mod(x_ref[...], g_ref[...], sc_ref[pl.ds(grp, 1), :], sh_ref[pl.ds(grp, 1), :])
    res = lax.dot_general(h.astype(BF16), w_ref[...], (((1,), (1,)), ((), ())), preferred_element_type=F32)
    o_ref[...] = res
    ot_ref[...] = res[:, COL_SMALL:COL_SMALL + SMALL_W].T


RPK_LANES = 256
RPK_MOVES = ((COL_QKV, 1808, 1536), (COL_U, 0, 512), (COL_Z, 512, 512), (COL_DNG, 3360, 512),
             (COL_XBC, 1024, 768), (COL_SMALL, 1792, 16), (COL_SMALL + 16, 3344, 16), (COL_GATES, 3872, 3072))
RPK_PAD = (COL_SMALL + 32, PROJ_W)


def _repack_kernel(w_ref, o_ref):
    for dst, src, n in RPK_MOVES:
        o_ref[dst:dst + n, :] = w_ref[src:src + n, :].astype(BF16)
    o_ref[RPK_PAD[0]:RPK_PAD[1], :] = jnp.zeros((RPK_PAD[1] - RPK_PAD[0], RPK_LANES), BF16)


def _repack_w_in(w_in):
    w_t = jnp.swapaxes(w_in, 1, 2)
    in_w = w_t.shape[1]
    return pl.pallas_call(
        _repack_kernel,
        grid=(DEPTH, D_MODEL // RPK_LANES),
        in_specs=[pl.BlockSpec((None, in_w, RPK_LANES), lambda l, i: (l, 0, i))],
        out_specs=pl.BlockSpec((None, PACK_W, RPK_LANES), lambda l, i: (l, 0, i)),
        out_shape=jax.ShapeDtypeStruct((DEPTH, PACK_W, D_MODEL), BF16),
        compiler_params=_cparams("parallel", "parallel"),
        name="repack_w_in",
    )(w_t)


def _inproj(x, mod, norm_g_row, w_packed, layer):
    return pl.pallas_call(
        _inproj_kernel,
        grid=(N_ROWS // INP_TM,),
        in_specs=[
            pl.BlockSpec((INP_TM, D_MODEL), lambda i: (i, 0)),
            pl.BlockSpec((8, D_MODEL), lambda i: (0, 3)),
            pl.BlockSpec((8, D_MODEL), lambda i: (0, 4)),
            pl.BlockSpec((1, D_MODEL), lambda i: (0, 0)),
            pl.BlockSpec((None, PROJ_W, D_MODEL), lambda i: (layer, 0, 0)),
        ],
        out_specs=[pl.BlockSpec((INP_TM, PROJ_W), lambda i: (i, 0)),
                   pl.BlockSpec((SMALL_W, INP_TM), lambda i: (0, i))],
        out_shape=[jax.ShapeDtypeStruct((N_ROWS, PROJ_W), F32),
                   jax.ShapeDtypeStruct((SMALL_W, N_ROWS), F32)],
        compiler_params=_cparams("parallel"),
        name="inproj",
    )(x, mod, mod, norm_g_row, w_packed)


S5_ROW = S5_CHUNK * S5_GROUP


S5_PREP_G = 2


def _s5_prep_kernel(*refs):
    staged = [_s5_prep_group(*(r.at[g] for r in refs)) for g in range(S5_PREP_G)]
    _round_robin([p for problems, _ in staged for p in problems])
    for _, finish in staged:
        finish()


def _s5_prep_group(lam_re_ref, lam_im_ref, ldt_ref, btr_ref, bti_ref, cr_ref, ci_ref, ctr_ref, cti_ref,
                   m_ref, pre_ref, pim_ref, qre_ref, qim_ref, are_ref, aim_ref):
    tau = lax.broadcasted_iota(jnp.int32, (S5_CHUNK, 1), 0).astype(F32)
    lane = lax.broadcasted_iota(jnp.int32, (S5_GROUP, S5_ROW), 1)
    lane_t = lane[0:1, :] // S5_GROUP
    rep = (lax.broadcasted_iota(jnp.int32, (S5_GROUP, S5_ROW), 0) == lane % S5_GROUP).astype(BF16)
    taps = [None, None]

    def outer(ar, ai, xr, xi):
        rr = ar[:, None, :] * xr[None, :, :] - ai[:, None, :] * xi[None, :, :]
        ii = ar[:, None, :] * xi[None, :, :] + ai[:, None, :] * xr[None, :, :]
        return rr.reshape(S5_ROW, S5_STATE), ii.reshape(S5_ROW, S5_STATE)

    def direction(d):
        lr = lam_re_ref[d:d + 1, :]
        li = lam_im_ref[d:d + 1, :]
        dt = jnp.exp(ldt_ref[d:d + 1, :])
        mag = jnp.exp(lr * dt)
        lb_re = mag * jnp.cos(li * dt)
        lb_im = mag * jnp.sin(li * dt)
        den = lr * lr + li * li
        cr = ((lb_re - 1.0) * lr + lb_im * li) / den
        ci = (lb_im * lr - (lb_re - 1.0) * li) / den
        bt_r = btr_ref[d]
        bt_i = bti_ref[d]
        bbt_r = cr * bt_r - ci * bt_i
        bbt_i = cr * bt_i + ci * bt_r
        c_r = cr_ref[d]
        c_i = ci_ref[d]

        def powtab(t):
            m = jnp.exp(t * (lr * dt))
            ang = t * (li * dt)
            return m * jnp.cos(ang), m * jnp.sin(ang)

        t_in = (S5_CHUNK - 1) - tau if d == 0 else tau
        ar, ai = powtab(t_in)
        ba_r, ba_i = outer(ar, ai, bbt_r, bbt_i)
        lanes = slice(d * S5_STATE, (d + 1) * S5_STATE)
        pre_ref[:, lanes] = ba_r.astype(BF16)
        pim_ref[:, lanes] = ba_i.astype(BF16)
        yield
        kt = _dot3(ba_r, ctr_ref[d]) - _dot3(ba_i, cti_ref[d])
        yield
        taps[d] = _dot_exact_rhs(kt, rep)
        yield
        t_out = tau + 1.0 if d == 0 else S5_CHUNK - tau
        ar, ai = powtab(t_out)
        qr, qi = outer(ar, ai, c_r, c_i)
        qre_ref[:, lanes] = qr.astype(BF16)
        qim_ref[:, lanes] = (-qi).astype(BF16)
        a16r, a16i = powtab(jnp.full((1, 1), float(S5_CHUNK), F32))
        are_ref[:, lanes] = a16r
        aim_ref[:, lanes] = a16i

    def finish():
        last = S5_ROW - S5_GROUP
        table = jnp.concatenate(
            [taps[0][:last, :], taps[0][last:, :] + taps[1][:S5_GROUP, :], taps[1][S5_GROUP:, :]], axis=0)
        mmat = table[last:last + S5_ROW, :]
        for t in range(1, S5_CHUNK):
            start = (S5_CHUNK - 1 - t) * S5_GROUP
            mmat = jnp.where(lane_t == t, table[start:start + S5_ROW, :], mmat)
        m_ref[...] = mmat.astype(BF16)

    return [direction(d) for d in range(N_DIR)], finish


def _s5_prep(lam_re, lam_im, log_dt, b_re, b_im, c_re, c_im):
    tg = lambda t: jnp.swapaxes(t, 1, 2)
    lam_re_g = tg(lam_re)
    lam_im_g = tg(lam_im)
    ldt_g = tg(log_dt)[..., None]
    bt_r = jnp.swapaxes(tg(b_re), -1, -2)
    bt_i = jnp.swapaxes(tg(b_im), -1, -2)
    c_r = tg(c_re)
    c_i = tg(c_im)
    ct_r = jnp.swapaxes(c_r, -1, -2)
    ct_i = jnp.swapaxes(c_i, -1, -2)

    def spec(*tail):
        n = len(tail)
        return pl.BlockSpec((None, S5_PREP_G) + tail, lambda l, g: (l, g) + (0,) * n)

    st = 2 * S5_STATE
    tab = jax.ShapeDtypeStruct((DEPTH, S5_GROUPS, S5_ROW, st), BF16)
    dec = jax.ShapeDtypeStruct((DEPTH, S5_GROUPS, 1, st), F32)
    return pl.pallas_call(
        _s5_prep_kernel,
        grid=(DEPTH, S5_GROUPS // S5_PREP_G),
        in_specs=[spec(2, 64), spec(2, 64), spec(2, 1), spec(2, 16, 64), spec(2, 16, 64),
                  spec(2, 16, 64), spec(2, 16, 64), spec(2, 64, S5_GROUP), spec(2, 64, S5_GROUP)],
        out_specs=[spec(S5_ROW, S5_ROW), spec(S5_ROW, st), spec(S5_ROW, st), spec(S5_ROW, st), spec(S5_ROW, st),
                   spec(1, st), spec(1, st)],
        out_shape=[jax.ShapeDtypeStruct((DEPTH, S5_GROUPS, S5_ROW, S5_ROW), BF16), tab, tab, tab, tab, dec, dec],
        compiler_params=_cparams("parallel", "parallel"),
        name="s5_prep",
    )(lam_re_g, lam_im_g, ldt_g, bt_r, bt_i, c_r, c_i, ct_r, ct_i)


S5_CTX_CH = SEQ // S5_CHUNK
S5_LAT_CH = DEC_SEQ // S5_CHUNK
S5_CTX_ROWS = S5_CTX_CH * BATCH
S5_LAT_ROWS = S5_LAT_CH * DEC_BATCH
S5_ROWS = S5_CTX_ROWS + S5_LAT_ROWS
S5_GB = 128 // S5_GROUP
S5_PERM = S5_GB * 128


def _s5_perm():
    src = np.arange(S5_PERM)
    s, g, j = src // 128, (src % 128) // S5_GROUP, src % S5_GROUP
    p = np.zeros((S5_PERM, S5_PERM), np.float32)
    p[src, g * 128 + s * S5_GROUP + j] = 1.0
    return jnp.asarray(p, BF16)


def _s5_kernel(u_ref, perm_ref, m_ref, pre_ref, pim_ref, qre_ref, qim_ref, are_ref, aim_ref, h0r_ref, h0i_ref,
               d_ref, y_ref, fr_ref, fi_ref, ug, sre, sim, hfr, hfi, hbr, hbi, ys):
    perm = perm_ref[...]
    half_w = S5_ROW // 2
    for half in range(2):
        x = jnp.concatenate([u_ref[pl.ds(half * 8 + s, S5_ROWS, stride=S5_CHUNK), :] for s in range(8)], axis=1)
        z = jnp.dot(x.astype(BF16), perm, preferred_element_type=F32)
        for g in range(S5_GB):
            ug[g, :, half * half_w:(half + 1) * half_w] = z[:, g * 128:(g + 1) * 128].astype(BF16)
    for g in range(S5_GB):
        sre[g] = jnp.dot(ug[g], pre_ref[g].astype(BF16), preferred_element_type=F32)
        sim[g] = jnp.dot(ug[g], pim_ref[g].astype(BF16), preferred_element_type=F32)
    fwd = lax.broadcasted_iota(jnp.int32, (1, 2 * S5_STATE), 1) < S5_STATE

    def scan(base, nchunk, nseq, init):
        def step(k, hs):
            rf = pl.ds(base + k, nseq, stride=nchunk)
            rb = pl.ds(base + nchunk - 1 - k, nseq, stride=nchunk)
            out = []
            for g in range(S5_GB):
                h_re, h_im = hs[2 * g], hs[2 * g + 1]
                hfr.at[g][rf, :] = h_re
                hfi.at[g][rf, :] = h_im
                hbr.at[g][rb, :] = h_re
                hbi.at[g][rb, :] = h_im
                s_r = jnp.where(fwd, sre.at[g][rf, :], sre.at[g][rb, :])
                s_i = jnp.where(fwd, sim.at[g][rf, :], sim.at[g][rb, :])
                ar = are_ref[g]
                ai = aim_ref[g]
                out.append(ar * h_re - ai * h_im + s_r)
                out.append(ar * h_im + ai * h_re + s_i)
            return tuple(out)

        return lax.fori_loop(0, nchunk, step, init, unroll=4)

    zero = jnp.zeros((BATCH, 2 * S5_STATE), F32)
    fin = scan(0, S5_CTX_CH, BATCH, (zero,) * (2 * S5_GB))
    lat0 = []
    for g in range(S5_GB):
        fr_ref[g] = fin[2 * g]
        fi_ref[g] = fin[2 * g + 1]
        lat0 += [h0r_ref[g], h0i_ref[g]]
    scan(S5_CTX_ROWS, S5_LAT_CH, DEC_BATCH, tuple(lat0))

    for g in range(S5_GB):
        h_re = jnp.where(fwd, hfr[g], hbr[g])
        h_im = jnp.where(fwd, hfi[g], hbi[g])
        y = jnp.dot(ug[g], m_ref[g].astype(BF16), preferred_element_type=F32)
        ys[g] = y + _bdot_nt(h_re, qre_ref[g]) + _bdot_nt(h_im, qim_ref[g])
    for half in range(2):
        w = jnp.concatenate([ys[g, :, half * half_w:(half + 1) * half_w] for g in range(S5_GB)], axis=1)
        w_hi = w.astype(BF16)
        w_lo = (w - w_hi.astype(F32)).astype(BF16)
        zo = sum(lax.dot_general(piece, perm, (((1,), (1,)), ((), ())), preferred_element_type=F32)
                 for piece in (w_hi, w_lo))
        for t in range(8):
            rows = pl.ds(half * 8 + t, S5_ROWS, stride=S5_CHUNK)
            y_ref[rows, :] = jax.nn.gelu(zo[:, t * 128:(t + 1) * 128] + d_ref[...] * u_ref[rows, :])


def _s5_scan(proj, perm, tables, h0_re, h0_im, d_row, layer):
    def lspec(*tail):
        n = len(tail)
        return pl.BlockSpec((None, S5_GB) + tail, lambda t: (layer, t) + (0,) * n)

    def gspec(*tail):
        n = len(tail)
        return pl.BlockSpec((S5_GB,) + tail, lambda t: (t,) + (0,) * n)

    st = 2 * S5_STATE
    return pl.pallas_call(
        _s5_kernel,
        grid=(S5_GROUPS // S5_GB,),
        in_specs=[pl.BlockSpec((N_ROWS, 128), lambda t: (0, COL_U // 128 + t)),
                  pl.BlockSpec((S5_PERM, S5_PERM), lambda t: (0, 0)),
                  lspec(S5_ROW, S5_ROW), lspec(S5_ROW, st), lspec(S5_ROW, st), lspec(S5_ROW, st),
                  lspec(S5_ROW, st), lspec(1, st), lspec(1, st), lspec(DEC_BATCH, st), lspec(DEC_BATCH, st),
                  pl.BlockSpec((1, 128), lambda t: (0, t))],
        out_specs=[pl.BlockSpec((N_ROWS, 128), lambda t: (0, t)), gspec(BATCH, st), gspec(BATCH, st)],
        out_shape=[jax.ShapeDtypeStruct((N_ROWS, S5_WIDTH), F32),
                   jax.ShapeDtypeStruct((S5_GROUPS, BATCH, st), F32),
                   jax.ShapeDtypeStruct((S5_GROUPS, BATCH, st), F32)],
        scratch_shapes=([pltpu.VMEM((S5_GB, S5_ROWS, S5_ROW), BF16)]
                        + [pltpu.VMEM((S5_GB, S5_ROWS, st), F32) for _ in range(6)]
                        + [pltpu.VMEM((S5_GB, S5_ROWS, S5_ROW), F32)]),
        compiler_params=_cparams("parallel"),
        name="s5_scan",
    )(proj, perm, *tables, h0_re, h0_im, d_row.reshape(1, -1))


PRE_ROWS = 128


def _conv_block(x_ref, w_ref, r0, seq, width):
    x = x_ref[pl.ds(r0, PRE_ROWS), :]
    prev = x_ref[pl.ds(jnp.maximum(r0 - 1, 0), 1), :]
    nxt = x_ref[pl.ds(jnp.minimum(r0 + PRE_ROWS, seq - 1), 1), :]
    rid = lax.broadcasted_iota(jnp.int32, (PRE_ROWS, 1), 0)
    pos = (r0 + rid) % width
    xm = jnp.where(rid == 0, prev, pltpu.roll(x, 1, 0))
    xm = jnp.where(pos == 0, 0.0, xm)
    xp = jnp.where(rid == PRE_ROWS - 1, nxt, pltpu.roll(x, PRE_ROWS - 1, 0))
    xp = jnp.where(pos == width - 1, 0.0, xp)
    return xm * w_ref[0:1, :] + x * w_ref[1:2, :] + xp * w_ref[2:3, :]


def _round_robin(problems):
    live = list(problems)
    while live:
        nxt = []
        for p in live:
            try:
                next(p)
                nxt.append(p)
            except StopIteration:
                pass
        live = nxt


def _tri(n, lower):
    r = lax.broadcasted_iota(jnp.int32, (n, n), 0)
    c = lax.broadcasted_iota(jnp.int32, (n, n), 1)
    return (r >= c) if lower else (r <= c)


SSD_PAIR = 2 * SSD_HEADDIM
SSD_PAIRS = SSD_HEADS // 2


def _ssd_kernel(xbc_ref, sm_ref, smt_ref, z_ref, cw_ref, cb_ref, dtb_r_ref, dtb_c_ref, alog_r_ref, alog_c_ref,
                dvec_ref, ng_ref, h0_ref, yn_ref, hf_ref, xs_scr, bc_scr, dac_scr, dar_scr, dtr_scr, h_scr, y_ref,
                *, seq, width):
    nck = seq // SSD_CHUNK

    def pre(bi, carry):
        r0 = pl.multiple_of(bi * PRE_ROWS, PRE_ROWS)
        rows = pl.ds(r0, PRE_ROWS)
        xc = _silu(_conv_block(xbc_ref, cw_ref, r0, seq, width) + cb_ref[...])
        xs = xc[:, :SSD_WIDTH]
        xs_scr[rows, :] = xs
        bc_scr[rows, :] = xc[:, SSD_WIDTH:]
        y_ref[rows, :] = dvec_ref[...] * xs
        dt_c = _softplus(sm_ref[rows, 0:16] + dtb_r_ref[...])
        dac_scr[rows, :] = dt_c * (-jnp.exp(alog_r_ref[...]))
        return carry

    lax.fori_loop(0, seq // PRE_ROWS, pre, 0)
    dt_r = _softplus(smt_ref[0:16, :] + dtb_c_ref[...])
    da_r = dt_r * (-jnp.exp(alog_c_ref[...]))
    for ck in range(nck):
        dtr_scr[ck] = dt_r[:, ck * SSD_CHUNK:(ck + 1) * SSD_CHUNK]
        dar_scr[ck] = da_r[:, ck * SSD_CHUNK:(ck + 1) * SSD_CHUNK]
    h_scr[...] = h0_ref[...]

    tril = _tri(SSD_CHUNK, True)
    triu = _tri(SSD_CHUNK, False)
    tril_b = tril.astype(BF16)
    triu_b = triu.astype(BF16)
    lo_half = lax.broadcasted_iota(jnp.int32, (1, SSD_PAIR), 1) < SSD_HEADDIM

    def chunk_problem(dirs, k):
        for d in dirs:
            c = k if d == 0 else nck - 1 - k
            r0 = pl.multiple_of(c * SSD_CHUNK, SSD_CHUNK)
            rows = pl.ds(r0, SSD_CHUNK)
            mask = tril if d == 0 else triu
            ac = _dot_exact_lhs(tril_b if d == 0 else triu_b, dac_scr[rows, :])
            at = _dot_exact_rhs(dar_scr[c], triu_b if d == 0 else tril_b)
            dt_row = dtr_scr[c]
            end = SSD_CHUNK - 1 if d == 0 else 0
            bcx = bc_scr[rows, :]
            gmat = []
            for g in range(SSD_GROUPS):
                bm = bcx[:, g * SSD_STATE:(g + 1) * SSD_STATE]
                cm = bcx[:, 2 * SSD_STATE + g * SSD_STATE:2 * SSD_STATE + (g + 1) * SSD_STATE]
                gmat.append((bm.T, cm, _bdot_nt(cm, bm)))
            yield
            for pr in range(SSD_PAIRS):
                bmt, cm, gm = gmat[pr // (SSD_PAIRS // SSD_GROUPS)]
                xpair = xs_scr[rows, pr * SSD_PAIR:(pr + 1) * SSD_PAIR]
                sc, bt, es, dec, xh = [], [], [], [], []
                for half in range(2):
                    ln = d * SSD_HEADS + 2 * pr + half
                    colb = jnp.broadcast_to(ac[:, ln:ln + 1], (SSD_CHUNK, SSD_CHUNK))
                    row = at[ln:ln + 1, :]
                    dtr = dt_row[ln:ln + 1, :]
                    seg = jnp.where(mask, jnp.exp(jnp.where(mask, colb - row, 0.0)), 0.0)
                    a_end = row[:, end:end + 1]
                    sc.append(gm * seg * dtr)
                    bt.append(bmt * (jnp.exp(a_end - row) * dtr))
                    es.append(jnp.exp(colb))
                    dec.append(jnp.exp(a_end))
                    xh.append(jnp.where(lo_half if half == 0 else jnp.logical_not(lo_half), xpair, 0.0))
                xst = jnp.concatenate(xh, axis=0)
                hs = h_scr[d, pr]
                y = _bdot(jnp.concatenate(sc, axis=1), xst)
                y = y + _bdot(cm, hs) * jnp.where(lo_half, es[0], es[1])
                y_ref[rows, pr * SSD_PAIR:(pr + 1) * SSD_PAIR] += y
                h_scr[d, pr] = (hs * jnp.where(lo_half, dec[0], dec[1])
                                + _bdot(jnp.concatenate(bt, axis=1), xst))
                yield

    def chunk_step(k, carry):
        _round_robin([chunk_problem((d,), k) for d in range(N_DIR)])
        return carry

    lax.fori_loop(0, nck, chunk_step, 0)
    hf_ref[...] = h_scr[...]

    def post(bi, carry):
        rows = pl.ds(pl.multiple_of(bi * PRE_ROWS, PRE_ROWS), PRE_ROWS)
        y = y_ref[rows, :] * _silu(z_ref[rows, :])
        y = y * lax.rsqrt(jnp.mean(y * y, axis=-1, keepdims=True) + EPS) * ng_ref[...]
        yn_ref[rows, :] = y.astype(BF16)
        return carry

    lax.fori_loop(0, seq // PRE_ROWS, post, 0)


def _ssd(proj, small_t, row0, nseq, seq, width, conv_w, conv_b, dt_bias, a_log, dvec, norm_g, h0):
    blk0 = row0 // seq
    nck = seq // SSD_CHUNK
    kern = functools.partial(_ssd_kernel, seq=seq, width=width)
    full = lambda *shape: pl.BlockSpec(shape, lambda b: (0,) * len(shape))
    dtb_r = dt_bias.reshape(1, 16)
    dtb_c = dt_bias.reshape(16, 1)
    al_r = a_log.reshape(1, 16)
    al_c = a_log.reshape(16, 1)
    st_spec = pl.BlockSpec((None, N_DIR, SSD_PAIRS, SSD_STATE, SSD_PAIR), lambda b: (b, 0, 0, 0, 0))
    return pl.pallas_call(
        kern,
        grid=(nseq,),
        in_specs=[
            pl.BlockSpec((seq, SSD_CONV_DIM), lambda b: (blk0 + b, COL_XBC // SSD_CONV_DIM)),
            pl.BlockSpec((seq, SMALL_W), lambda b: (blk0 + b, COL_SMALL // SMALL_W)),
            pl.BlockSpec((32, seq), lambda b: (0, blk0 + b)),
            pl.BlockSpec((seq, SSD_WIDTH), lambda b: (blk0 + b, COL_Z // SSD_WIDTH)),
            full(3, SSD_CONV_DIM), full(1, SSD_CONV_DIM), full(1, 16), full(16, 1), full(1, 16), full(16, 1),
            full(1, SSD_WIDTH), full(1, SSD_WIDTH), st_spec,
        ],
        out_specs=[pl.BlockSpec((seq, SSD_WIDTH), lambda b: (b, 0)), st_spec],
        out_shape=[jax.ShapeDtypeStruct((nseq * seq, SSD_WIDTH), BF16),
                   jax.ShapeDtypeStruct((nseq, N_DIR, SSD_PAIRS, SSD_STATE, SSD_PAIR), F32)],
        scratch_shapes=[
            pltpu.VMEM((seq, SSD_WIDTH), F32), pltpu.VMEM((seq, 4 * SSD_STATE), F32),
            pltpu.VMEM((seq, 16), F32), pltpu.VMEM((nck, 16, SSD_CHUNK), F32),
            pltpu.VMEM((nck, 16, SSD_CHUNK), F32),
            pltpu.VMEM((N_DIR, SSD_PAIRS, SSD_STATE, SSD_PAIR), F32),
            pltpu.VMEM((seq, SSD_WIDTH), F32),
        ],
        compiler_params=_cparams("parallel"),
        name="ssd_seq%d" % seq,
    )(proj, proj, small_t, proj, conv_w, conv_b.reshape(1, -1), dtb_r, dtb_c, al_r, al_c, dvec,
      norm_g.reshape(1, -1), h0)


def _ssd_state_to_pairs(h):
    lead = h.shape[:-3]
    t = h.reshape(lead + (SSD_PAIRS, 2, SSD_HEADDIM, SSD_STATE))
    t = jnp.moveaxis(t, -1, -3)
    return t.reshape(lead + (SSD_PAIRS, SSD_STATE, SSD_PAIR))


def _ssd_state_from_pairs(hp):
    lead = hp.shape[:-3]
    t = hp.reshape(lead + (SSD_PAIRS, SSD_STATE, 2, SSD_HEADDIM))
    t = jnp.moveaxis(t, -3, -1)
    return t.reshape(lead + (SSD_HEADS, SSD_HEADDIM, SSD_STATE))


DN_ST = DN_HEADS * DN_CHUNK
DN_PAR = 4


def _dn_kernel(qkv_ref, sm_ref, smt_ref, dg_ref, cw_ref, dtb_r_ref, alog_r_ref, dtb_c_ref, alog_c_ref, ng_ref,
               s0_ref, on_ref, sf_ref, q_scr, k_scr, v_scr, b_scr, g_scr, s_scr, u_scr, wq_scr, a_scr, kd_scr,
               gl_scr, grow_scr, o_ref, *, seq, width):
    nck = seq // DN_CHUNK
    g_rows = -jnp.exp(alog_c_ref[...]) * _softplus(smt_ref[24:32, :] + dtb_c_ref[...])
    for ck in range(nck):
        grow_scr[ck] = g_rows[:, ck * DN_CHUNK:(ck + 1) * DN_CHUNK]

    def pre(bi, carry):
        r0 = pl.multiple_of(bi * PRE_ROWS, PRE_ROWS)
        rows = pl.ds(r0, PRE_ROWS)
        xc = _silu(_conv_block(qkv_ref, cw_ref, r0, seq, width))
        for h in range(DN_HEADS):
            q = xc[:, h * DN_DK:(h + 1) * DN_DK]
            k = xc[:, DN_QK + h * DN_DK:DN_QK + (h + 1) * DN_DK]
            q_scr[rows, h * DN_DK:(h + 1) * DN_DK] = (
                q * lax.rsqrt(jnp.sum(q * q, axis=-1, keepdims=True) + EPS) * (DN_DK ** -0.5))
            k_scr[rows, h * DN_DK:(h + 1) * DN_DK] = (
                k * lax.rsqrt(jnp.sum(k * k, axis=-1, keepdims=True) + EPS))
        v_scr[rows, :] = xc[:, 2 * DN_QK:]
        b_scr[rows, :] = _sigmoid(sm_ref[rows, 16:24])
        g_scr[rows, :] = -jnp.exp(alog_r_ref[...]) * _softplus(sm_ref[rows, 24:32] + dtb_r_ref[...])
        o_ref[rows, :] = jnp.zeros((PRE_ROWS, DN_V), F32)
        return carry

    lax.fori_loop(0, seq // PRE_ROWS, pre, 0)
    s_scr[...] = s0_ref[...]

    r = lax.broadcasted_iota(jnp.int32, (DN_ST, DN_ST), 0)
    c = lax.broadcasted_iota(jnp.int32, (DN_ST, DN_ST), 1)
    same = (r // DN_CHUNK) == (c // DN_CHUNK)
    eye = (r == c).astype(F32)
    tril64 = _tri(DN_CHUNK, True).astype(BF16)
    triu64 = _tri(DN_CHUNK, False).astype(BF16)
    tj = lax.broadcasted_iota(jnp.int32, (DN_CHUNK, DN_ST), 0)
    ti = lax.broadcasted_iota(jnp.int32, (DN_CHUNK, DN_ST), 1) % DN_CHUNK
    cum_f = (tj <= ti).astype(BF16)
    cum_b = (tj >= ti).astype(BF16)

    def chunk_problem(d, ci):
        r0 = pl.multiple_of(ci * DN_CHUNK, DN_CHUNK)
        rows = pl.ds(r0, DN_CHUNK)
        incl = jnp.logical_and(same, (r >= c) if d == 0 else (r <= c))
        strict = jnp.logical_and(same, (r > c) if d == 0 else (r < c))
        gc_c = _dot_exact_lhs(tril64 if d == 0 else triu64, g_scr[rows, :])
        gc_t = _dot_exact_rhs(grow_scr[ci], cum_f if d == 0 else cum_b)
        gc_r = jnp.concatenate(
            [jnp.broadcast_to(gc_t[d * DN_HEADS + h:d * DN_HEADS + h + 1, :], (DN_CHUNK, DN_ST))
             for h in range(DN_HEADS)], axis=0)
        beta = b_scr[rows, :]
        end_row = DN_CHUNK - 1 if d == 0 else 0
        k_st, q_st, v_st, bt_st, gc_st, gl_st = [], [], [], [], [], []
        for h in range(DN_HEADS):
            ln = d * DN_HEADS + h
            k_st.append(k_scr[rows, h * DN_DK:(h + 1) * DN_DK])
            q_st.append(q_scr[rows, h * DN_DK:(h + 1) * DN_DK])
            v_st.append(v_scr[rows, h * DN_DV:(h + 1) * DN_DV])
            bt_st.append(beta[:, ln:ln + 1])
            col = gc_c[:, ln:ln + 1]
            gc_st.append(col)
            gl_st.append(col[end_row:end_row + 1, :])
        kst = jnp.concatenate(k_st, axis=0)
        qst = jnp.concatenate(q_st, axis=0)
        vst = jnp.concatenate(v_st, axis=0)
        bst = jnp.concatenate(bt_st, axis=0)
        gst = jnp.concatenate(gc_st, axis=0)
        decay = jnp.where(incl, jnp.exp(jnp.where(incl, gst - gc_r, 0.0)), 0.0)
        kb = kst * bst
        kstb = kst.astype(BF16)
        m = jnp.where(strict, _bdot_nt(kb, kstb) * decay, 0.0)
        attn = jnp.where(incl, _bdot_nt(qst, kstb) * decay, 0.0)
        a_scr[d, ci] = attn.astype(BF16)
        yield
        def wide(bd):
            return sum(bd[h * DN_CHUNK:(h + 1) * DN_CHUNK, :] for h in range(DN_HEADS))

        def block_diag(w):
            return jnp.where(same, jnp.concatenate([w] * DN_HEADS, axis=0), 0.0)

        m_w = wide(m)
        t_w = wide(eye) - m_w
        p_w = jnp.dot(m_w.astype(BF16), m.astype(BF16), preferred_element_type=F32)
        yield
        for lvl in range(5):
            p_bd = block_diag(p_w).astype(BF16)
            if lvl < 4:
                both = jnp.dot(jnp.concatenate([t_w, p_w], axis=0).astype(BF16), p_bd,
                               preferred_element_type=F32)
                t_w = t_w + both[:DN_CHUNK]
                p_w = both[DN_CHUNK:]
            else:
                t_w = t_w + jnp.dot(t_w.astype(BF16), p_bd, preferred_element_type=F32)
            yield
        tb = block_diag(t_w).astype(BF16)
        rhs = jnp.concatenate([vst * bst, kb * jnp.exp(gst)], axis=1)
        x0 = jnp.dot(tb, rhs.astype(BF16), preferred_element_type=F32)
        yield
        res = rhs - x0 - _dot3(m, x0)
        yield
        uw = x0 + jnp.dot(tb, res.astype(BF16), preferred_element_type=F32)
        yield
        qg = qst * jnp.exp(gst)
        u_scr[d, ci] = uw[:, :DN_DV]
        kdec = []
        for h in range(DN_HEADS):
            hs = slice(h * DN_CHUNK, (h + 1) * DN_CHUNK)
            wq_scr[d, ci, h] = jnp.concatenate([uw[hs, DN_DV:], qg[hs, :]], axis=0).astype(BF16)
            gl = gl_st[h]
            kdec.append(k_st[h] * jnp.exp(gl - gc_st[h]))
            gl_scr[d, ci, h:h + 1, :] = jnp.broadcast_to(jnp.exp(gl), (1, DN_DV))
        kd_scr[d, ci] = jnp.concatenate(kdec, axis=0).astype(BF16)

    def chunk_step(kk, carry):
        _round_robin([chunk_problem(d, kk * DN_PAR + j) for j in range(DN_PAR) for d in range(N_DIR)])
        return carry

    lax.fori_loop(0, nck // DN_PAR, chunk_step, 0)

    def state_problem(d, ci):
        rows = pl.ds(pl.multiple_of(ci * DN_CHUNK, DN_CHUNK), DN_CHUNK)
        s_old, vnew, qs_all = [], [], []
        for h in range(DN_HEADS):
            hs = slice(h * DN_CHUNK, (h + 1) * DN_CHUNK)
            s_h = s_scr[d, h]
            ws = jnp.dot(wq_scr[d, ci, h], s_h.astype(BF16), preferred_element_type=F32)
            s_old.append(s_h)
            vnew.append((u_scr[d, ci, hs, :] - ws[:DN_CHUNK]).astype(BF16))
            qs_all.append(ws[DN_CHUNK:])
        yield
        o_st = jnp.concatenate(qs_all, axis=0) + jnp.dot(
            a_scr[d, ci], jnp.concatenate(vnew, axis=0), preferred_element_type=F32)
        for h in range(DN_HEADS):
            hs = slice(h * DN_CHUNK, (h + 1) * DN_CHUNK)
            s_scr[d, h] = s_old[h] * gl_scr[d, ci, h:h + 1, :] + lax.dot_general(
                kd_scr[d, ci, hs, :], vnew[h], (((0,), (0,)), ((), ())), preferred_element_type=F32)
        yield
        for h in range(DN_HEADS):
            hs = slice(h * DN_CHUNK, (h + 1) * DN_CHUNK)
            o_ref[rows, h * DN_DV:(h + 1) * DN_DV] += o_st[hs, :]

    def state_step(kk, carry):
        _round_robin([state_problem(0, kk), state_problem(1, nck - 1 - kk)])
        return carry

    lax.fori_loop(0, nck, state_step, 0, unroll=4)
    sf_ref[...] = s_scr[...]

    def post(bi, carry):
        rows = pl.ds(pl.multiple_of(bi * PRE_ROWS, PRE_ROWS), PRE_ROWS)
        for h in range(DN_HEADS):
            cols = slice(h * DN_DV, (h + 1) * DN_DV)
            oh = o_ref[rows, cols]
            oh = oh * lax.rsqrt(jnp.mean(oh * oh, axis=-1, keepdims=True) + EPS) * ng_ref[...]
            on_ref[rows, cols] = (oh * _silu(dg_ref[rows, cols])).astype(BF16)
        return carry

    lax.fori_loop(0, seq // PRE_ROWS, post, 0)


def _dn(proj, small_t, row0, nseq, seq, width, conv_w, dt_bias, a_log, norm_g, s0):
    blk0 = row0 // seq
    nck = seq // DN_CHUNK
    kern = functools.partial(_dn_kernel, seq=seq, width=width)
    full = lambda *shape: pl.BlockSpec(shape, lambda b: (0,) * len(shape))
    dtb_r = dt_bias.reshape(1, 8)
    al_r = a_log.reshape(1, 8)
    dtb_c = dt_bias.reshape(8, 1)
    al_c = a_log.reshape(8, 1)
    return pl.pallas_call(
        kern,
        grid=(nseq,),
        in_specs=[
            pl.BlockSpec((seq, DN_CONV_DIM), lambda b: (blk0 + b, COL_QKV // DN_CONV_DIM)),
            pl.BlockSpec((seq, SMALL_W), lambda b: (blk0 + b, COL_SMALL // SMALL_W)),
            pl.BlockSpec((32, seq), lambda b: (0, blk0 + b)),
            pl.BlockSpec((seq, DN_V), lambda b: (blk0 + b, COL_DNG // DN_V)),
            full(3, DN_CONV_DIM), full(1, 8), full(1, 8), full(8, 1), full(8, 1), full(1, DN_DV),
            pl.BlockSpec((None, N_DIR, DN_HEADS, DN_DK, DN_DV), lambda b: (b, 0, 0, 0, 0)),
        ],
        out_specs=[
            pl.BlockSpec((seq, DN_V), lambda b: (b, 0)),
            pl.BlockSpec((None, N_DIR, DN_HEADS, DN_DK, DN_DV), lambda b: (b, 0, 0, 0, 0)),
        ],
        out_shape=[jax.ShapeDtypeStruct((nseq * seq, DN_V), BF16),
                   jax.ShapeDtypeStruct((nseq, N_DIR, DN_HEADS, DN_DK, DN_DV), F32)],
        scratch_shapes=[
            pltpu.VMEM((seq, DN_QK), F32), pltpu.VMEM((seq, DN_QK), F32), pltpu.VMEM((seq, DN_V), F32),
            pltpu.VMEM((seq, 8), F32), pltpu.VMEM((seq, 8), F32),
            pltpu.VMEM((N_DIR, DN_HEADS, DN_DK, DN_DV), F32),
            pltpu.VMEM((N_DIR, nck, DN_ST, DN_DV), F32),
            pltpu.VMEM((N_DIR, nck, DN_HEADS, 2 * DN_CHUNK, DN_DK), BF16),
            pltpu.VMEM((N_DIR, nck, DN_ST, DN_ST), BF16),
            pltpu.VMEM((N_DIR, nck, DN_ST, DN_DK), BF16),
            pltpu.VMEM((N_DIR, nck, 8, DN_DV), F32),
            pltpu.VMEM((nck, 8, DN_CHUNK), F32),
            pltpu.VMEM((seq, DN_V), F32),
        ],
        compiler_params=_cparams("parallel"),
        name="dn_seq%d" % seq,
    )(proj, proj, small_t, proj, conv_w, dtb_r, al_r, dtb_c, al_c, norm_g.reshape(1, -1), s0)


MRG_TM = 512
MRG_TD = 1024


def _merge_kernel(x_ref, sh_ref, sc_ref, gt_ref, ng_ref, y5_ref, ysc_ref, ysl_ref,
                  odc_ref, odl_ref, wg0_ref, wg1_ref, wg2_ref, glu_ref, swo_ref,
                  dwo_ref, wo_ref, o_ref, h_scr, a_scr, b_scr, c_scr, acc_scr):
    i = pl.program_id(0)
    j = pl.program_id(1)
    grp = _row_group(i, MRG_TM)

    @pl.when(j == 0)
    def _():
        is_ctx = i < N_CTX // MRG_TM
        h = _norm_mod(x_ref[...], ng_ref[...], sc_ref[pl.ds(grp, 1), :], sh_ref[pl.ds(grp, 1), :])
        h_scr[...] = h.astype(BF16)
        a_scr[...] = y5_ref[...].astype(BF16)
        b_scr[...] = jnp.where(is_ctx, ysc_ref[...], ysl_ref[...])
        c_scr[...] = jnp.where(is_ctx, odc_ref[...], odl_ref[...])
        acc_scr[...] = jnp.zeros_like(acc_scr)

    h = h_scr[...]
    nt = (((1,), (1,)), ((), ()))
    gate = lambda w_ref: _sigmoid(lax.dot_general(h, w_ref[...], nt, preferred_element_type=F32))
    g5 = a_scr[...]
    br_a = (jnp.dot(g5, glu_ref[0], preferred_element_type=F32)
            * _sigmoid(jnp.dot(g5, glu_ref[1], preferred_element_type=F32)))
    br_b = jnp.dot(b_scr[...], swo_ref[...], preferred_element_type=F32)
    br_c = jnp.dot(c_scr[...], dwo_ref[...], preferred_element_type=F32)
    merged = gate(wg0_ref) * br_a + gate(wg1_ref) * br_b + gate(wg2_ref) * br_c
    acc_scr[...] += jnp.dot(merged.astype(BF16), wo_ref[...], preferred_element_type=F32)

    @pl.when(j == pl.num_programs(1) - 1)
    def _():
        o_ref[...] = x_ref[...] + gt_ref[pl.ds(grp, 1), :] * acc_scr[...]


def _merge(x, mod, norm_g_row, w_packed, y5, ys_c, ys_l, od_c, od_l, glu_b, swo_b, dwo_b, wo_b, layer):
    nctx = N_CTX // MRG_TM
    rowblk = lambda w, col: pl.BlockSpec((MRG_TM, w), lambda i, j: (i, col // w))
    ctxblk = lambda w: pl.BlockSpec((MRG_TM, w), lambda i, j: (jnp.minimum(i, nctx - 1), 0))
    latblk = lambda w: pl.BlockSpec((MRG_TM, w), lambda i, j: (jnp.maximum(i - nctx, 0), 0))
    row1 = lambda w: pl.BlockSpec((1, w), lambda i, j: (0, 0))
    modblk = lambda k: pl.BlockSpec((8, D_MODEL), lambda i, j: (0, k))
    gateblk = lambda k: pl.BlockSpec((None, MRG_TD, D_MODEL),
                                     lambda i, j: (layer, (COL_GATES + k * D_MODEL) // MRG_TD + j, 0))
    return pl.pallas_call(
        _merge_kernel,
        grid=(N_ROWS // MRG_TM, D_MODEL // MRG_TD),
        in_specs=[
            rowblk(D_MODEL, 0), modblk(3), modblk(4), modblk(5), row1(D_MODEL),
            rowblk(S5_WIDTH, 0),
            ctxblk(SSD_WIDTH), latblk(SSD_WIDTH), ctxblk(DN_V), latblk(DN_V),
            gateblk(0), gateblk(1), gateblk(2),
            pl.BlockSpec((None, 2, S5_WIDTH, MRG_TD), lambda i, j: (layer, 0, 0, j)),
            pl.BlockSpec((None, SSD_WIDTH, MRG_TD), lambda i, j: (layer, 0, j)),
            pl.BlockSpec((None, DN_V, MRG_TD), lambda i, j: (layer, 0, j)),
            pl.BlockSpec((None, MRG_TD, D_MODEL), lambda i, j: (layer, j, 0)),
        ],
        out_specs=rowblk(D_MODEL, 0),
        out_shape=jax.ShapeDtypeStruct((N_ROWS, D_MODEL), F32),
        scratch_shapes=[pltpu.VMEM((MRG_TM, D_MODEL), BF16), pltpu.VMEM((MRG_TM, S5_WIDTH), BF16),
                        pltpu.VMEM((MRG_TM, SSD_WIDTH), BF16), pltpu.VMEM((MRG_TM, DN_V), BF16),
                        pltpu.VMEM((MRG_TM, D_MODEL), F32)],
        compiler_params=_cparams("parallel", "arbitrary"),
        name="merge",
    )(x, mod, mod, mod, norm_g_row, y5, ys_c, ys_l, od_c, od_l,
      w_packed, w_packed, w_packed, glu_b, swo_b, dwo_b, wo_b)


FIN_TM = 1024


def _final_norm_kernel(x_ref, g_ref, oc_ref, ol_ref):
    i = pl.program_id(0)
    x = x_ref[...]
    y = x * lax.rsqrt(jnp.mean(x * x, axis=-1, keepdims=True) + EPS) * g_ref[...]

    @pl.when(i < N_CTX // FIN_TM)
    def _():
        oc_ref[...] = y

    @pl.when(i >= N_CTX // FIN_TM)
    def _():
        ol_ref[...] = y


def _final_norm(x, g):
    nctx = N_CTX // FIN_TM
    return pl.pallas_call(
        _final_norm_kernel,
        grid=(N_ROWS // FIN_TM,),
        in_specs=[pl.BlockSpec((FIN_TM, D_MODEL), lambda i: (i, 0)), pl.BlockSpec((1, D_MODEL), lambda i: (0, 0))],
        out_specs=[pl.BlockSpec((FIN_TM, D_MODEL), lambda i: (jnp.minimum(i, nctx - 1), 0)),
                   pl.BlockSpec((FIN_TM, D_MODEL), lambda i: (jnp.maximum(i - nctx, 0), 0))],
        out_shape=[jax.ShapeDtypeStruct((N_CTX, D_MODEL), F32), jax.ShapeDtypeStruct((N_LAT, D_MODEL), F32)],
        compiler_params=_cparams("arbitrary"),
        name="final_norm",
    )(x, g.reshape(1, -1))


def kernel(x_prompt, x_sample, state_s5_re, state_s5_im, state_ssd, state_dn, c, c_ctx, ada_w, ada_b, norm_g, ffn_wi, ffn_wo, w_in, s5_lam_re, s5_lam_im, s5_log_dt, s5_b_re, s5_b_im, s5_c_re, s5_c_im, s5_d, s5_glu, ssd_conv_w, ssd_conv_b, ssd_dt_bias, ssd_a_log, ssd_d, ssd_norm_g, ssd_w_out, dn_conv_w, dn_dt_bias, dn_a_log, dn_norm_g, dn_w_out, w_out, final_norm_g):
    x = jnp.concatenate([x_prompt.reshape(N_CTX, D_MODEL), x_sample.reshape(N_LAT, D_MODEL)], axis=0)
    cond8 = jnp.concatenate([c_ctx[None, :], c, jnp.zeros((8 - 1 - DEC_BATCH, D_MODEL), F32)], axis=0)
    mods = _ada_mods(cond8, ada_w, ada_b)

    s5_tables = _s5_prep(s5_lam_re, s5_lam_im, s5_log_dt, s5_b_re, s5_b_im, s5_c_re, s5_c_im)
    s5_perm = _s5_perm()
    w_packed = _repack_w_in(w_in)
    glu_b = s5_glu.astype(BF16)
    swo_b = ssd_w_out.astype(BF16)
    dwo_b = dn_w_out.astype(BF16)
    wo_b = w_out.astype(BF16)
    ssd_d_rows = jnp.repeat(ssd_d, SSD_HEADDIM, axis=1).reshape(DEPTH, 1, SSD_WIDTH)

    def s5_h0(state):
        return state.transpose(1, 3, 0, 2, 4).reshape(DEPTH, S5_GROUPS, DEC_BATCH, 2 * S5_STATE)

    h0_re = s5_h0(state_s5_re)
    h0_im = s5_h0(state_s5_im)
    ssd_h0_lat = _ssd_state_to_pairs(jnp.swapaxes(state_ssd, 0, 1))
    ssd_h0_ctx = jnp.zeros((BATCH, N_DIR, SSD_PAIRS, SSD_STATE, SSD_PAIR), F32)
    dn_s0_lat = jnp.swapaxes(state_dn, 0, 1)
    dn_s0_ctx = jnp.zeros((BATCH, N_DIR, DN_HEADS, DN_DK, DN_DV), F32)

    new_s5_re, new_s5_im, new_ssd, new_dn = [], [], [], []
    for l in range(DEPTH):
        mod = mods[l]
        x = _ffn(x, mod, norm_g[l, 0:1], ffn_wi, ffn_wo, l, 0)
        proj, small_t = _inproj(x, mod, norm_g[l, 1:2], w_packed, l)

        y5, f_re, f_im = _s5_scan(proj, s5_perm, s5_tables, h0_re, h0_im, s5_d[l], l)
        new_s5_re.append(f_re.reshape(S5_GROUPS, BATCH, N_DIR, S5_STATE).transpose(1, 2, 0, 3))
        new_s5_im.append(f_im.reshape(S5_GROUPS, BATCH, N_DIR, S5_STATE).transpose(1, 2, 0, 3))

        ys_c, hs_c = _ssd(proj, small_t, 0, BATCH, SEQ, SEQ, ssd_conv_w[l], ssd_conv_b[l], ssd_dt_bias[l],
                          ssd_a_log[l], ssd_d_rows[l], ssd_norm_g[l], ssd_h0_ctx)
        ys_l, _ = _ssd(proj, small_t, N_CTX, DEC_BATCH, DEC_SEQ, GRID_W, ssd_conv_w[l], ssd_conv_b[l],
                       ssd_dt_bias[l], ssd_a_log[l], ssd_d_rows[l], ssd_norm_g[l], ssd_h0_lat[l])
        new_ssd.append(_ssd_state_from_pairs(hs_c))
        od_c, sd_c = _dn(proj, small_t, 0, BATCH, SEQ, SEQ, dn_conv_w[l], dn_dt_bias[l], dn_a_log[l],
                         dn_norm_g[l], dn_s0_ctx)
        od_l, _ = _dn(proj, small_t, N_CTX, DEC_BATCH, DEC_SEQ, GRID_W, dn_conv_w[l], dn_dt_bias[l],
                      dn_a_log[l], dn_norm_g[l], dn_s0_lat[l])
        new_dn.append(sd_c)

        x = _merge(x, mod, norm_g[l, 1:2], w_packed, y5, ys_c, ys_l, od_c, od_l, glu_b, swo_b,
                   dwo_b, wo_b, l)
        x = _ffn(x, mod, norm_g[l, 2:3], ffn_wi, ffn_wo, l, 1)

    y_ctx, y_lat = _final_norm(x, final_norm_g)
    y_prompt = y_ctx.reshape(BATCH, SEQ, D_MODEL)
    y_sample = y_lat.reshape(DEC_BATCH, DEC_SEQ, D_MODEL)
    return (y_prompt, y_sample, jnp.stack(new_s5_re, axis=1), jnp.stack(new_s5_im, axis=1),
            jnp.stack(new_ssd, axis=1), jnp.stack(new_dn, axis=1))
```
